```python
import math
import jax, jax.numpy as jnp
from jax import lax
import numpy as np

D_MODEL = 1024
BATCH = 16
SEQ = 2048
DEPTH = 4

NSA_HEADS = 16
NSA_KV_GROUPS = 4
HEAD_DIM = 64
NSA_HPG = NSA_HEADS // NSA_KV_GROUPS
CMP_BLOCK = 32
CMP_STRIDE = 16
CMP_HIDDEN = 256
SEL_BLOCK = 64
SEL_TOPK = 8
WINDOW = 512
NSA_QBLOCK = 64
ROPE_THETA = 500000.0
ROPE_DIM = HEAD_DIM // 4
NSA_Q_WIDTH = NSA_HEADS * HEAD_DIM
NSA_KV_WIDTH = NSA_KV_GROUPS * HEAD_DIM
NSA_IN_WIDTH = NSA_Q_WIDTH + 6 * NSA_KV_WIDTH + 3 * NSA_HEADS

SSM_EXPAND = 2
SSM_D_INNER = SSM_EXPAND * D_MODEL
SSM_HEAD_DIM = 64
SSM_HEADS = SSM_D_INNER // SSM_HEAD_DIM
SSM_GROUPS = 4
SSM_HPG = SSM_HEADS // SSM_GROUPS
SSM_STATE = 128
SSM_CONV = 4
SSM_CHUNK = 128
SSM_CONV_DIM = SSM_D_INNER + 2 * SSM_GROUPS * SSM_STATE
SSM_IN_WIDTH = SSM_D_INNER + SSM_CONV_DIM + SSM_HEADS

FFN_HIDDEN = -(-8 * D_MODEL // (3 * 256)) * 256

N_NSA_LAYERS = (DEPTH + 1) // 2
N_SSM_LAYERS = DEPTH // 2
EPS = 1e-6

kernel_name = "nsa_mamba2_interleaved_hybrid"


def rms_norm(x, w):
    xf = x.astype(jnp.float32)
    xf = xf * lax.rsqrt(jnp.mean(xf * xf, axis=-1, keepdims=True) + EPS)
    return (xf * w.astype(jnp.float32)).astype(x.dtype)


def partial_rope(x, pos):
    half = ROPE_DIM // 2
    inv = jnp.power(jnp.float32(ROPE_THETA), -jnp.arange(half, dtype=jnp.float32) / half)
    ang = pos.astype(jnp.float32)[..., None] * inv
    ang = ang.reshape(ang.shape[:2] + (1,) * (x.ndim - 3) + (half,))
    cos, sin = jnp.cos(ang), jnp.sin(ang)
    xr = x[..., :ROPE_DIM].astype(jnp.float32)
    x1, x2 = xr[..., :half], xr[..., half:]
    rot = jnp.concatenate([x1 * cos - x2 * sin, x2 * cos + x1 * sin], axis=-1).astype(x.dtype)
    return jnp.concatenate([rot, x[..., ROPE_DIM:]], axis=-1)


def masked_softmax(s, mask):
    s = jnp.where(mask, s.astype(jnp.float32), -1e30)
    return jnp.where(mask, jax.nn.softmax(s, axis=-1), 0.0)


def nsa_mixer(h, pos, w_in, q_norm, k_norm, cmp_pe, cmp_w1, cmp_b1, cmp_w2, w_out):
    B, S, _ = h.shape
    G, HP, dh, QB = NSA_KV_GROUPS, NSA_HPG, HEAD_DIM, NSA_QBLOCK
    scale = dh ** -0.5
    cuts = list(np.cumsum([NSA_Q_WIDTH] + [NSA_KV_WIDTH] * 6))
    q, k_c, v_c, k_s, v_s, k_w, v_w, g = jnp.split(h @ w_in, cuts, axis=-1)
    q = partial_rope(rms_norm(q.reshape(B, S, G, HP, dh), q_norm), pos)

    nc = (S - CMP_BLOCK) // CMP_STRIDE + 1
    starts = np.arange(nc) * CMP_STRIDE
    blk_idx = starts[:, None] + np.arange(CMP_BLOCK)[None, :]
    cmp_end = starts + CMP_BLOCK - 1

    def compress(t, pe, w1, b1, w2):
        tb = t[:, blk_idx] + pe[:, None, :]
        tb = jnp.moveaxis(tb, 3, 2).reshape(B, nc, G, CMP_BLOCK * dh)
        return jax.nn.silu(tb @ w1 + b1) @ w2

    kc = compress(k_c.reshape(B, S, G, dh), cmp_pe[0], cmp_w1[0], cmp_b1[0], cmp_w2[0])
    vc = compress(v_c.reshape(B, S, G, dh), cmp_pe[1], cmp_w1[1], cmp_b1[1], cmp_w2[1])
    kc = partial_rope(rms_norm(kc, k_norm[0]), pos[:, cmp_end])
    s_c = jnp.einsum('bqghd,bcgd->bghqc', q, kc).astype(jnp.float32) * scale
    mask_c = jnp.asarray(cmp_end[None, :] <= np.arange(S)[:, None])
    p_c = masked_softmax(s_c, mask_c)
    o_c = jnp.einsum('bghqc,bcgd->bqghd', p_c.astype(vc.dtype), vc)

    sel_nb = S // SEL_BLOCK
    n_sel = min(SEL_TOPK, sel_nb)
    js = np.arange(sel_nb)[None, :] * SEL_BLOCK
    overlap = ((starts[:, None] < js + SEL_BLOCK) & (starts[:, None] + CMP_BLOCK > js)).astype(np.float32)
    imp = jnp.einsum('bghqc,cj->bgqj', p_c, jnp.asarray(overlap))
    t_blk = np.arange(S)[:, None] // SEL_BLOCK
    jb = np.arange(sel_nb)[None, :]
    forced = (jb == 0) | (jb == t_blk) | (jb == t_blk - 1)
    imp = jnp.where(forced, jnp.inf, jnp.where(jb > t_blk, -jnp.inf, imp))
    _, sel_idx = lax.top_k(imp, n_sel)

    def to_blocks(t):
        return t.reshape(B, sel_nb, SEL_BLOCK, G, dh).transpose(0, 3, 1, 2, 4)
    ks = partial_rope(rms_norm(k_s.reshape(B, S, G, dh), k_norm[1]), pos)
    ks_b, vs_b = to_blocks(ks), to_blocks(v_s.reshape(B, S, G, dh))
    kw = partial_rope(rms_norm(k_w.reshape(B, S, G, dh), k_norm[2]), pos)
    pad = ((0, 0), (WINDOW, 0), (0, 0), (0, 0))
    kw_pad, vw_pad = jnp.pad(kw, pad), jnp.pad(v_w.reshape(B, S, G, dh), pad)

    nqb = S // QB
    q_blocks = jnp.moveaxis(q.reshape(B, nqb, QB, G, HP, dh), 1, 0)
    idx_blocks = jnp.moveaxis(sel_idx.reshape(B, G, nqb, QB, n_sel), 2, 0)
    q_starts = jnp.arange(nqb, dtype=jnp.int32) * QB
    bi = jnp.arange(B)[:, None, None, None]
    gi = jnp.arange(G)[None, :, None, None]

    def block_step(args):
        q0, qb, ib = args
        tq = q0 + jnp.arange(QB, dtype=jnp.int32)
        kg = ks_b[bi, gi, ib]
        vg = vs_b[bi, gi, ib]
        s = jnp.einsum('bqghd,bgqnld->bghqnl', qb, kg).astype(jnp.float32) * scale
        kp = ib[..., None] * SEL_BLOCK + jnp.arange(SEL_BLOCK)
        m = (kp <= tq[:, None, None])[:, :, None].reshape(B, G, 1, QB, n_sel * SEL_BLOCK)
        p = masked_softmax(s.reshape(B, G, HP, QB, n_sel * SEL_BLOCK), m)
        o_s = jnp.einsum('bghqk,bgqkd->bqghd', p.astype(vg.dtype),
                         vg.reshape(B, G, QB, n_sel * SEL_BLOCK, dh))
        kwin = lax.dynamic_slice_in_dim(kw_pad, q0, WINDOW + QB, axis=1)
        vwin = lax.dynamic_slice_in_dim(vw_pad, q0, WINDOW + QB, axis=1)
        kpos = q0 - WINDOW + jnp.arange(WINDOW + QB, dtype=jnp.int32)
        mw = ((kpos[None, :] <= tq[:, None]) & (kpos[None, :] > tq[:, None] - WINDOW)
              & (kpos[None, :] >= 0))
        sw = jnp.einsum('bqghd,bkgd->bghqk', qb, kwin).astype(jnp.float32) * scale
        pw = masked_softmax(sw, mw)
        o_w = jnp.einsum('bghqk,bkgd->bqghd', pw.astype(vwin.dtype), vwin)
        return o_s, o_w

    o_s, o_w = lax.map(block_step, (q_starts, q_blocks, idx_blocks))
    o_s = jnp.moveaxis(o_s, 0, 1).reshape(B, S, G, HP, dh)
    o_w = jnp.moveaxis(o_w, 0, 1).reshape(B, S, G, HP, dh)

    gates = jax.nn.sigmoid(g.astype(jnp.float32)).reshape(B, S, G, HP, 3, 1)
    o = (gates[..., 0, :] * o_c + gates[..., 1, :] * o_s + gates[..., 2, :] * o_w).astype(h.dtype)
    return o.reshape(B, S, NSA_Q_WIDTH) @ w_out


def ssd_mixer(h, w_in, conv_w, conv_b, dt_bias, a_log, d_skip, norm_w, w_out):
    B, S, _ = h.shape
    G, HP, P, N, Q = SSM_GROUPS, SSM_HPG, SSM_HEAD_DIM, SSM_STATE, SSM_CHUNK
    f32 = jnp.float32
    z, xbc, dt = jnp.split(h @ w_in, [SSM_D_INNER, SSM_D_INNER + SSM_CONV_DIM], axis=-1)
    xbc = lax.conv_general_dilated(xbc, conv_w[:, None, :], window_strides=(1,),
                                   padding=[(SSM_CONV - 1, 0)],
                                   dimension_numbers=('NWC', 'WIO', 'NWC'),
                                   feature_group_count=SSM_CONV_DIM)
    xbc = jax.nn.silu(xbc + conv_b)
    xs, bm, cm = jnp.split(xbc, [SSM_D_INNER, SSM_D_INNER + G * N], axis=-1)
    xs = xs.reshape(B, S, G, HP, P)
    bm = bm.reshape(B, S, G, N)
    cm = cm.reshape(B, S, G, N)
    dt = jax.nn.softplus(dt.astype(f32) + dt_bias.astype(f32)).reshape(B, S, G, HP)
    A = -jnp.exp(a_log.astype(f32)).reshape(G, HP)

    nch = S // Q
    def chunks(t):
        return jnp.moveaxis(t.reshape((B, nch, Q) + t.shape[2:]), 1, 0)
    tril = jnp.asarray(np.tril(np.ones((Q, Q), dtype=bool)))[None, :, :, None, None]

    def step(state, inp):
        xc, dtc, bc, cc = inp
        bc, cc = bc.astype(f32), cc.astype(f32)
        cum = jnp.cumsum(dtc * A, axis=1)
        seg = cum[:, :, None] - cum[:, None]
        L = jnp.exp(jnp.where(tril, seg, -jnp.inf))
        xdt = xc.astype(f32) * dtc[..., None]
        cb = jnp.einsum('btgn,bsgn->btsg', cc, bc)
        y_diag = jnp.einsum('btsgh,bsghp->btghp', cb[..., None] * L, xdt)
        y_off = jnp.einsum('btgn,bghpn->btghp', cc, state) * jnp.exp(cum)[..., None]
        decay = jnp.exp(cum[:, -1:] - cum)
        new_state = (state * jnp.exp(cum[:, -1])[..., None, None]
                     + jnp.einsum('bsgn,bsghp->bghpn', bc, xdt * decay[..., None]))
        return new_state, y_diag + y_off

    state0 = jnp.zeros((B, G, HP, P, N), f32)
    _, y = lax.scan(step, state0, (chunks(xs), chunks(dt), chunks(bm), chunks(cm)))
    y = jnp.moveaxis(y, 0, 1).reshape(B, S, G, HP, P)
    y = y + d_skip.astype(f32).reshape(G, HP)[..., None] * xs.astype(f32)
    yg = (y.reshape(B, S, SSM_D_INNER) * jax.nn.silu(z.astype(f32))).reshape(B, S, G, SSM_D_INNER // G)
    yg = yg * lax.rsqrt(jnp.mean(yg * yg, axis=-1, keepdims=True) + EPS)
    y = (yg.reshape(B, S, SSM_D_INNER) * norm_w.astype(f32)).astype(h.dtype)
    return y @ w_out


def swiglu(h, w_gate, w_up, w_down):
    return (jax.nn.silu(h @ w_gate) * (h @ w_up)) @ w_down


def setup_inputs(seed: int = 0) -> dict:
    key = jax.random.key(seed)
    keys = iter(jax.random.split(key, 32))
    f32 = jnp.float32

    def nrm(shape, scale):
        return jax.random.normal(next(keys), shape, f32) * scale

    def gain(shape):
        return 1.0 + nrm(shape, 0.02)

    NA, NB = N_NSA_LAYERS, N_SSM_LAYERS
    x = nrm((BATCH, SEQ, D_MODEL), 1.0)
    positions = jnp.broadcast_to(jnp.arange(SEQ, dtype=jnp.int32), (BATCH, SEQ))
    mix_norm_w = gain((DEPTH, D_MODEL))
    ffn_norm_w = gain((DEPTH, D_MODEL))
    ffn_w_gate = nrm((DEPTH, D_MODEL, FFN_HIDDEN), D_MODEL ** -0.5)
    ffn_w_up = nrm((DEPTH, D_MODEL, FFN_HIDDEN), D_MODEL ** -0.5)
    ffn_w_down = nrm((DEPTH, FFN_HIDDEN, D_MODEL), FFN_HIDDEN ** -0.5)
    nsa_w_in = nrm((NA, D_MODEL, NSA_IN_WIDTH), D_MODEL ** -0.5)
    nsa_q_norm = gain((NA, HEAD_DIM))
    nsa_k_norm = gain((NA, 3, HEAD_DIM))
    nsa_cmp_pe = nrm((NA, 2, CMP_BLOCK, HEAD_DIM), 0.1)
    nsa_cmp_w1 = nrm((NA, 2, CMP_BLOCK * HEAD_DIM, CMP_HIDDEN), (CMP_BLOCK * HEAD_DIM) ** -0.5)
    nsa_cmp_b1 = nrm((NA, 2, CMP_HIDDEN), 0.01)
    nsa_cmp_w2 = nrm((NA, 2, CMP_HIDDEN, HEAD_DIM), CMP_HIDDEN ** -0.5)
    nsa_w_out = nrm((NA, NSA_Q_WIDTH, D_MODEL), NSA_Q_WIDTH ** -0.5)
    ssm_w_in = nrm((NB, D_MODEL, SSM_IN_WIDTH), D_MODEL ** -0.5)
    ssm_conv_w = nrm((NB, SSM_CONV, SSM_CONV_DIM), SSM_CONV ** -0.5)
    ssm_conv_b = nrm((NB, SSM_CONV_DIM), 0.01)
    dt0 = jnp.exp(jax.random.uniform(next(keys), (NB, SSM_HEADS), f32,
                                     minval=math.log(1e-3), maxval=math.log(1e-1)))
    ssm_dt_bias = dt0 + jnp.log(-jnp.expm1(-dt0))
    ssm_a_log = jnp.log(jax.random.uniform(next(keys), (NB, SSM_HEADS), f32, minval=1.0, maxval=16.0))
    ssm_d = gain((NB, SSM_HEADS))
    ssm_norm_w = gain((NB, SSM_D_INNER))
    ssm_w_out = nrm((NB, SSM_D_INNER, D_MODEL), SSM_D_INNER ** -0.5)
    return {"x": x, "positions": positions, "mix_norm_w": mix_norm_w, "ffn_norm_w": ffn_norm_w,
            "ffn_w_gate": ffn_w_gate, "ffn_w_up": ffn_w_up, "ffn_w_down": ffn_w_down,
            "nsa_w_in": nsa_w_in, "nsa_q_norm": nsa_q_norm, "nsa_k_norm": nsa_k_norm,
            "nsa_cmp_pe": nsa_cmp_pe, "nsa_cmp_w1": nsa_cmp_w1, "nsa_cmp_b1": nsa_cmp_b1,
            "nsa_cmp_w2": nsa_cmp_w2, "nsa_w_out": nsa_w_out,
            "ssm_w_in": ssm_w_in, "ssm_conv_w": ssm_conv_w, "ssm_conv_b": ssm_conv_b,
            "ssm_dt_bias": ssm_dt_bias, "ssm_a_log": ssm_a_log, "ssm_d": ssm_d,
            "ssm_norm_w": ssm_norm_w, "ssm_w_out": ssm_w_out}


def reference(x, positions, mix_norm_w, ffn_norm_w, ffn_w_gate, ffn_w_up, ffn_w_down,
              nsa_w_in, nsa_q_norm, nsa_k_norm, nsa_cmp_pe, nsa_cmp_w1, nsa_cmp_b1, nsa_cmp_w2, nsa_w_out,
              ssm_w_in, ssm_conv_w, ssm_conv_b, ssm_dt_bias, ssm_a_log, ssm_d, ssm_norm_w, ssm_w_out):
    h = x
    for i in range(DEPTH):
        j = i // 2
        hn = rms_norm(h, mix_norm_w[i])
        if i % 2 == 0:
            mix = nsa_mixer(hn, positions, nsa_w_in[j], nsa_q_norm[j], nsa_k_norm[j], nsa_cmp_pe[j],
                            nsa_cmp_w1[j], nsa_cmp_b1[j], nsa_cmp_w2[j], nsa_w_out[j])
        else:
            mix = ssd_mixer(hn, ssm_w_in[j], ssm_conv_w[j], ssm_conv_b[j], ssm_dt_bias[j],
                            ssm_a_log[j], ssm_d[j], ssm_norm_w[j], ssm_w_out[j])
        h = h + mix.astype(h.dtype)
        h = h + swiglu(rms_norm(h, ffn_norm_w[i]), ffn_w_gate[i], ffn_w_up[i], ffn_w_down[i]).astype(h.dtype)
    return h
```

```python
import functools
import math

import numpy as np
import jax
import jax.numpy as jnp
from jax import lax
from jax.experimental import pallas as pl
from jax.experimental.pallas import tpu as pltpu

F32 = jnp.float32
BF16 = jnp.bfloat16

D_MODEL = 1024
DEPTH = 4
EPS = 1e-6

NSA_HEADS = 16
NSA_G = 4
NSA_HPG = NSA_HEADS // NSA_G
HEAD_DIM = 64
CMP_BLOCK = 32
CMP_STRIDE = 16
CMP_HIDDEN = 256
SEL_BLOCK = 64
SEL_SHIFT = 6
SEL_TOPK = 8
WINDOW = 512
ROPE_THETA = 500000.0
ROPE_DIM = HEAD_DIM // 4
ROPE_HALF = ROPE_DIM // 2
NSA_Q_WIDTH = NSA_HEADS * HEAD_DIM
NSA_KV_WIDTH = NSA_G * HEAD_DIM
NSA_GATE_PAD = 128
NSA_IN_PAD = NSA_Q_WIDTH + 6 * NSA_KV_WIDTH + NSA_GATE_PAD

SSM_D_INNER = 2 * D_MODEL
SSM_P = 64
SSM_HEADS = SSM_D_INNER // SSM_P
SSM_G = 4
SSM_HPG = SSM_HEADS // SSM_G
SSM_N = 128
SSM_CONV = 4
SSM_CHUNK = 128
SSM_CONV_DIM = SSM_D_INNER + 2 * SSM_G * SSM_N
SSM_DT_PAD = 128
SSM_IN_PAD = SSM_D_INNER + SSM_CONV_DIM + SSM_DT_PAD

FFN_HIDDEN = -(-8 * D_MODEL // (3 * 256)) * 256

LANES = 128
VMEM_LIMIT_BYTES = 52 * 1024 * 1024

MASK_BIG = 1e30
M_INIT = -3e38

_NT = (((1,), (1,)), ((), ()))


def _cparams(*sem):
    return pltpu.CompilerParams(dimension_semantics=sem, vmem_limit_bytes=VMEM_LIMIT_BYTES)


def _dot(a, b):
    return jnp.dot(a, b, preferred_element_type=F32)


def _dot_nt(a, b):
    return lax.dot_general(a, b, _NT, preferred_element_type=F32)


def _split3(x):
    a = x.astype(BF16)
    r = x - a.astype(F32)
    b = r.astype(BF16)
    c = (r - b.astype(F32)).astype(BF16)
    return a, b, c


def _rms_rows(x, w):
    return x * lax.rsqrt(jnp.mean(x * x, axis=-1, keepdims=True) + EPS) * w


def _silu(x):
    return x / (1.0 + jnp.exp(-x))


def _rope_table_kernel(pos_ref, c_ref, sa_ref, sb_ref):
    pos = pos_ref[...].astype(F32)
    lane = lax.broadcasted_iota(jnp.int32, (1, LANES), 1)
    d = lane & (HEAD_DIM - 1)
    f = d & (ROPE_HALF - 1)
    inv = jnp.zeros((1, LANES), F32)
    for i in range(ROPE_HALF):
        inv = jnp.where(f == i, float(np.power(np.float32(ROPE_THETA), np.float32(-i / ROPE_HALF))), inv)
    ang = pos * inv
    cos, sin = jnp.cos(ang), jnp.sin(ang)
    c_ref[...] = jnp.where(d < ROPE_DIM, cos, 1.0)
    sa_ref[...] = jnp.where(d < ROPE_HALF, -sin, 0.0)
    sb_ref[...] = jnp.where((d >= ROPE_HALF) & (d < ROPE_DIM), sin, 0.0)


def _rope_tables(pos_col, tm):
    n = pos_col.shape[0]
    out = jax.ShapeDtypeStruct((n, LANES), F32)
    spec = pl.BlockSpec((tm, LANES), lambda i: (i, 0))
    return pl.pallas_call(
        _rope_table_kernel,
        grid=(n // tm,),
        in_specs=[pl.BlockSpec((tm, 1), lambda i: (i, 0))],
        out_specs=[spec, spec, spec],
        out_shape=[out, out, out],
        compiler_params=_cparams("parallel"),
        name="rope_tables",
    )(pos_col)


def _rope_lanes(x, c, sa, sb):
    return x * c + pltpu.roll(x, LANES - ROPE_HALF, 1) * sa + pltpu.roll(x, ROPE_HALF, 1) * sb


def _head_norm_rope(y, w, bd, c, sa, sb, scale):
    outs = []
    for j in range(y.shape[1] // LANES):
        yc = y[:, LANES * j:LANES * (j + 1)]
        sq = yc * yc
        hi = sq.astype(BF16)
        lo = (sq - hi.astype(F32)).astype(BF16)
        ms = _dot(hi, bd) + _dot(lo, bd)
        yn = yc * lax.rsqrt(ms + EPS) * w[:, LANES * j:LANES * (j + 1)]
        outs.append(_rope_lanes(yn, c, sa, sb) * scale)
    return jnp.concatenate(outs, axis=1)


def _nsa_inproj_kernel(x_ref, nw_ref, w_ref, c_ref, sa_ref, sb_ref, qn_ref, ksn_ref, kwn_ref, bd_ref,
                       q_ref, kc_ref, vc_ref, ks_ref, vs_ref, kw_ref, vw_ref, g_ref):
    xn = _rms_rows(x_ref[...], nw_ref[...]).astype(BF16)
    c, sa, sb, bd = c_ref[...], sa_ref[...], sb_ref[...], bd_ref[...]
    kvw = NSA_KV_WIDTH

    def proj(lo, width):
        return _dot(xn, w_ref[:, lo:lo + width])

    q_ref[...] = _head_norm_rope(proj(0, NSA_Q_WIDTH), qn_ref[...], bd, c, sa, sb,
                                 HEAD_DIM ** -0.5).astype(BF16)
    base = NSA_Q_WIDTH
    kc_ref[...] = proj(base, kvw)
    vc_ref[...] = proj(base + kvw, kvw)
    ks_ref[...] = _head_norm_rope(proj(base + 2 * kvw, kvw), ksn_ref[...], bd, c, sa, sb, 1.0).astype(BF16)
    vs_ref[...] = proj(base + 3 * kvw, kvw).astype(BF16)
    kw_ref[...] = _head_norm_rope(proj(base + 4 * kvw, kvw), kwn_ref[...], bd, c, sa, sb, 1.0).astype(BF16)
    vw_ref[...] = proj(base + 5 * kvw, kvw).astype(BF16)
    gl = proj(base + 6 * kvw, NSA_GATE_PAD)
    g_ref[...] = 1.0 / (1.0 + jnp.exp(-gl))


def _nsa_inproj(h, nw, w, tabs, qn, ksn, kwn, bd, tm=512):
    T = h.shape[0]
    row = lambda width: pl.BlockSpec((tm, width), lambda i: (i, 0))
    full = lambda a: pl.BlockSpec(a.shape, lambda i: (0,) * a.ndim)
    c, sa, sb = tabs
    kvw = NSA_KV_WIDTH
    outs = [(NSA_Q_WIDTH, BF16), (kvw, F32), (kvw, F32), (kvw, BF16), (kvw, BF16), (kvw, BF16), (kvw, BF16),
            (NSA_GATE_PAD, F32)]
    return pl.pallas_call(
        _nsa_inproj_kernel,
        grid=(T // tm,),
        in_specs=[row(D_MODEL), full(nw), full(w), row(LANES), row(LANES), row(LANES),
                  full(qn), full(ksn), full(kwn), full(bd)],
        out_specs=[row(wd) for wd, _ in outs],
        out_shape=[jax.ShapeDtypeStruct((T, wd), dt) for wd, dt in outs],
        compiler_params=_cparams("parallel"),
        name="nsa_inproj",
    )(h, nw, w, c, sa, sb, qn, ksn, kwn, bd)


def _nsa_compress_kernel(xk_ref, xv_ref, pe_ref, w1_ref, b1_ref, w2k_ref, w2v_ref, knw_ref,
                         c_ref, sa_ref, sb_ref, kc_ref, vc_ref):
    ncp = xk_ref.shape[2]

    def hidden(x, which):
        lo = _dot((x + pe_ref[which, 0:1, :]).astype(BF16), w1_ref[which, 0])
        hi = _dot((x + pe_ref[which, 1:2, :]).astype(BF16), w1_ref[which, 1])
        return _silu(lo + pltpu.roll(hi, ncp - 1, 0) + b1_ref[which]).astype(BF16)

    for g in range(NSA_G):
        kc = _dot(hidden(xk_ref[0, g], 0), w2k_ref[...])
        ms = jnp.sum(kc * kc, axis=-1, keepdims=True) * (1.0 / HEAD_DIM)
        kn = kc * lax.rsqrt(ms + EPS) * knw_ref[...]
        kn = _rope_lanes(kn, c_ref[...], sa_ref[...], sb_ref[...])
        kc_ref[0, g] = kn[:, :HEAD_DIM].astype(BF16)
        vc_ref[0, g] = _dot(hidden(xv_ref[0, g], 1), w2v_ref[...]).astype(BF16)


def _nsa_compress(xk, xv, pe, w1, b1, w2k, w2v, knw, tabs):
    B, G, ncp, width = xk.shape
    c, sa, sb = tabs
    xspec = pl.BlockSpec((1, G, ncp, width), lambda b: (b, 0, 0, 0))
    tspec = pl.BlockSpec((ncp, LANES), lambda b: (b, 0))
    full = lambda a: pl.BlockSpec(a.shape, lambda b: (0,) * a.ndim)
    ospec = pl.BlockSpec((1, G, ncp, HEAD_DIM), lambda b: (b, 0, 0, 0))
    oshape = jax.ShapeDtypeStruct((B, G, ncp, HEAD_DIM), BF16)
    return pl.pallas_call(
        _nsa_compress_kernel,
        grid=(B,),
        in_specs=[xspec, xspec, full(pe), full(w1), full(b1), full(w2k), full(w2v), full(knw),
                  tspec, tspec, tspec],
        out_specs=[ospec, ospec],
        out_shape=[oshape, oshape],
        compiler_params=_cparams("parallel"),
        name="nsa_compress",
    )(xk, xv, pe, w1, b1, w2k, w2v, knw, c, sa, sb)


def _flash_step(qa, k_tile, v_tile, state, mask=None):
    m, l, acc = state
    s = _dot_nt(qa, k_tile)
    if mask is not None:
        s = jnp.where(mask, s, -MASK_BIG)
    m_new = jnp.maximum(m, jnp.max(s, axis=1, keepdims=True))
    alpha = jnp.exp(m - m_new)
    p = jnp.exp(s - m_new)
    l = alpha * l + jnp.sum(p, axis=1, keepdims=True)
    acc = alpha * acc + _dot(p.astype(BF16), v_tile)
    return m_new, l, acc


def _nsa_attn_kernel(q_ref, kc_ref, vc_ref, ks_ref, vs_ref, kw_ref, vw_ref, g_ref, ovl_ref, o_ref,
                     ksa_ref, kwa_ref, vsg_ref, vwg_ref, qa_ref, *, tq):
    S = ks_ref.shape[0]
    ncp = kc_ref.shape[2]
    nblk = S // SEL_BLOCK
    G, HP, dh = NSA_G, NSA_HPG, HEAD_DIM
    rows = HP * tq
    qi = pl.program_id(1)
    q0 = qi * tq

    @pl.when(qi == 0)
    def _():
        rblk = lax.broadcasted_iota(jnp.int32, (S, dh), 0) >> SEL_SHIFT
        lane = lax.broadcasted_iota(jnp.int32, (S, dh), 1)
        onehot = jnp.where(rblk == lane, 1.0, 0.0).astype(BF16)
        zeros = jnp.zeros((S, dh), BF16)
        for g in range(G):
            ksa_ref[g, :, 0:dh] = ks_ref[:, dh * g:dh * (g + 1)]
            ksa_ref[g, :, dh:2 * dh] = onehot
            kwa_ref[g, :, 0:dh] = kw_ref[:, dh * g:dh * (g + 1)]
            kwa_ref[g, :, dh:2 * dh] = zeros
            vsg_ref[g] = vs_ref[:, dh * g:dh * (g + 1)]
            vwg_ref[g] = vw_ref[:, dh * g:dh * (g + 1)]

    t_col = q0 + lax.broadcasted_iota(jnp.int32, (tq, 1), 0)
    t_row = q0 + lax.broadcasted_iota(jnp.int32, (1, tq), 1)
    t_stack = q0 + (lax.broadcasted_iota(jnp.int32, (rows, 1), 0) & (tq - 1))
    k_loc = lax.broadcasted_iota(jnp.int32, (1, tq), 1)

    cmp_end = lax.broadcasted_iota(jnp.int32, (1, ncp), 1) * CMP_STRIDE + (CMP_BLOCK - 1)
    cmask = cmp_end <= t_col
    jb = lax.broadcasted_iota(jnp.int32, (nblk, tq), 0)
    tblk = t_row >> SEL_SHIFT
    forced = (jb == 0) | (jb == tblk) | (jb == tblk - 1)

    for g in range(G):
        kc, vc = kc_ref[0, g], vc_ref[0, g]
        psum = jnp.zeros((tq, ncp), F32)
        oc = []
        for h in range(HP):
            hd = g * HP + h
            qh = q_ref[:, dh * hd:dh * (hd + 1)]
            s = jnp.where(cmask, _dot_nt(qh, kc), -MASK_BIG)
            m = jnp.max(s, axis=1, keepdims=True)
            p = jnp.where(cmask, jnp.exp(s - m), 0.0)
            l = jnp.sum(p, axis=1, keepdims=True)
            p = p * jnp.where(l > 0.0, 1.0 / l, 0.0)
            psum = psum + p
            oc.append(_dot(p.astype(BF16), vc))
            qa_ref[h * tq:(h + 1) * tq, 0:dh] = qh
        ovl = ovl_ref[...]
        p1, p2, p3 = _split3(psum)
        imp_t = (_dot_nt(ovl, p1) + _dot_nt(ovl, p2) + _dot_nt(ovl, p3))[0:nblk]
        v = jnp.where(forced, MASK_BIG, jnp.where(jb > tblk, -MASK_BIG, imp_t))
        cnt = jnp.zeros((nblk, tq), jnp.int32)
        for j in range(nblk):
            rj = v[j:j + 1, :]
            beats = (rj > v) | ((rj == v) & (jb > j))
            cnt = cnt + jnp.where(beats, 1, 0)
        selneg_t = jnp.where(cnt < SEL_TOPK, 0.0, -MASK_BIG)
        pad_t = jnp.zeros((LANES - nblk, tq), F32)
        selneg = jnp.concatenate([selneg_t, pad_t], axis=0).T
        saug = selneg[:, 0:dh].astype(BF16)
        for h in range(HP):
            qa_ref[h * tq:(h + 1) * tq, dh:2 * dh] = saug
        qa = qa_ref[...]

        init = (jnp.full((rows, 1), M_INIT, F32), jnp.zeros((rows, 1), F32), jnp.zeros((rows, dh), F32))

        causal = (q0 + k_loc) <= t_stack
        st = _flash_step(qa, ksa_ref[g, pl.ds(q0, tq), :], vsg_ref[g, pl.ds(q0, tq), :], init, causal)

        def sel_body(kt, st):
            k0 = pl.multiple_of(kt * tq, tq)
            return _flash_step(qa, ksa_ref[g, pl.ds(k0, tq), :], vsg_ref[g, pl.ds(k0, tq), :], st)

        m_s, l_s, acc_s = lax.fori_loop(0, qi, sel_body, st)
        o_s = acc_s * (1.0 / l_s)

        st = _flash_step(qa, kwa_ref[g, pl.ds(q0, tq), :], vwg_ref[g, pl.ds(q0, tq), :], init, causal)

        def win_body(dk, st):
            k0 = pl.multiple_of((qi - dk) * tq, tq)
            inside = (t_stack - (k0 + k_loc)) < WINDOW
            return _flash_step(qa, kwa_ref[g, pl.ds(k0, tq), :], vwg_ref[g, pl.ds(k0, tq), :], st, inside)

        n_back = jnp.minimum(qi, (WINDOW + tq - 1) // tq)
        m_w, l_w, acc_w = lax.fori_loop(1, n_back + 1, win_body, st)
        o_w = acc_w * (1.0 / l_w)

        for h in range(HP):
            hd = g * HP + h
            gc = g_ref[:, 3 * hd:3 * hd + 1]
            gs = g_ref[:, 3 * hd + 1:3 * hd + 2]
            gw = g_ref[:, 3 * hd + 2:3 * hd + 3]
            r = slice(h * tq, (h + 1) * tq)
            o_ref[:, dh * hd:dh * (hd + 1)] = (gc * oc[h] + gs * o_s[r] + gw * o_w[r]).astype(BF16)


def _nsa_attn(q, kc, vc, ks, vs, kw, vw, gates, ovl, B, S, tq=256):
    T = B * S
    nq = S // tq
    G, dh = NSA_G, HEAD_DIM
    ncp = kc.shape[2]
    qspec = pl.BlockSpec((tq, NSA_Q_WIDTH), lambda b, i: (b * nq + i, 0))
    cspec = pl.BlockSpec((1, G, ncp, dh), lambda b, i: (b, 0, 0, 0))
    kvspec = pl.BlockSpec((S, NSA_KV_WIDTH), lambda b, i: (b, 0))
    gspec = pl.BlockSpec((tq, NSA_GATE_PAD), lambda b, i: (b * nq + i, 0))
    ovspec = pl.BlockSpec(ovl.shape, lambda b, i: (0, 0))
    return pl.pallas_call(
        functools.partial(_nsa_attn_kernel, tq=tq),
        grid=(B, nq),
        in_specs=[qspec, cspec, cspec, kvspec, kvspec, kvspec, kvspec, gspec, ovspec],
        out_specs=qspec,
        out_shape=jax.ShapeDtypeStruct((T, NSA_Q_WIDTH), BF16),
        scratch_shapes=[pltpu.VMEM((G, S, 2 * dh), BF16), pltpu.VMEM((G, S, 2 * dh), BF16),
                        pltpu.VMEM((G, S, dh), BF16), pltpu.VMEM((G, S, dh), BF16),
                        pltpu.VMEM((NSA_HPG * tq, 2 * dh), BF16)],
        compiler_params=_cparams("arbitrary", "arbitrary"),
        name="nsa_attn",
    )(q, kc, vc, ks, vs, kw, vw, gates, ovl)


def _proj_res_kernel(a_ref, w_ref, r_ref, o_ref):
    o_ref[...] = r_ref[...] + _dot(a_ref[...], w_ref[...])


def _proj_res(a, w, res, tm=512):
    T, K = a.shape
    N = w.shape[1]
    return pl.pallas_call(
        _proj_res_kernel,
        grid=(T // tm,),
        in_specs=[pl.BlockSpec((tm, K), lambda i: (i, 0)), pl.BlockSpec((K, N), lambda i: (0, 0)),
                  pl.BlockSpec((tm, N), lambda i: (i, 0))],
        out_specs=pl.BlockSpec((tm, N), lambda i: (i, 0)),
        out_shape=jax.ShapeDtypeStruct((T, N), F32),
        compiler_params=_cparams("parallel"),
        name="proj_residual",
    )(a, w, res)


def _ffn_kernel(x_ref, nw_ref, wg_ref, wu_ref, wd_ref, o_ref, xn_ref, acc_ref):
    k = pl.program_id(1)

    @pl.when(k == 0)
    def _():
        xn_ref[...] = _rms_rows(x_ref[...], nw_ref[...]).astype(BF16)
        acc_ref[...] = x_ref[...]

    xn = xn_ref[...]
    a = (_silu(_dot(xn, wg_ref[...])) * _dot(xn, wu_ref[...])).astype(BF16)
    acc_ref[...] += _dot(a, wd_ref[...])

    @pl.when(k == pl.num_programs(1) - 1)
    def _():
        o_ref[...] = acc_ref[...]


def _ffn(h, nw, wg, wu, wd, tm=512, th=1408):
    T = h.shape[0]
    H = wg.shape[1]
    return pl.pallas_call(
        _ffn_kernel,
        grid=(T // tm, H // th),
        in_specs=[pl.BlockSpec((tm, D_MODEL), lambda i, k: (i, 0)),
                  pl.BlockSpec((1, D_MODEL), lambda i, k: (0, 0)),
                  pl.BlockSpec((D_MODEL, th), lambda i, k: (0, k)),
                  pl.BlockSpec((D_MODEL, th), lambda i, k: (0, k)),
                  pl.BlockSpec((th, D_MODEL), lambda i, k: (k, 0))],
        out_specs=pl.BlockSpec((tm, D_MODEL), lambda i, k: (i, 0)),
        out_shape=jax.ShapeDtypeStruct((T, D_MODEL), F32),
        scratch_shapes=[pltpu.VMEM((tm, D_MODEL), BF16), pltpu.VMEM((tm, D_MODEL), F32)],
        compiler_params=_cparams("parallel", "arbitrary"),
        name="ffn",
    )(h, nw, wg, wu, wd)


def _ssm_inproj_kernel(x_ref, nw_ref, w_ref, z_ref, xbc_ref, dt_ref):
    xn = _rms_rows(x_ref[...], nw_ref[...]).astype(BF16)
    chunk = 1024
    for lo in range(0, SSM_D_INNER, chunk):
        z_ref[:, lo:lo + chunk] = _dot(xn, w_ref[:, lo:lo + chunk])
    for lo in range(0, SSM_CONV_DIM, chunk):
        xbc_ref[:, lo:lo + chunk] = _dot(xn, w_ref[:, SSM_D_INNER + lo:SSM_D_INNER + lo + chunk])
    base = SSM_D_INNER + SSM_CONV_DIM
    dt_ref[...] = _dot(xn, w_ref[:, base:base + SSM_DT_PAD])


def _ssm_inproj(h, nw, w, tm=256):
    T = h.shape[0]
    row = lambda width: pl.BlockSpec((tm, width), lambda i: (i, 0))
    full = lambda a: pl.BlockSpec(a.shape, lambda i: (0,) * a.ndim)
    widths = (SSM_D_INNER, SSM_CONV_DIM, SSM_DT_PAD)
    return pl.pallas_call(
        _ssm_inproj_kernel,
        grid=(T // tm,),
        in_specs=[row(D_MODEL), full(nw), full(w)],
        out_specs=[row(wd) for wd in widths],
        out_shape=[jax.ShapeDtypeStruct((T, wd), F32) for wd in widths],
        compiler_params=_cparams("parallel"),
        name="ssm_inproj",
    )(h, nw, w)


def _ssd_kernel(xbc_ref, z_ref, dt_ref, cw_ref, cb_ref, dtb_ref, alog_ref, dsk_ref, nw_ref, tri_ref,
                y_ref, ext_ref, state_ref):
    Q, P, N, G, HPG = SSM_CHUNK, SSM_P, SSM_N, SSM_G, SSM_HPG
    c = pl.program_id(1)
    halo = 8

    @pl.when(c == 0)
    def _():
        ext_ref[0:halo, :] = jnp.zeros((halo, SSM_CONV_DIM), F32)
        state_ref[...] = jnp.zeros_like(state_ref)

    x = xbc_ref[...]
    ext_ref[halo:halo + Q, :] = x
    acc = cb_ref[...] + cw_ref[SSM_CONV - 1:SSM_CONV, :] * x
    for k in range(SSM_CONV - 1):
        sh = SSM_CONV - 1 - k
        acc = acc + cw_ref[k:k + 1, :] * ext_ref[halo - sh:halo - sh + Q, :]
    ext_ref[0:halo, :] = x[Q - halo:Q, :]
    act = _silu(acc)
    xs = act[:, 0:SSM_D_INNER]
    xs_b = xs.astype(BF16)
    bm = act[:, SSM_D_INNER:SSM_D_INNER + G * N]
    cm_b = act[:, SSM_D_INNER + G * N:SSM_D_INNER + 2 * G * N].astype(BF16)

    dtl = dt_ref[...] + dtb_ref[...]
    dt = jnp.maximum(dtl, 0.0) + jnp.log(1.0 + jnp.exp(-jnp.abs(dtl)))
    a = dt * (-jnp.exp(alog_ref[...]))
    a1, a2, a3 = _split3(a)
    tri = tri_ref[...]
    cum = _dot(tri, a1) + _dot(tri, a2) + _dot(tri, a3)
    cum_t = cum.T
    dt_t = dt.T
    row_i = lax.broadcasted_iota(jnp.int32, (Q, Q), 0)
    col_i = lax.broadcasted_iota(jnp.int32, (Q, Q), 1)
    tril = row_i >= col_i

    y_parts = []
    for g in range(G):
        cg = cm_b[:, N * g:N * (g + 1)]
        bg = bm[:, N * g:N * (g + 1)]
        cb = _dot_nt(cg, bg.astype(BF16))
        bg_t = bg.T
        st_g = state_ref[g]
        y_off = _dot(cg, st_g.astype(BF16))
        new_state = []
        for hh in range(HPG):
            h = g * HPG + hh
            cum_col = cum[:, h:h + 1]
            cum_row = cum_t[h:h + 1, :]
            dt_row = dt_t[h:h + 1, :]
            cum_last = cum_row[:, Q - 1:Q]
            L = jnp.exp(jnp.where(tril, cum_col - cum_row, -jnp.inf))
            mm = (cb * L * dt_row).astype(BF16)
            xh_b = xs_b[:, P * h:P * (h + 1)]
            yh = _dot(mm, xh_b) + jnp.exp(cum_col) * y_off[:, P * hh:P * (hh + 1)]
            yh = yh + dsk_ref[:, P * h:P * (h + 1)] * xs[:, P * h:P * (h + 1)]
            y_parts.append(yh)
            wgt = (bg_t * (dt_row * jnp.exp(cum_last - cum_row))).astype(BF16)
            new_state.append(st_g[:, P * hh:P * (hh + 1)] * jnp.exp(cum_last) + _dot(wgt, xh_b))
        state_ref[g] = jnp.concatenate(new_state, axis=1)

    gw = SSM_D_INNER // G
    for g in range(G):
        yg = jnp.concatenate(y_parts[g * HPG:(g + 1) * HPG], axis=1)
        yg = yg * _silu(z_ref[:, gw * g:gw * (g + 1)])
        yg = yg * lax.rsqrt(jnp.mean(yg * yg, axis=-1, keepdims=True) + EPS)
        y_ref[:, gw * g:gw * (g + 1)] = (yg * nw_ref[:, gw * g:gw * (g + 1)]).astype(BF16)


def _ssd(xbc, z, dt, cw, cb, dtb, alog, dsk, nw, tri, B, S):
    Q = SSM_CHUNK
    nch = S // Q
    row = lambda width: pl.BlockSpec((Q, width), lambda b, c: (b * nch + c, 0))
    full = lambda a: pl.BlockSpec(a.shape, lambda b, c: (0,) * a.ndim)
    return pl.pallas_call(
        _ssd_kernel,
        grid=(B, nch),
        in_specs=[row(SSM_CONV_DIM), row(SSM_D_INNER), row(SSM_DT_PAD), full(cw), full(cb), full(dtb),
                  full(alog), full(dsk), full(nw), full(tri)],
        out_specs=row(SSM_D_INNER),
        out_shape=jax.ShapeDtypeStruct((B * S, SSM_D_INNER), BF16),
        scratch_shapes=[pltpu.VMEM((8 + Q, SSM_CONV_DIM), F32),
                        pltpu.VMEM((SSM_G, SSM_N, SSM_HPG * SSM_P), F32)],
        compiler_params=_cparams("arbitrary", "arbitrary"),
        name="ssd_scan",
    )(xbc, z, dt, cw, cb, dtb, alog, dsk, nw, tri)


def _block_diag_mean():
    i = np.arange(LANES)
    return jnp.asarray((i[:, None] // HEAD_DIM == i[None, :] // HEAD_DIM) / HEAD_DIM, BF16)


def _overlap_t(S):
    nc = (S - CMP_BLOCK) // CMP_STRIDE + 1
    ncp = S // CMP_STRIDE
    nblk = S // SEL_BLOCK
    starts = np.arange(ncp) * CMP_STRIDE
    js = np.arange(nblk)[:, None] * SEL_BLOCK
    ov = (starts[None, :] < js + SEL_BLOCK) & (starts[None, :] + CMP_BLOCK > js) & (np.arange(ncp)[None, :] < nc)
    out = np.zeros((LANES, ncp), np.float32)
    out[:nblk] = ov
    return jnp.asarray(out, BF16)


def _pad_cols(w, width):
    return jnp.pad(w, ((0, 0), (0, width - w.shape[1])))


def _nsa_layer(h, tabs, tabs_c, B, S, nw, w_in, q_norm, k_norm, cmp_pe, cmp_w1, cmp_b1, cmp_w2, w_out):
    T = B * S
    G, dh = NSA_G, HEAD_DIM
    ncp = S // CMP_STRIDE
    w = _pad_cols(w_in, NSA_IN_PAD).astype(BF16)
    qn = jnp.tile(q_norm, NSA_HEADS)[None, :]
    ksn = jnp.tile(k_norm[1], G)[None, :]
    kwn = jnp.tile(k_norm[2], G)[None, :]
    q, kc_raw, vc_raw, ks, vs, kw, vw, gates = _nsa_inproj(h, nw[None, :], w, tabs, qn, ksn, kwn,
                                                           _block_diag_mean())

    def to_chunks(t):
        t = t.reshape(B, ncp, CMP_STRIDE, G, dh).transpose(0, 3, 1, 2, 4)
        return t.reshape(B, G, ncp, CMP_STRIDE * dh)

    half = CMP_STRIDE * dh
    pe = cmp_pe.reshape(2, 2, half)
    w1 = cmp_w1.reshape(2, 2, half, CMP_HIDDEN).astype(BF16)
    b1 = cmp_b1[:, None, :]
    w2k = _pad_cols(cmp_w2[0], LANES).astype(BF16)
    w2v = cmp_w2[1].astype(BF16)
    knw = _pad_cols(k_norm[0][None, :], LANES)
    kc, vc = _nsa_compress(to_chunks(kc_raw), to_chunks(vc_raw), pe, w1, b1, w2k, w2v, knw, tabs_c)

    o = _nsa_attn(q, kc, vc, ks, vs, kw, vw, gates, _overlap_t(S), B, S)
    return _proj_res(o, w_out.astype(BF16), h)


def _ssd_layer(h, B, S, nw, w_in, conv_w, conv_b, dt_bias, a_log, d_skip, norm_w, w_out):
    w = _pad_cols(w_in, SSM_IN_PAD).astype(BF16)
    z, xbc, dt = _ssm_inproj(h, nw[None, :], w)
    pad1 = lambda v: _pad_cols(v[None, :], SSM_DT_PAD)
    dsk = jnp.repeat(d_skip, SSM_P)[None, :]
    tri = jnp.asarray(np.tril(np.ones((SSM_CHUNK, SSM_CHUNK), np.float32)), BF16)
    y = _ssd(xbc, z, dt, conv_w, conv_b[None, :], pad1(dt_bias), pad1(a_log), dsk, norm_w[None, :], tri, B, S)
    return _proj_res(y, w_out.astype(BF16), h)


def kernel(x, positions, mix_norm_w, ffn_norm_w, ffn_w_gate, ffn_w_up, ffn_w_down, nsa_w_in, nsa_q_norm, nsa_k_norm, nsa_cmp_pe, nsa_cmp_w1, nsa_cmp_b1, nsa_cmp_w2, nsa_w_out, ssm_w_in, ssm_conv_w, ssm_conv_b, ssm_dt_bias, ssm_a_log, ssm_d, ssm_norm_w, ssm_w_out):
    B, S, D = x.shape
    T = B * S
    h = x.reshape(T, D)
    ncp = S // CMP_STRIDE
    tabs = _rope_tables(positions.reshape(T, 1), 1024)
    pos_c = jnp.pad(positions[:, CMP_BLOCK - 1::CMP_STRIDE], ((0, 0), (0, 1)))[:, :ncp]
    tabs_c = _rope_tables(pos_c.reshape(B * ncp, 1), ncp)
    for i in range(DEPTH):
        j = i // 2
        if i % 2 == 0:
            h = _nsa_layer(h, tabs, tabs_c, B, S, mix_norm_w[i], nsa_w_in[j], nsa_q_norm[j], nsa_k_norm[j],
                           nsa_cmp_pe[j], nsa_cmp_w1[j], nsa_cmp_b1[j], nsa_cmp_w2[j], nsa_w_out[j])
        else:
            h = _ssd_layer(h, B, S, mix_norm_w[i], ssm_w_in[j], ssm_conv_w[j], ssm_conv_b[j], ssm_dt_bias[j],
                           ssm_a_log[j], ssm_d[j], ssm_norm_w[j], ssm_w_out[j])
        h = _ffn(h, ffn_norm_w[i][None, :], ffn_w_gate[i].astype(BF16), ffn_w_up[i].astype(BF16),
                 ffn_w_down[i].astype(BF16))
    return h.reshape(B, S, D)
```

```python
import math

import numpy as np
import jax
import jax.numpy as jnp
from jax import lax
from jax.experimental import pallas as pl
from jax.experimental.pallas import tpu as pltpu

F32 = jnp.float32
BF16 = jnp.bfloat16

D_MODEL = 1024
DEPTH = 4
EPS = 1e-6

NSA_HEADS = 16
NSA_G = 4
NSA_HPG = NSA_HEADS // NSA_G
HEAD_DIM = 64
CMP_BLOCK = 32
CMP_STRIDE = 16
CMP_HIDDEN = 256
SEL_BLOCK = 64
SEL_SHIFT = 6
SEL_TOPK = 8
WINDOW = 512
ROPE_THETA = 500000.0
ROPE_DIM = HEAD_DIM // 4
ROPE_HALF = ROPE_DIM // 2
NSA_Q_WIDTH = NSA_HEADS * HEAD_DIM
NSA_KV_WIDTH = NSA_G * HEAD_DIM
NSA_GATE_PAD = 128
NSA_TQ = 256
QSCALE = HEAD_DIM ** -0.5 * math.log2(math.e)

SSM_D_INNER = 2 * D_MODEL
SSM_P = 64
SSM_HEADS = SSM_D_INNER // SSM_P
SSM_G = 4
SSM_HPG = SSM_HEADS // SSM_G
SSM_N = 128
SSM_CONV = 4
SSM_CHUNK = 128
SSM_CONV_DIM = SSM_D_INNER + 2 * SSM_G * SSM_N
SSM_DT_PAD = 128
SSM_IN_PAD = SSM_D_INNER + SSM_CONV_DIM + SSM_DT_PAD

FFN_HIDDEN = -(-8 * D_MODEL // (3 * 256)) * 256

LANES = 128
VMEM_LIMIT_BYTES = 52 * 1024 * 1024

MASK_BIG = 1e30
M_INIT = -3e38

_NT = (((1,), (1,)), ((), ()))


def _cparams(*sem, flags=None):
    return pltpu.CompilerParams(dimension_semantics=sem, vmem_limit_bytes=VMEM_LIMIT_BYTES, flags=flags)


def _dot(a, b):
    return jnp.dot(a, b, preferred_element_type=F32)


def _dot_nt(a, b):
    return lax.dot_general(a, b, _NT, preferred_element_type=F32)


def _split3(x):
    a = x.astype(BF16)
    r = x - a.astype(F32)
    b = r.astype(BF16)
    c = (r - b.astype(F32)).astype(BF16)
    return a, b, c


def _rms_rows(x, w):
    return x * lax.rsqrt(jnp.mean(x * x, axis=-1, keepdims=True) + EPS) * w


def _silu(x):
    return x / (1.0 + jnp.exp(-x))


def _rope_table_kernel(pos_ref, c_ref, sa_ref, sb_ref):
    pos = pos_ref[...].astype(F32)
    lane = lax.broadcasted_iota(jnp.int32, (1, LANES), 1)
    d = lane & (HEAD_DIM - 1)
    f = d & (ROPE_HALF - 1)
    inv = jnp.zeros((1, LANES), F32)
    for i in range(ROPE_HALF):
        inv = jnp.where(f == i, float(np.power(np.float32(ROPE_THETA), np.float32(-i / ROPE_HALF))), inv)
    ang = pos * inv
    cos, sin = jnp.cos(ang), jnp.sin(ang)
    c_ref[...] = jnp.where(d < ROPE_DIM, cos, 1.0)
    sa_ref[...] = jnp.where(d < ROPE_HALF, -sin, 0.0)
    sb_ref[...] = jnp.where((d >= ROPE_HALF) & (d < ROPE_DIM), sin, 0.0)


def _rope_tables(pos_col, tm):
    n = pos_col.shape[0]
    out = jax.ShapeDtypeStruct((n, LANES), F32)
    spec = pl.BlockSpec((tm, LANES), lambda i: (i, 0))
    return pl.pallas_call(
        _rope_table_kernel,
        grid=(n // tm,),
        in_specs=[pl.BlockSpec((tm, 1), lambda i: (i, 0))],
        out_specs=[spec, spec, spec],
        out_shape=[out, out, out],
        compiler_params=_cparams("parallel"),
        name="rope_tables",
    )(pos_col)


def _rope_table_t_kernel(pos_ref, c_ref, s_ref):
    pos = pos_ref[...].astype(F32)
    f = lax.broadcasted_iota(jnp.int32, (ROPE_HALF, 1), 0)
    inv = jnp.zeros((ROPE_HALF, 1), F32)
    for i in range(ROPE_HALF):
        inv = jnp.where(f == i, float(np.power(np.float32(ROPE_THETA), np.float32(-i / ROPE_HALF))), inv)
    ang = inv * pos
    c_ref[...] = jnp.cos(ang)
    s_ref[...] = jnp.sin(ang)


def _rope_tables_t(pos_row, tm):
    n = pos_row.shape[1]
    out = jax.ShapeDtypeStruct((ROPE_HALF, n), F32)
    spec = pl.BlockSpec((ROPE_HALF, tm), lambda i: (0, i))
    return pl.pallas_call(
        _rope_table_t_kernel,
        grid=(n // tm,),
        in_specs=[pl.BlockSpec((1, tm), lambda i: (0, i))],
        out_specs=[spec, spec],
        out_shape=[out, out],
        compiler_params=_cparams("parallel"),
        name="rope_tables_t",
    )(pos_row)


def _rope_lanes(x, c, sa, sb):
    return x * c + pltpu.roll(x, LANES - ROPE_HALF, 1) * sa + pltpu.roll(x, ROPE_HALF, 1) * sb


def _head_norm_rope(y, w, bd, c, sa, sb, scale):
    outs = []
    for j in range(y.shape[1] // LANES):
        yc = y[:, LANES * j:LANES * (j + 1)]
        sq = yc * yc
        hi = sq.astype(BF16)
        lo = (sq - hi.astype(F32)).astype(BF16)
        ms = _dot(hi, bd) + _dot(lo, bd)
        yn = yc * lax.rsqrt(ms + EPS) * w[:, LANES * j:LANES * (j + 1)]
        outs.append(_rope_lanes(yn, c, sa, sb) * scale)
    return jnp.concatenate(outs, axis=1)


def _nsa_inproj_kernel(x_ref, nw_ref, wqt_ref, wk_ref, wvt_ref, wgt_ref, c_ref, sa_ref, sb_ref, ct_ref, st_ref,
                       qn_ref, ksn_ref, kwn_ref, bd_ref,
                       qt_ref, kc_ref, vc_ref, ks_ref, kw_ref, vst_ref, vwt_ref, gt_ref):
    xn = _rms_rows(x_ref[...], nw_ref[...]).astype(BF16)
    c, sa, sb, bd = c_ref[...], sa_ref[...], sb_ref[...], bd_ref[...]
    kvw, dh, tq = NSA_KV_WIDTH, HEAD_DIM, NSA_TQ

    def proj(lo):
        return _dot(xn, wk_ref[:, lo:lo + kvw])

    kc_ref[...] = proj(0)
    vc_ref[...] = proj(kvw)
    ks_ref[...] = _head_norm_rope(proj(2 * kvw), ksn_ref[...], bd, c, sa, sb, 1.0).astype(BF16)
    kw_ref[...] = _head_norm_rope(proj(3 * kvw), kwn_ref[...], bd, c, sa, sb, 1.0).astype(BF16)

    qn = qn_ref[...]
    for ch in range(x_ref.shape[0] // tq):
        xc = xn[ch * tq:(ch + 1) * tq]
        cos, sin = ct_ref[:, ch * tq:(ch + 1) * tq], st_ref[:, ch * tq:(ch + 1) * tq]
        yt = _dot_nt(wqt_ref[...], xc)
        for hd in range(NSA_HEADS):
            yh = yt[dh * hd:dh * (hd + 1)]
            yn = yh * lax.rsqrt(jnp.mean(yh * yh, axis=0, keepdims=True) + EPS) * qn
            x1, x2 = yn[0:ROPE_HALF], yn[ROPE_HALF:ROPE_DIM]
            rot = jnp.concatenate([x1 * cos - x2 * sin, x2 * cos + x1 * sin, yn[ROPE_DIM:]], axis=0)
            qt_ref[ch, dh * hd:dh * (hd + 1), :] = (rot * QSCALE).astype(BF16)
        vt = _dot_nt(wvt_ref[...], xc)
        vst_ref[ch] = vt[0:kvw].astype(BF16)
        vwt_ref[ch] = vt[kvw:2 * kvw].astype(BF16)
        gt_ref[ch] = 1.0 / (1.0 + jnp.exp(-_dot_nt(wgt_ref[...], xc)))


def _nsa_inproj(h, nw, wqt, wk, wvt, wgt, tabs, tabs_t, qn, ksn, kwn, bd, tm=512):
    T = h.shape[0]
    tq, kvw = NSA_TQ, NSA_KV_WIDTH
    row = lambda width: pl.BlockSpec((tm, width), lambda i: (i, 0))
    full = lambda a: pl.BlockSpec(a.shape, lambda i: (0,) * a.ndim)
    colt = pl.BlockSpec((ROPE_HALF, tm), lambda i: (0, i))
    tile = lambda ch: pl.BlockSpec((tm // tq, ch, tq), lambda i: (i, 0, 0))
    tshape = lambda ch, dt: jax.ShapeDtypeStruct((T // tq, ch, tq), dt)
    c, sa, sb = tabs
    ct, st = tabs_t
    return pl.pallas_call(
        _nsa_inproj_kernel,
        grid=(T // tm,),
        in_specs=[row(D_MODEL), full(nw), full(wqt), full(wk), full(wvt), full(wgt),
                  row(LANES), row(LANES), row(LANES), colt, colt,
                  full(qn), full(ksn), full(kwn), full(bd)],
        out_specs=[tile(NSA_Q_WIDTH), row(kvw), row(kvw), row(kvw), row(kvw), tile(kvw), tile(kvw),
                   tile(NSA_GATE_PAD)],
        out_shape=[tshape(NSA_Q_WIDTH, BF16), jax.ShapeDtypeStruct((T, kvw), F32),
                   jax.ShapeDtypeStruct((T, kvw), F32), jax.ShapeDtypeStruct((T, kvw), BF16),
                   jax.ShapeDtypeStruct((T, kvw), BF16), tshape(kvw, BF16), tshape(kvw, BF16),
                   tshape(NSA_GATE_PAD, F32)],
        compiler_params=_cparams("parallel"),
        name="nsa_inproj",
    )(h, nw, wqt, wk, wvt, wgt, c, sa, sb, ct, st, qn, ksn, kwn, bd)


def _nsa_compress_kernel(xk_ref, xv_ref, pe_ref, w1_ref, b1_ref, w2k_ref, w2vt_ref, knw_ref,
                         c_ref, sa_ref, sb_ref, kc_ref, vct_ref):
    ncp = xk_ref.shape[2]

    def hidden(x, which):
        lo = _dot((x + pe_ref[which, 0:1, :]).astype(BF16), w1_ref[which, 0])
        hi = _dot((x + pe_ref[which, 1:2, :]).astype(BF16), w1_ref[which, 1])
        return _silu(lo + pltpu.roll(hi, ncp - 1, 0) + b1_ref[which]).astype(BF16)

    for g in range(NSA_G):
        kc = _dot(hidden(xk_ref[0, g], 0), w2k_ref[...])
        ms = jnp.sum(kc * kc, axis=-1, keepdims=True) * (1.0 / HEAD_DIM)
        kn = kc * lax.rsqrt(ms + EPS) * knw_ref[...]
        kn = _rope_lanes(kn, c_ref[...], sa_ref[...], sb_ref[...])
        kc_ref[0, g] = kn[:, :HEAD_DIM].astype(BF16)
        vct_ref[0, g] = _dot_nt(w2vt_ref[...], hidden(xv_ref[0, g], 1)).astype(BF16)


def _nsa_compress(xk, xv, pe, w1, b1, w2k, w2vt, knw, tabs):
    B, G, ncp, width = xk.shape
    c, sa, sb = tabs
    xspec = pl.BlockSpec((1, G, ncp, width), lambda b: (b, 0, 0, 0))
    tspec = pl.BlockSpec((ncp, LANES), lambda b: (b, 0))
    full = lambda a: pl.BlockSpec(a.shape, lambda b: (0,) * a.ndim)
    return pl.pallas_call(
        _nsa_compress_kernel,
        grid=(B,),
        in_specs=[xspec, xspec, full(pe), full(w1), full(b1), full(w2k), full(w2vt), full(knw),
                  tspec, tspec, tspec],
        out_specs=[pl.BlockSpec((1, G, ncp, HEAD_DIM), lambda b: (b, 0, 0, 0)),
                   pl.BlockSpec((1, G, HEAD_DIM, ncp), lambda b: (b, 0, 0, 0))],
        out_shape=[jax.ShapeDtypeStruct((B, G, ncp, HEAD_DIM), BF16),
                   jax.ShapeDtypeStruct((B, G, HEAD_DIM, ncp), BF16)],
        compiler_params=_cparams("parallel"),
        name="nsa_compress",
    )(xk, xv, pe, w1, b1, w2k, w2vt, knw, c, sa, sb)


def _flash_steps(qats, k_tiles, vt_tiles, states, mask=None):
    scores = [_dot(k, q) for k, q in zip(k_tiles, qats)]
    mid = []
    for s, (m, l, _) in zip(scores, states):
        if mask is not None:
            s = jnp.where(mask, s, -MASK_BIG)
        m_new = jnp.maximum(m, jnp.max(s, axis=0, keepdims=True))
        alpha = jnp.exp2(m - m_new)
        p = jnp.exp2(s - m_new)
        mid.append((m_new, alpha, alpha * l + jnp.sum(p, axis=0, keepdims=True), p.astype(BF16)))
    return [(m_new, l_new, alpha * acc + _dot(vt, p))
            for (m_new, alpha, l_new, p), vt, (_, _, acc) in zip(mid, vt_tiles, states)]


def _nsa_attn_kernel(qt_ref, kc_ref, vct_ref, ks_ref, vst_ref, kw_ref, vwt_ref, gt_ref, ovl_ref, o_ref,
                     ksa_ref, kwa_ref, qat_ref, part_ref):
    S = ks_ref.shape[0]
    ncp = kc_ref.shape[2]
    nblk = S // SEL_BLOCK
    G, HP, dh, tq = NSA_G, NSA_HPG, HEAD_DIM, NSA_TQ
    cols = HP * tq
    qi = pl.program_id(1)
    q0 = qi * tq

    @pl.when(qi == 0)
    def _():
        rblk = lax.broadcasted_iota(jnp.int32, (S, dh), 0) >> SEL_SHIFT
        lane = lax.broadcasted_iota(jnp.int32, (S, dh), 1)
        onehot = jnp.where(rblk == lane, 1.0, 0.0).astype(BF16)
        zeros = jnp.zeros((S, dh), BF16)
        for g in range(G):
            ksa_ref[g, :, 0:dh] = ks_ref[:, dh * g:dh * (g + 1)]
            ksa_ref[g, :, dh:2 * dh] = onehot
            kwa_ref[g, :, 0:dh] = kw_ref[:, dh * g:dh * (g + 1)]
            kwa_ref[g, :, dh:2 * dh] = zeros

    t_cols = q0 + (lax.broadcasted_iota(jnp.int32, (1, cols), 1) & (tq - 1))
    t_q = t_cols[:, 0:tq]
    k_loc = lax.broadcasted_iota(jnp.int32, (tq, 1), 0)

    cmp_end = lax.broadcasted_iota(jnp.int32, (ncp, 1), 0) * CMP_STRIDE + (CMP_BLOCK - 1)
    cmask = cmp_end <= t_cols
    jb = lax.broadcasted_iota(jnp.int32, (nblk, tq), 0)
    tblk = t_q >> SEL_SHIFT
    forced = (jb == 0) | (jb == tblk) | (jb == tblk - 1)
    gt = gt_ref[0]

    def gate_row(g, branch):
        return jnp.concatenate([gt[3 * (g * HP + h) + branch:3 * (g * HP + h) + branch + 1] for h in range(HP)],
                               axis=1)

    init = (jnp.full((1, cols), M_INIT, F32), jnp.zeros((1, cols), F32), jnp.zeros((dh, cols), F32))
    causal = (q0 + k_loc) <= t_cols
    n_back = (WINDOW + tq - 1) // tq
    back = []
    for dk in range(1, n_back + 1):
        kt = qi - dk
        far = jnp.where(kt < 0, 2 * WINDOW + S, 0)
        back.append((jnp.maximum(kt, 0), (t_cols - (kt * tq + k_loc) + far) < WINDOW))

    grp = range(G)
    vrow = lambda g: slice(dh * g, dh * (g + 1))
    for g in grp:
        qat_ref[g, dh + nblk:2 * dh, :] = jnp.zeros((dh - nblk, cols), BF16)
        for h in range(HP):
            hd = g * HP + h
            qat_ref[g, 0:dh, h * tq:(h + 1) * tq] = qt_ref[0, dh * hd:dh * (hd + 1), :]

    sc = [_dot(kc_ref[0, g], qat_ref[g, 0:dh, :]) for g in grp]
    pc = []
    for g in grp:
        s = jnp.where(cmask, sc[g], -MASK_BIG)
        m = jnp.max(s, axis=0, keepdims=True)
        p = jnp.where(cmask, jnp.exp2(s - m), 0.0)
        l = jnp.sum(p, axis=0, keepdims=True)
        pc.append(p * jnp.where(l > 0.0, 1.0 / l, 0.0))
    oc = [_dot(vct_ref[0, g], pc[g].astype(BF16)) for g in grp]
    ovl = ovl_ref[...]
    imp = []
    for g in grp:
        psum = pc[g][:, 0:tq]
        for h in range(1, HP):
            psum = psum + pc[g][:, h * tq:(h + 1) * tq]
        p1, p2, p3 = _split3(psum)
        imp.append((_dot(ovl, p1) + _dot(ovl, p2) + _dot(ovl, p3))[0:nblk])
    for g in grp:
        v = jnp.where(forced, MASK_BIG, jnp.where(jb > tblk, -MASK_BIG, imp[g]))
        cnt = jnp.zeros((nblk, tq), jnp.int32)
        for j in range(nblk):
            rj = v[j:j + 1, :]
            beats = (rj > v) | ((rj == v) & (jb > j))
            cnt = cnt + jnp.where(beats, 1, 0)
        selneg = jnp.where(cnt < SEL_TOPK, 0.0, -MASK_BIG).astype(BF16)
        for h in range(HP):
            qat_ref[g, dh:dh + nblk, h * tq:(h + 1) * tq] = selneg
    qats = [qat_ref[g] for g in grp]

    states = _flash_steps(qats, [kwa_ref[g, pl.ds(q0, tq), :] for g in grp],
                          [vwt_ref[qi, vrow(g), :] for g in grp], [init] * G, causal)
    for kt, inside in back:
        k0 = pl.multiple_of(kt * tq, tq)
        states = _flash_steps(qats, [kwa_ref[g, pl.ds(k0, tq), :] for g in grp],
                              [vwt_ref[kt, vrow(g), :] for g in grp], states, inside)
    for g in grp:
        _, l_w, acc_w = states[g]
        part_ref[g] = gate_row(g, 0) * oc[g] + gate_row(g, 2) * (acc_w * (1.0 / l_w))

    sel_state = _flash_steps(qats, [ksa_ref[g, pl.ds(q0, tq), :] for g in grp],
                             [vst_ref[qi, vrow(g), :] for g in grp], [init] * G, causal)

    def sel_body(kt, states):
        k0 = pl.multiple_of(kt * tq, tq)
        return tuple(_flash_steps([qat_ref[g] for g in grp], [ksa_ref[g, pl.ds(k0, tq), :] for g in grp],
                                  [vst_ref[kt, vrow(g), :] for g in grp], states))

    sel_state = lax.fori_loop(0, qi, sel_body, tuple(sel_state))

    for g in range(G):
        _, l_s, acc_s = sel_state[g]
        og = part_ref[g] + gate_row(g, 1) * (acc_s * (1.0 / l_s))
        og_t = jnp.concatenate([og[:, h * tq:(h + 1) * tq] for h in range(HP)], axis=0)
        o_ref[:, HP * dh * g:HP * dh * (g + 1)] = og_t.T.astype(BF16)


def _nsa_attn(qt, kc, vct, ks, vst, kw, vwt, gt, ovl, B, S):
    T = B * S
    tq = NSA_TQ
    nq = S // tq
    G, dh = NSA_G, HEAD_DIM
    ncp = kc.shape[2]
    qspec = pl.BlockSpec((1, NSA_Q_WIDTH, tq), lambda b, i: (b * nq + i, 0, 0))
    gspec = pl.BlockSpec((1, NSA_GATE_PAD, tq), lambda b, i: (b * nq + i, 0, 0))
    kcspec = pl.BlockSpec((1, G, ncp, dh), lambda b, i: (b, 0, 0, 0))
    vcspec = pl.BlockSpec((1, G, dh, ncp), lambda b, i: (b, 0, 0, 0))
    kspec = pl.BlockSpec((S, NSA_KV_WIDTH), lambda b, i: (b, 0))
    vspec = pl.BlockSpec((nq, NSA_KV_WIDTH, tq), lambda b, i: (b, 0, 0))
    ovspec = pl.BlockSpec(ovl.shape, lambda b, i: (0, 0))
    return pl.pallas_call(
        _nsa_attn_kernel,
        grid=(B, nq),
        in_specs=[qspec, kcspec, vcspec, kspec, vspec, kspec, vspec, gspec, ovspec],
        out_specs=pl.BlockSpec((tq, NSA_Q_WIDTH), lambda b, i: (b * nq + i, 0)),
        out_shape=jax.ShapeDtypeStruct((T, NSA_Q_WIDTH), BF16),
        scratch_shapes=[pltpu.VMEM((G, S, 2 * dh), BF16), pltpu.VMEM((G, S, 2 * dh), BF16),
                        pltpu.VMEM((G, 2 * dh, NSA_HPG * tq), BF16),
                        pltpu.VMEM((G, dh, NSA_HPG * tq), F32)],
        compiler_params=_cparams("arbitrary", "arbitrary"),
        name="nsa_attn",
    )(qt, kc, vct, ks, vst, kw, vwt, gt, ovl)


def _proj_res_kernel(a_ref, w_ref, r_ref, o_ref):
    o_ref[...] = r_ref[...] + _dot(a_ref[...], w_ref[...])


def _proj_res(a, w, res, tm=512):
    T, K = a.shape
    N = w.shape[1]
    return pl.pallas_call(
        _proj_res_kernel,
        grid=(T // tm,),
        in_specs=[pl.BlockSpec((tm, K), lambda i: (i, 0)), pl.BlockSpec((K, N), lambda i: (0, 0)),
                  pl.BlockSpec((tm, N), lambda i: (i, 0))],
        out_specs=pl.BlockSpec((tm, N), lambda i: (i, 0)),
        out_shape=jax.ShapeDtypeStruct((T, N), F32),
        compiler_params=_cparams("parallel"),
        name="proj_residual",
    )(a, w, res)


def _ffn_kernel(x_ref, nw_ref, wg_ref, wu_ref, wd_ref, o_ref, xn_ref, acc_ref):
    k = pl.program_id(1)

    @pl.when(k == 0)
    def _():
        xn_ref[...] = _rms_rows(x_ref[...], nw_ref[...]).astype(BF16)
        acc_ref[...] = x_ref[...]

    xn = xn_ref[...]
    a = (_silu(_dot(xn, wg_ref[...])) * _dot(xn, wu_ref[...])).astype(BF16)
    acc_ref[...] += _dot(a, wd_ref[...])

    @pl.when(k == pl.num_programs(1) - 1)
    def _():
        o_ref[...] = acc_ref[...]


def _ffn(h, nw, wg, wu, wd, tm=512, th=1408):
    T = h.shape[0]
    H = wg.shape[1]
    return pl.pallas_call(
        _ffn_kernel,
        grid=(T // tm, H // th),
        in_specs=[pl.BlockSpec((tm, D_MODEL), lambda i, k: (i, 0)),
                  pl.BlockSpec((1, D_MODEL), lambda i, k: (0, 0)),
                  pl.BlockSpec((D_MODEL, th), lambda i, k: (0, k)),
                  pl.BlockSpec((D_MODEL, th), lambda i, k: (0, k)),
                  pl.BlockSpec((th, D_MODEL), lambda i, k: (k, 0))],
        out_specs=pl.BlockSpec((tm, D_MODEL), lambda i, k: (i, 0)),
        out_shape=jax.ShapeDtypeStruct((T, D_MODEL), F32),
        scratch_shapes=[pltpu.VMEM((tm, D_MODEL), BF16), pltpu.VMEM((tm, D_MODEL), F32)],
        compiler_params=_cparams("parallel", "arbitrary"),
        name="ffn",
    )(h, nw, wg, wu, wd)


def _ssm_inproj_kernel(x_ref, nw_ref, w_ref, z_ref, xbc_ref, dt_ref):
    xn = _rms_rows(x_ref[...], nw_ref[...]).astype(BF16)
    chunk = 1024
    for lo in range(0, SSM_D_INNER, chunk):
        z_ref[:, lo:lo + chunk] = _dot(xn, w_ref[:, lo:lo + chunk])
    for lo in range(0, SSM_CONV_DIM, chunk):
        xbc_ref[:, lo:lo + chunk] = _dot(xn, w_ref[:, SSM_D_INNER + lo:SSM_D_INNER + lo + chunk])
    base = SSM_D_INNER + SSM_CONV_DIM
    dt_ref[...] = _dot(xn, w_ref[:, base:base + SSM_DT_PAD])


def _ssm_inproj(h, nw, w, tm=256):
    T = h.shape[0]
    row = lambda width: pl.BlockSpec((tm, width), lambda i: (i, 0))
    full = lambda a: pl.BlockSpec(a.shape, lambda i: (0,) * a.ndim)
    widths = (SSM_D_INNER, SSM_CONV_DIM, SSM_DT_PAD)
    return pl.pallas_call(
        _ssm_inproj_kernel,
        grid=(T // tm,),
        in_specs=[row(D_MODEL), full(nw), full(w)],
        out_specs=[row(wd) for wd in widths],
        out_shape=[jax.ShapeDtypeStruct((T, wd), F32) for wd in widths],
        compiler_params=_cparams("parallel"),
        name="ssm_inproj",
    )(h, nw, w)


def _ssd_kernel(xbc_ref, z_ref, dt_ref, cw_ref, cb_ref, dtb_ref, alog_ref, dsk_ref, nw_ref, tri_ref,
                y_ref, ext_ref, state_ref):
    Q, P, N, G, HPG = SSM_CHUNK, SSM_P, SSM_N, SSM_G, SSM_HPG
    c = pl.program_id(1)
    halo = 8

    @pl.when(c == 0)
    def _():
        ext_ref[0:halo, :] = jnp.zeros((halo, SSM_CONV_DIM), F32)
        state_ref[...] = jnp.zeros_like(state_ref)

    x = xbc_ref[...]
    ext_ref[halo:halo + Q, :] = x
    acc = cb_ref[...] + cw_ref[SSM_CONV - 1:SSM_CONV, :] * x
    for k in range(SSM_CONV - 1):
        sh = SSM_CONV - 1 - k
        acc = acc + cw_ref[k:k + 1, :] * ext_ref[halo - sh:halo - sh + Q, :]
    ext_ref[0:halo, :] = x[Q - halo:Q, :]
    act = _silu(acc)
    xs = act[:, 0:SSM_D_INNER]
    xs_b = xs.astype(BF16)
    bm = act[:, SSM_D_INNER:SSM_D_INNER + G * N]
    cm_b = act[:, SSM_D_INNER + G * N:SSM_D_INNER + 2 * G * N].astype(BF16)

    dtl = dt_ref[...] + dtb_ref[...]
    dt = jnp.maximum(dtl, 0.0) + jnp.log(1.0 + jnp.exp(-jnp.abs(dtl)))
    a = dt * (-jnp.exp(alog_ref[...]))
    a1, a2, a3 = _split3(a)
    tri = tri_ref[...]
    cum = _dot(tri, a1) + _dot(tri, a2) + _dot(tri, a3)
    cum_t = cum.T
    dt_t = dt.T
    row_i = lax.broadcasted_iota(jnp.int32, (Q, Q), 0)
    col_i = lax.broadcasted_iota(jnp.int32, (Q, Q), 1)
    tril = row_i >= col_i

    y_parts = []
    for g in range(G):
        cg = cm_b[:, N * g:N * (g + 1)]
        bg = bm[:, N * g:N * (g + 1)]
        cb = _dot_nt(cg, bg.astype(BF16))
        bg_t = bg.T
        st_g = state_ref[g]
        y_off = _dot(cg, st_g.astype(BF16))
        new_state = []
        for hh in range(HPG):
            h = g * HPG + hh
            cum_col = cum[:, h:h + 1]
            cum_row = cum_t[h:h + 1, :]
            dt_row = dt_t[h:h + 1, :]
            cum_last = cum_row[:, Q - 1:Q]
            L = jnp.exp(jnp.where(tril, cum_col - cum_row, -jnp.inf))
            mm = (cb * L * dt_row).astype(BF16)
            xh_b = xs_b[:, P * h:P * (h + 1)]
            yh = _dot(mm, xh_b) + jnp.exp(cum_col) * y_off[:, P * hh:P * (hh + 1)]
            yh = yh + dsk_ref[:, P * h:P * (h + 1)] * xs[:, P * h:P * (h + 1)]
            y_parts.append(yh)
            wgt = (bg_t * (dt_row * jnp.exp(cum_last - cum_row))).astype(BF16)
            new_state.append(st_g[:, P * hh:P * (hh + 1)] * jnp.exp(cum_last) + _dot(wgt, xh_b))
        state_ref[g] = jnp.concatenate(new_state, axis=1)

    gw = SSM_D_INNER // G
    for g in range(G):
        yg = jnp.concatenate(y_parts[g * HPG:(g + 1) * HPG], axis=1)
        yg = yg * _silu(z_ref[:, gw * g:gw * (g + 1)])
        yg = yg * lax.rsqrt(jnp.mean(yg * yg, axis=-1, keepdims=True) + EPS)
        y_ref[:, gw * g:gw * (g + 1)] = (yg * nw_ref[:, gw * g:gw * (g + 1)]).astype(BF16)


def _ssd(xbc, z, dt, cw, cb, dtb, alog, dsk, nw, tri, B, S):
    Q = SSM_CHUNK
    nch = S // Q
    row = lambda width: pl.BlockSpec((Q, width), lambda b, c: (b * nch + c, 0))
    full = lambda a: pl.BlockSpec(a.shape, lambda b, c: (0,) * a.ndim)
    return pl.pallas_call(
        _ssd_kernel,
        grid=(B, nch),
        in_specs=[row(SSM_CONV_DIM), row(SSM_D_INNER), row(SSM_DT_PAD), full(cw), full(cb), full(dtb),
                  full(alog), full(dsk), full(nw), full(tri)],
        out_specs=row(SSM_D_INNER),
        out_shape=jax.ShapeDtypeStruct((B * S, SSM_D_INNER), BF16),
        scratch_shapes=[pltpu.VMEM((8 + Q, SSM_CONV_DIM), F32),
                        pltpu.VMEM((SSM_G, SSM_N, SSM_HPG * SSM_P), F32)],
        compiler_params=_cparams("arbitrary", "arbitrary"),
        name="ssd_scan",
    )(xbc, z, dt, cw, cb, dtb, alog, dsk, nw, tri)


def _block_diag_mean():
    i = np.arange(LANES)
    return jnp.asarray((i[:, None] // HEAD_DIM == i[None, :] // HEAD_DIM) / HEAD_DIM, BF16)


def _overlap_t(S):
    nc = (S - CMP_BLOCK) // CMP_STRIDE + 1
    ncp = S // CMP_STRIDE
    nblk = S // SEL_BLOCK
    starts = np.arange(ncp) * CMP_STRIDE
    js = np.arange(nblk)[:, None] * SEL_BLOCK
    ov = (starts[None, :] < js + SEL_BLOCK) & (starts[None, :] + CMP_BLOCK > js) & (np.arange(ncp)[None, :] < nc)
    out = np.zeros((LANES, ncp), np.float32)
    out[:nblk] = ov
    return jnp.asarray(out, BF16)


def _pad_cols(w, width):
    return jnp.pad(w, ((0, 0), (0, width - w.shape[1])))


def _nsa_layer(h, tabs, tabs_t, tabs_c, B, S, nw, w_in, q_norm, k_norm, cmp_pe, cmp_w1, cmp_b1, cmp_w2, w_out):
    G, dh, kvw = NSA_G, HEAD_DIM, NSA_KV_WIDTH
    ncp = S // CMP_STRIDE
    cut = lambda i: w_in[:, NSA_Q_WIDTH + i * kvw:NSA_Q_WIDTH + (i + 1) * kvw]
    wqt = w_in[:, :NSA_Q_WIDTH].T.astype(BF16)
    wk = jnp.concatenate([cut(0), cut(1), cut(2), cut(4)], axis=1).astype(BF16)
    wvt = jnp.concatenate([cut(3), cut(5)], axis=1).T.astype(BF16)
    wgt = _pad_cols(w_in[:, NSA_Q_WIDTH + 6 * kvw:], NSA_GATE_PAD).T.astype(BF16)
    qn = jnp.broadcast_to(q_norm[:, None], (dh, NSA_TQ))
    ksn = jnp.tile(k_norm[1], G)[None, :]
    kwn = jnp.tile(k_norm[2], G)[None, :]
    qt, kc_raw, vc_raw, ks, kw, vst, vwt, gt = _nsa_inproj(h, nw[None, :], wqt, wk, wvt, wgt, tabs, tabs_t,
                                                           qn, ksn, kwn, _block_diag_mean())

    def to_chunks(t):
        t = t.reshape(B, ncp, CMP_STRIDE, G, dh).transpose(0, 3, 1, 2, 4)
        return t.reshape(B, G, ncp, CMP_STRIDE * dh)

    half = CMP_STRIDE * dh
    pe = cmp_pe.reshape(2, 2, half)
    w1 = cmp_w1.reshape(2, 2, half, CMP_HIDDEN).astype(BF16)
    b1 = cmp_b1[:, None, :]
    w2k = _pad_cols(cmp_w2[0], LANES).astype(BF16)
    w2vt = cmp_w2[1].T.astype(BF16)
    knw = _pad_cols(k_norm[0][None, :], LANES)
    kc, vct = _nsa_compress(to_chunks(kc_raw), to_chunks(vc_raw), pe, w1, b1, w2k, w2vt, knw, tabs_c)

    o = _nsa_attn(qt, kc, vct, ks, vst, kw, vwt, gt, _overlap_t(S), B, S)
    return _proj_res(o, w_out.astype(BF16), h)


def _ssd_layer(h, B, S, nw, w_in, conv_w, conv_b, dt_bias, a_log, d_skip, norm_w, w_out):
    w = _pad_cols(w_in, SSM_IN_PAD).astype(BF16)
    z, xbc, dt = _ssm_inproj(h, nw[None, :], w)
    pad1 = lambda v: _pad_cols(v[None, :], SSM_DT_PAD)
    dsk = jnp.repeat(d_skip, SSM_P)[None, :]
    tri = jnp.asarray(np.tril(np.ones((SSM_CHUNK, SSM_CHUNK), np.float32)), BF16)
    y = _ssd(xbc, z, dt, conv_w, conv_b[None, :], pad1(dt_bias), pad1(a_log), dsk, norm_w[None, :], tri, B, S)
    return _proj_res(y, w_out.astype(BF16), h)


def kernel(x, positions, mix_norm_w, ffn_norm_w, ffn_w_gate, ffn_w_up, ffn_w_down, nsa_w_in, nsa_q_norm, nsa_k_norm, nsa_cmp_pe, nsa_cmp_w1, nsa_cmp_b1, nsa_cmp_w2, nsa_w_out, ssm_w_in, ssm_conv_w, ssm_conv_b, ssm_dt_bias, ssm_a_log, ssm_d, ssm_norm_w, ssm_w_out):
    B, S, D = x.shape
    T = B * S
    h = x.reshape(T, D)
    ncp = S // CMP_STRIDE
    tabs = _rope_tables(positions.reshape(T, 1), 1024)
    tabs_t = _rope_tables_t(positions.reshape(1, T), 2048)
    pos_c = jnp.pad(positions[:, CMP_BLOCK - 1::CMP_STRIDE], ((0, 0), (0, 1)))[:, :ncp]
    tabs_c = _rope_tables(pos_c.reshape(B * ncp, 1), ncp)
    for i in range(DEPTH):
        j = i // 2
        if i % 2 == 0:
            h = _nsa_layer(h, tabs, tabs_t, tabs_c, B, S, mix_norm_w[i], nsa_w_in[j], nsa_q_norm[j], nsa_k_norm[j],
                           nsa_cmp_pe[j], nsa_cmp_w1[j], nsa_cmp_b1[j], nsa_cmp_w2[j], nsa_w_out[j])
        else:
            h = _ssd_layer(h, B, S, mix_norm_w[i], ssm_w_in[j], ssm_conv_w[j], ssm_conv_b[j], ssm_dt_bias[j],
                           ssm_a_log[j], ssm_d[j], ssm_norm_w[j], ssm_w_out[j])
        h = _ffn(h, ffn_norm_w[i][None, :], ffn_w_gate[i].astype(BF16), ffn_w_up[i].astype(BF16),
                 ffn_w_down[i].astype(BF16))
    return h.reshape(B, S, D)
```

```python
import functools
import math

import numpy as np
import jax
import jax.numpy as jnp
from jax import lax
from jax.experimental import pallas as pl
from jax.experimental.pallas import tpu as pltpu

F32 = jnp.float32
BF16 = jnp.bfloat16

D_MODEL = 1024
DEPTH = 4
EPS = 1e-6

NSA_HEADS = 16
NSA_G = 4
NSA_HPG = NSA_HEADS // NSA_G
HEAD_DIM = 64
CMP_BLOCK = 32
CMP_STRIDE = 16
CMP_HIDDEN = 256
SEL_BLOCK = 64
SEL_SHIFT = 6
SEL_TOPK = 8
WINDOW = 512
ROPE_THETA = 500000.0
ROPE_DIM = HEAD_DIM // 4
ROPE_HALF = ROPE_DIM // 2
NSA_Q_WIDTH = NSA_HEADS * HEAD_DIM
NSA_KV_WIDTH = NSA_G * HEAD_DIM
NSA_GATE_PAD = 128
NSA_TQ = 256
QSCALE = HEAD_DIM ** -0.5 * math.log2(math.e)

SSM_D_INNER = 2 * D_MODEL
SSM_P = 64
SSM_HEADS = SSM_D_INNER // SSM_P
SSM_G = 4
SSM_HPG = SSM_HEADS // SSM_G
SSM_N = 128
SSM_CONV = 4
SSM_CHUNK = 128
SSM_CONV_DIM = SSM_D_INNER + 2 * SSM_G * SSM_N
SSM_DT_PAD = 128
SSM_HALO = 8
SSM_IN_PAD = SSM_D_INNER + SSM_CONV_DIM + SSM_DT_PAD

FFN_HIDDEN = -(-8 * D_MODEL // (3 * 256)) * 256

LANES = 128
VMEM_LIMIT_BYTES = 52 * 1024 * 1024

MASK_BIG = 1e30
M_INIT = -3e38

_NT = (((1,), (1,)), ((), ()))


def _cparams(*sem, flags=None):
    return pltpu.CompilerParams(dimension_semantics=sem, vmem_limit_bytes=VMEM_LIMIT_BYTES, flags=flags)


def _dot(a, b):
    return jnp.dot(a, b, preferred_element_type=F32)


def _dot_nt(a, b):
    return lax.dot_general(a, b, _NT, preferred_element_type=F32)


def _split3(x):
    a = x.astype(BF16)
    r = x - a.astype(F32)
    b = r.astype(BF16)
    c = (r - b.astype(F32)).astype(BF16)
    return a, b, c


def _rms_rows(x, w):
    return x * lax.rsqrt(jnp.mean(x * x, axis=-1, keepdims=True) + EPS) * w


def _silu(x):
    h = 0.5 * x
    return h + h * jnp.tanh(h)


def _rope_table_kernel(pos_ref, c_ref, sa_ref, sb_ref):
    pos = pos_ref[...].astype(F32)
    lane = lax.broadcasted_iota(jnp.int32, (1, LANES), 1)
    d = lane & (HEAD_DIM - 1)
    f = d & (ROPE_HALF - 1)
    inv = jnp.zeros((1, LANES), F32)
    for i in range(ROPE_HALF):
        inv = jnp.where(f == i, float(np.power(np.float32(ROPE_THETA), np.float32(-i / ROPE_HALF))), inv)
    ang = pos * inv
    cos, sin = jnp.cos(ang), jnp.sin(ang)
    c_ref[...] = jnp.where(d < ROPE_DIM, cos, 1.0)
    sa_ref[...] = jnp.where(d < ROPE_HALF, -sin, 0.0)
    sb_ref[...] = jnp.where((d >= ROPE_HALF) & (d < ROPE_DIM), sin, 0.0)


def _rope_tables(pos_col, tm):
    n = pos_col.shape[0]
    out = jax.ShapeDtypeStruct((n, LANES), F32)
    spec = pl.BlockSpec((tm, LANES), lambda i: (i, 0))
    return pl.pallas_call(
        _rope_table_kernel,
        grid=(n // tm,),
        in_specs=[pl.BlockSpec((tm, 1), lambda i: (i, 0))],
        out_specs=[spec, spec, spec],
        out_shape=[out, out, out],
        compiler_params=_cparams("parallel"),
        name="rope_tables",
    )(pos_col)


def _rope_table_t_kernel(pos_ref, c_ref, s_ref):
    pos = pos_ref[...].astype(F32)
    f = lax.broadcasted_iota(jnp.int32, (ROPE_HALF, 1), 0)
    inv = jnp.zeros((ROPE_HALF, 1), F32)
    for i in range(ROPE_HALF):
        inv = jnp.where(f == i, float(np.power(np.float32(ROPE_THETA), np.float32(-i / ROPE_HALF))), inv)
    ang = inv * pos
    c_ref[...] = jnp.cos(ang)
    s_ref[...] = jnp.sin(ang)


def _rope_tables_t(pos_row, tm):
    n = pos_row.shape[1]
    out = jax.ShapeDtypeStruct((ROPE_HALF, n), F32)
    spec = pl.BlockSpec((ROPE_HALF, tm), lambda i: (0, i))
    return pl.pallas_call(
        _rope_table_t_kernel,
        grid=(n // tm,),
        in_specs=[pl.BlockSpec((1, tm), lambda i: (0, i))],
        out_specs=[spec, spec],
        out_shape=[out, out],
        compiler_params=_cparams("parallel"),
        name="rope_tables_t",
    )(pos_row)


def _rope_lanes(x, c, sa, sb):
    return x * c + pltpu.roll(x, LANES - ROPE_HALF, 1) * sa + pltpu.roll(x, ROPE_HALF, 1) * sb


def _head_norm_rope(y, w, bd, c, sa, sb, scale):
    outs = []
    for j in range(y.shape[1] // LANES):
        yc = y[:, LANES * j:LANES * (j + 1)]
        sq = yc * yc
        hi = sq.astype(BF16)
        lo = (sq - hi.astype(F32)).astype(BF16)
        ms = _dot(hi, bd) + _dot(lo, bd)
        yn = yc * lax.rsqrt(ms + EPS) * w[:, LANES * j:LANES * (j + 1)]
        outs.append(_rope_lanes(yn, c, sa, sb) * scale)
    return jnp.concatenate(outs, axis=1)


def _nsa_inproj_kernel(x_ref, nw_ref, wqt_ref, wk_ref, wvt_ref, wgt_ref, c_ref, sa_ref, sb_ref, ct_ref, st_ref,
                       qn_ref, ksn_ref, kwn_ref, bd_ref,
                       qt_ref, kc_ref, vc_ref, ks_ref, kw_ref, vst_ref, vwt_ref, gt_ref):
    xn = _rms_rows(x_ref[...], nw_ref[...]).astype(BF16)
    c, sa, sb, bd = c_ref[...], sa_ref[...], sb_ref[...], bd_ref[...]
    kvw, dh, tq = NSA_KV_WIDTH, HEAD_DIM, NSA_TQ

    def proj(lo):
        return _dot(xn, wk_ref[:, lo:lo + kvw])

    kc_ref[...] = proj(0)
    vc_ref[...] = proj(kvw)
    ks_ref[...] = _head_norm_rope(proj(2 * kvw), ksn_ref[...], bd, c, sa, sb, 1.0).astype(BF16)
    kw_ref[...] = _head_norm_rope(proj(3 * kvw), kwn_ref[...], bd, c, sa, sb, 1.0).astype(BF16)

    qn = qn_ref[...]
    for ch in range(x_ref.shape[0] // tq):
        xc = xn[ch * tq:(ch + 1) * tq]
        cos, sin = ct_ref[:, ch * tq:(ch + 1) * tq], st_ref[:, ch * tq:(ch + 1) * tq]
        yt = _dot_nt(wqt_ref[...], xc)
        for hd in range(NSA_HEADS):
            yh = yt[dh * hd:dh * (hd + 1)]
            yn = yh * lax.rsqrt(jnp.mean(yh * yh, axis=0, keepdims=True) + EPS) * qn
            x1, x2 = yn[0:ROPE_HALF], yn[ROPE_HALF:ROPE_DIM]
            rot = jnp.concatenate([x1 * cos - x2 * sin, x2 * cos + x1 * sin, yn[ROPE_DIM:]], axis=0)
            qt_ref[ch, dh * hd:dh * (hd + 1), :] = (rot * QSCALE).astype(BF16)
        vt = _dot_nt(wvt_ref[...], xc)
        vst_ref[ch] = vt[0:kvw].astype(BF16)
        vwt_ref[ch] = vt[kvw:2 * kvw].astype(BF16)
        gt_ref[ch] = 1.0 / (1.0 + jnp.exp(-_dot_nt(wgt_ref[...], xc)))


def _nsa_inproj(h, nw, wqt, wk, wvt, wgt, tabs, tabs_t, qn, ksn, kwn, bd, tm=512):
    T = h.shape[0]
    tq, kvw = NSA_TQ, NSA_KV_WIDTH
    row = lambda width: pl.BlockSpec((tm, width), lambda i: (i, 0))
    full = lambda a: pl.BlockSpec(a.shape, lambda i: (0,) * a.ndim)
    colt = pl.BlockSpec((ROPE_HALF, tm), lambda i: (0, i))
    tile = lambda ch: pl.BlockSpec((tm // tq, ch, tq), lambda i: (i, 0, 0))
    tshape = lambda ch, dt: jax.ShapeDtypeStruct((T // tq, ch, tq), dt)
    c, sa, sb = tabs
    ct, st = tabs_t
    return pl.pallas_call(
        _nsa_inproj_kernel,
        grid=(T // tm,),
        in_specs=[row(D_MODEL), full(nw), full(wqt), full(wk), full(wvt), full(wgt),
                  row(LANES), row(LANES), row(LANES), colt, colt,
                  full(qn), full(ksn), full(kwn), full(bd)],
        out_specs=[tile(NSA_Q_WIDTH), row(kvw), row(kvw), row(kvw), row(kvw), tile(kvw), tile(kvw),
                   tile(NSA_GATE_PAD)],
        out_shape=[tshape(NSA_Q_WIDTH, BF16), jax.ShapeDtypeStruct((T, kvw), F32),
                   jax.ShapeDtypeStruct((T, kvw), F32), jax.ShapeDtypeStruct((T, kvw), BF16),
                   jax.ShapeDtypeStruct((T, kvw), BF16), tshape(kvw, BF16), tshape(kvw, BF16),
                   tshape(NSA_GATE_PAD, F32)],
        compiler_params=_cparams("parallel"),
        name="nsa_inproj",
    )(h, nw, wqt, wk, wvt, wgt, c, sa, sb, ct, st, qn, ksn, kwn, bd)


def _nsa_compress_kernel(xk_ref, xv_ref, pe_ref, w1_ref, b1_ref, w2k_ref, w2vt_ref, knw_ref,
                         c_ref, sa_ref, sb_ref, kc_ref, vct_ref):
    ncp = xk_ref.shape[2]

    def hidden(x, which):
        lo = _dot((x + pe_ref[which, 0:1, :]).astype(BF16), w1_ref[which, 0])
        hi = _dot((x + pe_ref[which, 1:2, :]).astype(BF16), w1_ref[which, 1])
        return _silu(lo + pltpu.roll(hi, ncp - 1, 0) + b1_ref[which]).astype(BF16)

    for g in range(NSA_G):
        kc = _dot(hidden(xk_ref[0, g], 0), w2k_ref[...])
        ms = jnp.sum(kc * kc, axis=-1, keepdims=True) * (1.0 / HEAD_DIM)
        kn = kc * lax.rsqrt(ms + EPS) * knw_ref[...]
        kn = _rope_lanes(kn, c_ref[...], sa_ref[...], sb_ref[...])
        kc_ref[0, g] = kn[:, :HEAD_DIM].astype(BF16)
        vct_ref[0, g] = _dot_nt(w2vt_ref[...], hidden(xv_ref[0, g], 1)).astype(BF16)


def _nsa_compress(xk, xv, pe, w1, b1, w2k, w2vt, knw, tabs):
    B, G, ncp, width = xk.shape
    c, sa, sb = tabs
    xspec = pl.BlockSpec((1, G, ncp, width), lambda b: (b, 0, 0, 0))
    tspec = pl.BlockSpec((ncp, LANES), lambda b: (b, 0))
    full = lambda a: pl.BlockSpec(a.shape, lambda b: (0,) * a.ndim)
    return pl.pallas_call(
        _nsa_compress_kernel,
        grid=(B,),
        in_specs=[xspec, xspec, full(pe), full(w1), full(b1), full(w2k), full(w2vt), full(knw),
                  tspec, tspec, tspec],
        out_specs=[pl.BlockSpec((1, G, ncp, HEAD_DIM), lambda b: (b, 0, 0, 0)),
                   pl.BlockSpec((1, G, HEAD_DIM, ncp), lambda b: (b, 0, 0, 0))],
        out_shape=[jax.ShapeDtypeStruct((B, G, ncp, HEAD_DIM), BF16),
                   jax.ShapeDtypeStruct((B, G, HEAD_DIM, ncp), BF16)],
        compiler_params=_cparams("parallel"),
        name="nsa_compress",
    )(xk, xv, pe, w1, b1, w2k, w2vt, knw, c, sa, sb)


def _flash_steps(qats, k_tiles, vt_tiles, states, mask=None):
    scores = [_dot(k, q) for k, q in zip(k_tiles, qats)]
    mid = []
    for s, (m, l, _) in zip(scores, states):
        if mask is not None:
            s = jnp.where(mask, s, -MASK_BIG)
        m_new = jnp.maximum(m, jnp.max(s, axis=0, keepdims=True))
        alpha = jnp.exp2(m - m_new)
        p = jnp.exp2(s - m_new)
        mid.append((m_new, alpha, alpha * l + jnp.sum(p, axis=0, keepdims=True), p.astype(BF16)))
    return [(m_new, l_new, alpha * acc + _dot(vt, p))
            for (m_new, alpha, l_new, p), vt, (_, _, acc) in zip(mid, vt_tiles, states)]


def _nsa_attn_kernel(qt_ref, kc_ref, vct_ref, ks_ref, vst_ref, kw_ref, vwt_ref, gt_ref, ovl_ref, o_ref,
                     ksa_ref, kwa_ref, qat_ref, part_ref):
    S = ks_ref.shape[0]
    ncp = kc_ref.shape[2]
    nblk = S // SEL_BLOCK
    G, HP, dh, tq = NSA_G, NSA_HPG, HEAD_DIM, NSA_TQ
    cols = HP * tq
    qi = pl.program_id(1)
    q0 = qi * tq

    @pl.when(qi == 0)
    def _():
        rblk = lax.broadcasted_iota(jnp.int32, (S, dh), 0) >> SEL_SHIFT
        lane = lax.broadcasted_iota(jnp.int32, (S, dh), 1)
        onehot = jnp.where(rblk == lane, 1.0, 0.0).astype(BF16)
        zeros = jnp.zeros((S, dh), BF16)
        for g in range(G):
            ksa_ref[g, :, 0:dh] = ks_ref[:, dh * g:dh * (g + 1)]
            ksa_ref[g, :, dh:2 * dh] = onehot
            kwa_ref[g, :, 0:dh] = kw_ref[:, dh * g:dh * (g + 1)]
            kwa_ref[g, :, dh:2 * dh] = zeros

    t_cols = q0 + (lax.broadcasted_iota(jnp.int32, (1, cols), 1) & (tq - 1))
    t_q = t_cols[:, 0:tq]
    k_loc = lax.broadcasted_iota(jnp.int32, (tq, 1), 0)

    cmp_end = lax.broadcasted_iota(jnp.int32, (ncp, 1), 0) * CMP_STRIDE + (CMP_BLOCK - 1)
    cmask = cmp_end <= t_cols
    jb = lax.broadcasted_iota(jnp.int32, (nblk, tq), 0)
    tblk = t_q >> SEL_SHIFT
    forced = (jb == 0) | (jb == tblk) | (jb == tblk - 1)
    gt = gt_ref[0]

    def gate_row(g, branch):
        return jnp.concatenate([gt[3 * (g * HP + h) + branch:3 * (g * HP + h) + branch + 1] for h in range(HP)],
                               axis=1)

    init = (jnp.full((1, cols), M_INIT, F32), jnp.zeros((1, cols), F32), jnp.zeros((dh, cols), F32))
    causal = (q0 + k_loc) <= t_cols
    n_back = (WINDOW + tq - 1) // tq
    back = []
    for dk in range(1, n_back + 1):
        kt = qi - dk
        far = jnp.where(kt < 0, 2 * WINDOW + S, 0)
        back.append((jnp.maximum(kt, 0), (t_cols - (kt * tq + k_loc) + far) < WINDOW))

    grp = range(G)
    vrow = lambda g: slice(dh * g, dh * (g + 1))
    for g in grp:
        qat_ref[g, dh + nblk:2 * dh, :] = jnp.zeros((dh - nblk, cols), BF16)
        for h in range(HP):
            hd = g * HP + h
            qat_ref[g, 0:dh, h * tq:(h + 1) * tq] = qt_ref[0, dh * hd:dh * (hd + 1), :]

    sc = [_dot(kc_ref[0, g], qat_ref[g, 0:dh, :]) for g in grp]
    pc = []
    for g in grp:
        s = jnp.where(cmask, sc[g], -MASK_BIG)
        m = jnp.max(s, axis=0, keepdims=True)
        p = jnp.where(cmask, jnp.exp2(s - m), 0.0)
        l = jnp.sum(p, axis=0, keepdims=True)
        pc.append(p * jnp.where(l > 0.0, 1.0 / l, 0.0))
    oc = [_dot(vct_ref[0, g], pc[g].astype(BF16)) for g in grp]
    ovl = ovl_ref[...]
    imp = []
    for g in grp:
        psum = pc[g][:, 0:tq]
        for h in range(1, HP):
            psum = psum + pc[g][:, h * tq:(h + 1) * tq]
        p1, p2, p3 = _split3(psum)
        imp.append((_dot(ovl, p1) + _dot(ovl, p2) + _dot(ovl, p3))[0:nblk])
    for g in grp:
        v = jnp.where(forced, MASK_BIG, jnp.where(jb > tblk, -MASK_BIG, imp[g]))
        cnt = jnp.zeros((nblk, tq), jnp.int32)
        for j in range(nblk):
            rj = v[j:j + 1, :]
            beats = (rj > v) | ((rj == v) & (jb > j))
            cnt = cnt + jnp.where(beats, 1, 0)
        selneg = jnp.where(cnt < SEL_TOPK, 0.0, -MASK_BIG).astype(BF16)
        for h in range(HP):
            qat_ref[g, dh:dh + nblk, h * tq:(h + 1) * tq] = selneg
    qats = [qat_ref[g] for g in grp]

    states = _flash_steps(qats, [kwa_ref[g, pl.ds(q0, tq), :] for g in grp],
                          [vwt_ref[qi, vrow(g), :] for g in grp], [init] * G, causal)
    for kt, inside in back:
        k0 = pl.multiple_of(kt * tq, tq)
        states = _flash_steps(qats, [kwa_ref[g, pl.ds(k0, tq), :] for g in grp],
                              [vwt_ref[kt, vrow(g), :] for g in grp], states, inside)
    for g in grp:
        _, l_w, acc_w = states[g]
        part_ref[g] = gate_row(g, 0) * oc[g] + gate_row(g, 2) * (acc_w * (1.0 / l_w))

    sel_state = _flash_steps(qats, [ksa_ref[g, pl.ds(q0, tq), :] for g in grp],
                             [vst_ref[qi, vrow(g), :] for g in grp], [init] * G, causal)

    def sel_body(kt, states):
        k0 = pl.multiple_of(kt * tq, tq)
        return tuple(_flash_steps([qat_ref[g] for g in grp], [ksa_ref[g, pl.ds(k0, tq), :] for g in grp],
                                  [vst_ref[kt, vrow(g), :] for g in grp], states))

    sel_state = lax.fori_loop(0, qi, sel_body, tuple(sel_state))

    for g in range(G):
        _, l_s, acc_s = sel_state[g]
        og = part_ref[g] + gate_row(g, 1) * (acc_s * (1.0 / l_s))
        og_t = jnp.concatenate([og[:, h * tq:(h + 1) * tq] for h in range(HP)], axis=0)
        o_ref[:, HP * dh * g:HP * dh * (g + 1)] = og_t.T.astype(BF16)


def _nsa_attn(qt, kc, vct, ks, vst, kw, vwt, gt, ovl, B, S):
    T = B * S
    tq = NSA_TQ
    nq = S // tq
    G, dh = NSA_G, HEAD_DIM
    ncp = kc.shape[2]
    qspec = pl.BlockSpec((1, NSA_Q_WIDTH, tq), lambda b, i: (b * nq + i, 0, 0))
    gspec = pl.BlockSpec((1, NSA_GATE_PAD, tq), lambda b, i: (b * nq + i, 0, 0))
    kcspec = pl.BlockSpec((1, G, ncp, dh), lambda b, i: (b, 0, 0, 0))
    vcspec = pl.BlockSpec((1, G, dh, ncp), lambda b, i: (b, 0, 0, 0))
    kspec = pl.BlockSpec((S, NSA_KV_WIDTH), lambda b, i: (b, 0))
    vspec = pl.BlockSpec((nq, NSA_KV_WIDTH, tq), lambda b, i: (b, 0, 0))
    ovspec = pl.BlockSpec(ovl.shape, lambda b, i: (0, 0))
    return pl.pallas_call(
        _nsa_attn_kernel,
        grid=(B, nq),
        in_specs=[qspec, kcspec, vcspec, kspec, vspec, kspec, vspec, gspec, ovspec],
        out_specs=pl.BlockSpec((tq, NSA_Q_WIDTH), lambda b, i: (b * nq + i, 0)),
        out_shape=jax.ShapeDtypeStruct((T, NSA_Q_WIDTH), BF16),
        scratch_shapes=[pltpu.VMEM((G, S, 2 * dh), BF16), pltpu.VMEM((G, S, 2 * dh), BF16),
                        pltpu.VMEM((G, 2 * dh, NSA_HPG * tq), BF16),
                        pltpu.VMEM((G, dh, NSA_HPG * tq), F32)],
        compiler_params=_cparams("arbitrary", "arbitrary"),
        name="nsa_attn",
    )(qt, kc, vct, ks, vst, kw, vwt, gt, ovl)


def _proj_res_kernel(a_ref, w_ref, r_ref, o_ref):
    o_ref[...] = r_ref[...] + _dot(a_ref[...], w_ref[...])


def _proj_res(a, w, res, tm=512):
    T, K = a.shape
    N = w.shape[1]
    return pl.pallas_call(
        _proj_res_kernel,
        grid=(T // tm,),
        in_specs=[pl.BlockSpec((tm, K), lambda i: (i, 0)), pl.BlockSpec((K, N), lambda i: (0, 0)),
                  pl.BlockSpec((tm, N), lambda i: (i, 0))],
        out_specs=pl.BlockSpec((tm, N), lambda i: (i, 0)),
        out_shape=jax.ShapeDtypeStruct((T, N), F32),
        compiler_params=_cparams("parallel"),
        name="proj_residual",
    )(a, w, res)


def _ffn_kernel(x_ref, nw_ref, wg_ref, wu_ref, wd_ref, o_ref, xn_ref, acc_ref):
    k = pl.program_id(1)

    @pl.when(k == 0)
    def _():
        xn_ref[...] = _rms_rows(x_ref[...], nw_ref[...]).astype(BF16)
        acc_ref[...] = x_ref[...]

    xn = xn_ref[...]
    a = (_silu(_dot(xn, wg_ref[...])) * _dot(xn, wu_ref[...])).astype(BF16)
    acc_ref[...] += _dot(a, wd_ref[...])

    @pl.when(k == pl.num_programs(1) - 1)
    def _():
        o_ref[...] = acc_ref[...]


def _ffn(h, nw, wg, wu, wd, tm=512, th=1408):
    T = h.shape[0]
    H = wg.shape[1]
    return pl.pallas_call(
        _ffn_kernel,
        grid=(T // tm, H // th),
        in_specs=[pl.BlockSpec((tm, D_MODEL), lambda i, k: (i, 0)),
                  pl.BlockSpec((1, D_MODEL), lambda i, k: (0, 0)),
                  pl.BlockSpec((D_MODEL, th), lambda i, k: (0, k)),
                  pl.BlockSpec((D_MODEL, th), lambda i, k: (0, k)),
                  pl.BlockSpec((th, D_MODEL), lambda i, k: (k, 0))],
        out_specs=pl.BlockSpec((tm, D_MODEL), lambda i, k: (i, 0)),
        out_shape=jax.ShapeDtypeStruct((T, D_MODEL), F32),
        scratch_shapes=[pltpu.VMEM((tm, D_MODEL), BF16), pltpu.VMEM((tm, D_MODEL), F32)],
        compiler_params=_cparams("parallel", "arbitrary"),
        name="ffn",
    )(h, nw, wg, wu, wd)


def _ssm_inproj_kernel(x_ref, nw_ref, w_ref, cw_ref, cb_ref, dtb_ref, zs_ref, xbc_ref, dt_ref, ext_ref,
                       *, tiles_per_seq):
    tm = x_ref.shape[0]
    halo = SSM_HALO
    chunk = 512
    xn = _rms_rows(x_ref[...], nw_ref[...]).astype(BF16)

    @pl.when(pl.program_id(0) % tiles_per_seq == 0)
    def _():
        ext_ref[0:halo, :] = jnp.zeros((halo, SSM_CONV_DIM), F32)

    for lo in range(0, SSM_D_INNER, chunk):
        zs_ref[:, lo:lo + chunk] = _silu(_dot(xn, w_ref[:, lo:lo + chunk])).astype(BF16)
    for lo in range(0, SSM_CONV_DIM, chunk):
        x = _dot(xn, w_ref[:, SSM_D_INNER + lo:SSM_D_INNER + lo + chunk])
        ext_ref[halo:halo + tm, lo:lo + chunk] = x
        acc = cb_ref[:, lo:lo + chunk] + cw_ref[SSM_CONV - 1:SSM_CONV, lo:lo + chunk] * x
        for k in range(SSM_CONV - 1):
            sh = SSM_CONV - 1 - k
            acc = acc + cw_ref[k:k + 1, lo:lo + chunk] * ext_ref[halo - sh:halo - sh + tm, lo:lo + chunk]
        ext_ref[0:halo, lo:lo + chunk] = x[tm - halo:tm]
        xbc_ref[:, lo:lo + chunk] = _silu(acc).astype(BF16)
    base = SSM_D_INNER + SSM_CONV_DIM
    dtl = _dot(xn, w_ref[:, base:base + SSM_DT_PAD]) + dtb_ref[...]
    dt_ref[...] = jnp.maximum(dtl, 0.0) + jnp.log(1.0 + jnp.exp(-jnp.abs(dtl)))


def _ssm_inproj(h, nw, w, cw, cb, dtb, S, tm=256):
    T = h.shape[0]
    row = lambda width: pl.BlockSpec((tm, width), lambda i: (i, 0))
    full = lambda a: pl.BlockSpec(a.shape, lambda i: (0,) * a.ndim)
    outs = ((SSM_D_INNER, BF16), (SSM_CONV_DIM, BF16), (SSM_DT_PAD, F32))
    return pl.pallas_call(
        functools.partial(_ssm_inproj_kernel, tiles_per_seq=S // tm),
        grid=(T // tm,),
        in_specs=[row(D_MODEL), full(nw), full(w), full(cw), full(cb), full(dtb)],
        out_specs=[row(wd) for wd, _ in outs],
        out_shape=[jax.ShapeDtypeStruct((T, wd), dt) for wd, dt in outs],
        scratch_shapes=[pltpu.VMEM((SSM_HALO + tm, SSM_CONV_DIM), F32)],
        compiler_params=_cparams("arbitrary"),
        name="ssm_inproj",
    )(h, nw, w, cw, cb, dtb)


def _ssd_kernel(xbc_ref, zs_ref, dt_ref, alog_ref, dsk_ref, nw_ref, tri_ref, y_ref, state_ref):
    Q, P, N, G, HPG = SSM_CHUNK, SSM_P, SSM_N, SSM_G, SSM_HPG
    gw = SSM_D_INNER // G

    @pl.when(pl.program_id(1) == 0)
    def _():
        state_ref[...] = jnp.zeros_like(state_ref)

    dt = dt_ref[...]
    a = dt * (-jnp.exp(alog_ref[...]))
    a1, a2, a3 = _split3(a)
    tri = tri_ref[...]
    cum = _dot(tri, a1) + _dot(tri, a2) + _dot(tri, a3)
    cum_t = cum.T
    dt_t = dt.T
    row_i = lax.broadcasted_iota(jnp.int32, (Q, Q), 0)
    col_i = lax.broadcasted_iota(jnp.int32, (Q, Q), 1)
    tril = row_i >= col_i
    eye = row_i == col_i

    b_off = SSM_D_INNER
    c_off = SSM_D_INNER + G * N
    for g in range(G):
        cg = xbc_ref[:, c_off + N * g:c_off + N * (g + 1)]
        bg = xbc_ref[:, b_off + N * g:b_off + N * (g + 1)]
        cb = _dot_nt(cg, bg)
        cg_f = cg.astype(F32)
        bg_t = bg.astype(F32).T
        st_g = state_ref[g]
        st_b = st_g.astype(BF16)
        lhs, wgt, rhs, keep = [], [], [], []
        for hh in range(HPG):
            h = g * HPG + hh
            cum_b = jnp.broadcast_to(cum[:, h:h + 1], (Q, Q))
            cum_row = cum_t[h:h + 1, :]
            dt_row = dt_t[h:h + 1, :]
            cum_last = cum_row[:, Q - 1:Q]
            mm = cb * jnp.exp(jnp.where(tril, cum_b - cum_row, -jnp.inf)) * dt_row
            mm = jnp.where(eye, mm + dsk_ref[:, h:h + 1], mm)
            lhs.append(jnp.concatenate([mm.astype(BF16), (jnp.exp(cum_b) * cg_f).astype(BF16)], axis=1))
            wgt.append((bg_t * (dt_row * jnp.exp(cum_last - cum_row))).astype(BF16))
            rhs.append(jnp.concatenate([xbc_ref[:, P * h:P * (h + 1)], st_b[:, P * hh:P * (hh + 1)]], axis=0))
            keep.append(jnp.broadcast_to(jnp.exp(cum_last), (1, P)))
        ys = [_dot(lhs[hh], rhs[hh]) for hh in range(HPG)]
        upd = [_dot(wgt[hh], rhs[hh][0:Q]) for hh in range(HPG)]
        state_ref[g] = st_g * jnp.concatenate(keep, axis=1) + jnp.concatenate(upd, axis=1)

        yg = jnp.concatenate(ys, axis=1) * zs_ref[:, gw * g:gw * (g + 1)].astype(F32)
        yg = yg * lax.rsqrt(jnp.mean(yg * yg, axis=-1, keepdims=True) + EPS)
        y_ref[:, gw * g:gw * (g + 1)] = (yg * nw_ref[:, gw * g:gw * (g + 1)]).astype(BF16)


def _ssd(xbc, zs, dt, alog, dsk, nw, tri, B, S):
    Q = SSM_CHUNK
    nch = S // Q
    row = lambda width: pl.BlockSpec((Q, width), lambda b, c: (b * nch + c, 0))
    full = lambda a: pl.BlockSpec(a.shape, lambda b, c: (0,) * a.ndim)
    return pl.pallas_call(
        _ssd_kernel,
        grid=(B, nch),
        in_specs=[row(SSM_CONV_DIM), row(SSM_D_INNER), row(SSM_DT_PAD), full(alog), full(dsk), full(nw),
                  full(tri)],
        out_specs=row(SSM_D_INNER),
        out_shape=jax.ShapeDtypeStruct((B * S, SSM_D_INNER), BF16),
        scratch_shapes=[pltpu.VMEM((SSM_G, SSM_N, SSM_HPG * SSM_P), F32)],
        compiler_params=_cparams("arbitrary", "arbitrary"),
        name="ssd_scan",
    )(xbc, zs, dt, alog, dsk, nw, tri)


def _block_diag_mean():
    i = np.arange(LANES)
    return jnp.asarray((i[:, None] // HEAD_DIM == i[None, :] // HEAD_DIM) / HEAD_DIM, BF16)


def _overlap_t(S):
    nc = (S - CMP_BLOCK) // CMP_STRIDE + 1
    ncp = S // CMP_STRIDE
    nblk = S // SEL_BLOCK
    starts = np.arange(ncp) * CMP_STRIDE
    js = np.arange(nblk)[:, None] * SEL_BLOCK
    ov = (starts[None, :] < js + SEL_BLOCK) & (starts[None, :] + CMP_BLOCK > js) & (np.arange(ncp)[None, :] < nc)
    out = np.zeros((LANES, ncp), np.float32)
    out[:nblk] = ov
    return jnp.asarray(out, BF16)


def _pad_cols(w, width):
    return jnp.pad(w, ((0, 0), (0, width - w.shape[1])))


def _nsa_layer(h, tabs, tabs_t, tabs_c, B, S, nw, w_in, q_norm, k_norm, cmp_pe, cmp_w1, cmp_b1, cmp_w2, w_out):
    G, dh, kvw = NSA_G, HEAD_DIM, NSA_KV_WIDTH
    ncp = S // CMP_STRIDE
    cut = lambda i: w_in[:, NSA_Q_WIDTH + i * kvw:NSA_Q_WIDTH + (i + 1) * kvw]
    wqt = w_in[:, :NSA_Q_WIDTH].T.astype(BF16)
    wk = jnp.concatenate([cut(0), cut(1), cut(2), cut(4)], axis=1).astype(BF16)
    wvt = jnp.concatenate([cut(3), cut(5)], axis=1).T.astype(BF16)
    wgt = _pad_cols(w_in[:, NSA_Q_WIDTH + 6 * kvw:], NSA_GATE_PAD).T.astype(BF16)
    qn = jnp.broadcast_to(q_norm[:, None], (dh, NSA_TQ))
    ksn = jnp.tile(k_norm[1], G)[None, :]
    kwn = jnp.tile(k_norm[2], G)[None, :]
    qt, kc_raw, vc_raw, ks, kw, vst, vwt, gt = _nsa_inproj(h, nw[None, :], wqt, wk, wvt, wgt, tabs, tabs_t,
                                                           qn, ksn, kwn, _block_diag_mean())

    def to_chunks(t):
        t = t.reshape(B, ncp, CMP_STRIDE, G, dh).transpose(0, 3, 1, 2, 4)
        return t.reshape(B, G, ncp, CMP_STRIDE * dh)

    half = CMP_STRIDE * dh
    pe = cmp_pe.reshape(2, 2, half)
    w1 = cmp_w1.reshape(2, 2, half, CMP_HIDDEN).astype(BF16)
    b1 = cmp_b1[:, None, :]
    w2k = _pad_cols(cmp_w2[0], LANES).astype(BF16)
    w2vt = cmp_w2[1].T.astype(BF16)
    knw = _pad_cols(k_norm[0][None, :], LANES)
    kc, vct = _nsa_compress(to_chunks(kc_raw), to_chunks(vc_raw), pe, w1, b1, w2k, w2vt, knw, tabs_c)

    o = _nsa_attn(qt, kc, vct, ks, vst, kw, vwt, gt, _overlap_t(S), B, S)
    return _proj_res(o, w_out.astype(BF16), h)


def _ssd_layer(h, B, S, nw, w_in, conv_w, conv_b, dt_bias, a_log, d_skip, norm_w, w_out):
    w = _pad_cols(w_in, SSM_IN_PAD).astype(BF16)
    pad1 = lambda v: _pad_cols(v[None, :], SSM_DT_PAD)
    zs, xbc, dt = _ssm_inproj(h, nw[None, :], w, conv_w, conv_b[None, :], pad1(dt_bias), S)
    tri = jnp.asarray(np.tril(np.ones((SSM_CHUNK, SSM_CHUNK), np.float32)), BF16)
    y = _ssd(xbc, zs, dt, pad1(a_log), pad1(d_skip), norm_w[None, :], tri, B, S)
    return _proj_res(y, w_out.astype(BF16), h)


def kernel(x, positions, mix_norm_w, ffn_norm_w, ffn_w_gate, ffn_w_up, ffn_w_down, nsa_w_in, nsa_q_norm, nsa_k_norm, nsa_cmp_pe, nsa_cmp_w1, nsa_cmp_b1, nsa_cmp_w2, nsa_w_out, ssm_w_in, ssm_conv_w, ssm_conv_b, ssm_dt_bias, ssm_a_log, ssm_d, ssm_norm_w, ssm_w_out):
    B, S, D = x.shape
    T = B * S
    h = x.reshape(T, D)
    ncp = S // CMP_STRIDE
    tabs = _rope_tables(positions.reshape(T, 1), 1024)
    tabs_t = _rope_tables_t(positions.reshape(1, T), 2048)
    pos_c = jnp.pad(positions[:, CMP_BLOCK - 1::CMP_STRIDE], ((0, 0), (0, 1)))[:, :ncp]
    tabs_c = _rope_tables(pos_c.reshape(B * ncp, 1), ncp)
    for i in range(DEPTH):
        j = i // 2
        if i % 2 == 0:
            h = _nsa_layer(h, tabs, tabs_t, tabs_c, B, S, mix_norm_w[i], nsa_w_in[j], nsa_q_norm[j], nsa_k_norm[j],
                           nsa_cmp_pe[j], nsa_cmp_w1[j], nsa_cmp_b1[j], nsa_cmp_w2[j], nsa_w_out[j])
        else:
            h = _ssd_layer(h, B, S, mix_norm_w[i], ssm_w_in[j], ssm_conv_w[j], ssm_conv_b[j], ssm_dt_bias[j],
                           ssm_a_log[j], ssm_d[j], ssm_norm_w[j], ssm_w_out[j])
        h = _ffn(h, ffn_norm_w[i][None, :], ffn_w_gate[i].astype(BF16), ffn_w_up[i].astype(BF16),
                 ffn_w_down[i].astype(BF16))
    return h.reshape(B, S, D)
```

```python
import functools
import math

import numpy as np
import jax
import jax.numpy as jnp
from jax import lax
from jax.experimental import pallas as pl
from jax.experimental.pallas import tpu as pltpu

F32 = jnp.float32
BF16 = jnp.bfloat16

D_MODEL = 1024
DEPTH = 4
EPS = 1e-6

NSA_HEADS = 16
NSA_G = 4
NSA_HPG = NSA_HEADS // NSA_G
HEAD_DIM = 64
CMP_BLOCK = 32
CMP_STRIDE = 16
CMP_HIDDEN = 256
SEL_BLOCK = 64
SEL_SHIFT = 6
SEL_TOPK = 8
WINDOW = 512
ROPE_THETA = 500000.0
ROPE_DIM = HEAD_DIM // 4
ROPE_HALF = ROPE_DIM // 2
NSA_Q_WIDTH = NSA_HEADS * HEAD_DIM
NSA_KV_WIDTH = NSA_G * HEAD_DIM
NSA_GATE_PAD = 128
NSA_TQ = 256
V_PAD = 16
QSCALE = HEAD_DIM ** -0.5 * math.log2(math.e)

SSM_D_INNER = 2 * D_MODEL
SSM_P = 64
SSM_HEADS = SSM_D_INNER // SSM_P
SSM_G = 4
SSM_HPG = SSM_HEADS // SSM_G
SSM_N = 128
SSM_CONV = 4
SSM_CHUNK = 128
SSM_CONV_DIM = SSM_D_INNER + 2 * SSM_G * SSM_N
SSM_DT_PAD = 128
SSM_HALO = 8
SSM_IN_PAD = SSM_D_INNER + SSM_CONV_DIM + SSM_DT_PAD

FFN_HIDDEN = -(-8 * D_MODEL // (3 * 256)) * 256

LANES = 128
VMEM_LIMIT_BYTES = 52 * 1024 * 1024

MASK_BIG = 1e30
M_INIT = -3e38

_NT = (((1,), (1,)), ((), ()))


def _cparams(*sem, flags=None):
    return pltpu.CompilerParams(dimension_semantics=sem, vmem_limit_bytes=VMEM_LIMIT_BYTES, flags=flags)


def _dot(a, b):
    return jnp.dot(a, b, preferred_element_type=F32)


def _dot_nt(a, b):
    return lax.dot_general(a, b, _NT, preferred_element_type=F32)


def _split3(x):
    a = x.astype(BF16)
    r = x - a.astype(F32)
    b = r.astype(BF16)
    c = (r - b.astype(F32)).astype(BF16)
    return a, b, c


def _rms_rows(x, w):
    return x * lax.rsqrt(jnp.mean(x * x, axis=-1, keepdims=True) + EPS) * w


def _silu(x):
    h = 0.5 * x
    return h + h * jnp.tanh(h)


def _rope_table_kernel(pos_ref, c_ref, sa_ref, sb_ref):
    pos = pos_ref[...].astype(F32)
    lane = lax.broadcasted_iota(jnp.int32, (1, LANES), 1)
    d = lane & (HEAD_DIM - 1)
    f = d & (ROPE_HALF - 1)
    inv = jnp.zeros((1, LANES), F32)
    for i in range(ROPE_HALF):
        inv = jnp.where(f == i, float(np.power(np.float32(ROPE_THETA), np.float32(-i / ROPE_HALF))), inv)
    ang = pos * inv
    cos, sin = jnp.cos(ang), jnp.sin(ang)
    c_ref[...] = jnp.where(d < ROPE_DIM, cos, 1.0)
    sa_ref[...] = jnp.where(d < ROPE_HALF, -sin, 0.0)
    sb_ref[...] = jnp.where((d >= ROPE_HALF) & (d < ROPE_DIM), sin, 0.0)


def _rope_tables(pos_col, tm):
    n = pos_col.shape[0]
    out = jax.ShapeDtypeStruct((n, LANES), F32)
    spec = pl.BlockSpec((tm, LANES), lambda i: (i, 0))
    return pl.pallas_call(
        _rope_table_kernel,
        grid=(n // tm,),
        in_specs=[pl.BlockSpec((tm, 1), lambda i: (i, 0))],
        out_specs=[spec, spec, spec],
        out_shape=[out, out, out],
        compiler_params=_cparams("parallel"),
        name="rope_tables",
    )(pos_col)


def _rope_table_t_kernel(pos_ref, c_ref, s_ref):
    pos = pos_ref[...].astype(F32)
    f = lax.broadcasted_iota(jnp.int32, (ROPE_HALF, 1), 0)
    inv = jnp.zeros((ROPE_HALF, 1), F32)
    for i in range(ROPE_HALF):
        inv = jnp.where(f == i, float(np.power(np.float32(ROPE_THETA), np.float32(-i / ROPE_HALF))), inv)
    ang = inv * pos
    c_ref[...] = jnp.cos(ang)
    s_ref[...] = jnp.sin(ang)


def _rope_tables_t(pos_row, tm):
    n = pos_row.shape[1]
    out = jax.ShapeDtypeStruct((ROPE_HALF, n), F32)
    spec = pl.BlockSpec((ROPE_HALF, tm), lambda i: (0, i))
    return pl.pallas_call(
        _rope_table_t_kernel,
        grid=(n // tm,),
        in_specs=[pl.BlockSpec((1, tm), lambda i: (0, i))],
        out_specs=[spec, spec],
        out_shape=[out, out],
        compiler_params=_cparams("parallel"),
        name="rope_tables_t",
    )(pos_row)


def _rope_lanes(x, c, sa, sb):
    return x * c + pltpu.roll(x, LANES - ROPE_HALF, 1) * sa + pltpu.roll(x, ROPE_HALF, 1) * sb


def _head_norm_rope(y, w, bd, c, sa, sb, scale):
    outs = []
    for j in range(y.shape[1] // LANES):
        yc = y[:, LANES * j:LANES * (j + 1)]
        sq = yc * yc
        hi = sq.astype(BF16)
        lo = (sq - hi.astype(F32)).astype(BF16)
        ms = _dot(hi, bd) + _dot(lo, bd)
        yn = yc * lax.rsqrt(ms + EPS) * w[:, LANES * j:LANES * (j + 1)]
        outs.append(_rope_lanes(yn, c, sa, sb) * scale)
    return jnp.concatenate(outs, axis=1)


def _nsa_inproj_kernel(x_ref, nw_ref, wqt_ref, wk_ref, wvt_ref, wgt_ref, c_ref, sa_ref, sb_ref, ct_ref, st_ref,
                       qn_ref, ksn_ref, kwn_ref, bd_ref,
                       qt_ref, kc_ref, vc_ref, ks_ref, kw_ref, vst_ref, vwt_ref, gt_ref):
    xn = _rms_rows(x_ref[...], nw_ref[...]).astype(BF16)
    c, sa, sb, bd = c_ref[...], sa_ref[...], sb_ref[...], bd_ref[...]
    kvw, dh, tq = NSA_KV_WIDTH, HEAD_DIM, NSA_TQ

    def proj(lo):
        return _dot(xn, wk_ref[:, lo:lo + kvw])

    kc_ref[...] = proj(0)
    vc_ref[...] = proj(kvw)
    ks_ref[...] = _head_norm_rope(proj(2 * kvw), ksn_ref[...], bd, c, sa, sb, 1.0).astype(BF16)
    kw_ref[...] = _head_norm_rope(proj(3 * kvw), kwn_ref[...], bd, c, sa, sb, 1.0).astype(BF16)

    qn = qn_ref[...]
    for ch in range(x_ref.shape[0] // tq):
        xc = xn[ch * tq:(ch + 1) * tq]
        cos, sin = ct_ref[:, ch * tq:(ch + 1) * tq], st_ref[:, ch * tq:(ch + 1) * tq]
        yt = _dot_nt(wqt_ref[...], xc)
        for hd in range(NSA_HEADS):
            yh = yt[dh * hd:dh * (hd + 1)]
            yn = yh * lax.rsqrt(jnp.mean(yh * yh, axis=0, keepdims=True) + EPS) * qn
            x1, x2 = yn[0:ROPE_HALF], yn[ROPE_HALF:ROPE_DIM]
            rot = jnp.concatenate([x1 * cos - x2 * sin, x2 * cos + x1 * sin, yn[ROPE_DIM:]], axis=0)
            qt_ref[ch, dh * hd:dh * (hd + 1), :] = (rot * QSCALE).astype(BF16)
        vt = _dot_nt(wvt_ref[...], xc)
        vst_ref[ch] = vt[0:kvw].astype(BF16)
        vwt_ref[ch] = vt[kvw:2 * kvw].astype(BF16)
        gt_ref[ch] = 1.0 / (1.0 + jnp.exp(-_dot_nt(wgt_ref[...], xc)))


def _nsa_inproj(h, nw, wqt, wk, wvt, wgt, tabs, tabs_t, qn, ksn, kwn, bd, tm=512):
    T = h.shape[0]
    tq, kvw = NSA_TQ, NSA_KV_WIDTH
    row = lambda width: pl.BlockSpec((tm, width), lambda i: (i, 0))
    full = lambda a: pl.BlockSpec(a.shape, lambda i: (0,) * a.ndim)
    colt = pl.BlockSpec((ROPE_HALF, tm), lambda i: (0, i))
    tile = lambda ch: pl.BlockSpec((tm // tq, ch, tq), lambda i: (i, 0, 0))
    tshape = lambda ch, dt: jax.ShapeDtypeStruct((T // tq, ch, tq), dt)
    c, sa, sb = tabs
    ct, st = tabs_t
    return pl.pallas_call(
        _nsa_inproj_kernel,
        grid=(T // tm,),
        in_specs=[row(D_MODEL), full(nw), full(wqt), full(wk), full(wvt), full(wgt),
                  row(LANES), row(LANES), row(LANES), colt, colt,
                  full(qn), full(ksn), full(kwn), full(bd)],
        out_specs=[tile(NSA_Q_WIDTH), row(kvw), row(kvw), row(kvw), row(kvw), tile(kvw), tile(kvw),
                   tile(NSA_GATE_PAD)],
        out_shape=[tshape(NSA_Q_WIDTH, BF16), jax.ShapeDtypeStruct((T, kvw), F32),
                   jax.ShapeDtypeStruct((T, kvw), F32), jax.ShapeDtypeStruct((T, kvw), BF16),
                   jax.ShapeDtypeStruct((T, kvw), BF16), tshape(kvw, BF16), tshape(kvw, BF16),
                   tshape(NSA_GATE_PAD, F32)],
        compiler_params=_cparams("parallel"),
        name="nsa_inproj",
    )(h, nw, wqt, wk, wvt, wgt, c, sa, sb, ct, st, qn, ksn, kwn, bd)


def _nsa_compress_kernel(xk_ref, xv_ref, pe_ref, w1_ref, b1_ref, w2k_ref, w2vt_ref, knw_ref,
                         c_ref, sa_ref, sb_ref, kc_ref, vct_ref):
    ncp = xk_ref.shape[2]

    def hidden(x, which):
        lo = _dot((x + pe_ref[which, 0:1, :]).astype(BF16), w1_ref[which, 0])
        hi = _dot((x + pe_ref[which, 1:2, :]).astype(BF16), w1_ref[which, 1])
        return _silu(lo + pltpu.roll(hi, ncp - 1, 0) + b1_ref[which]).astype(BF16)

    for g in range(NSA_G):
        kc = _dot(hidden(xk_ref[0, g], 0), w2k_ref[...])
        ms = jnp.sum(kc * kc, axis=-1, keepdims=True) * (1.0 / HEAD_DIM)
        kn = kc * lax.rsqrt(ms + EPS) * knw_ref[...]
        kn = _rope_lanes(kn, c_ref[...], sa_ref[...], sb_ref[...])
        kc_ref[0, g] = kn[:, :HEAD_DIM].astype(BF16)
        vct_ref[0, g] = _dot_nt(w2vt_ref[...], hidden(xv_ref[0, g], 1)).astype(BF16)


def _nsa_compress(xk, xv, pe, w1, b1, w2k, w2vt, knw, tabs):
    B, G, ncp, width = xk.shape
    c, sa, sb = tabs
    xspec = pl.BlockSpec((1, G, ncp, width), lambda b: (b, 0, 0, 0))
    tspec = pl.BlockSpec((ncp, LANES), lambda b: (b, 0))
    full = lambda a: pl.BlockSpec(a.shape, lambda b: (0,) * a.ndim)
    return pl.pallas_call(
        _nsa_compress_kernel,
        grid=(B,),
        in_specs=[xspec, xspec, full(pe), full(w1), full(b1), full(w2k), full(w2vt), full(knw),
                  tspec, tspec, tspec],
        out_specs=[pl.BlockSpec((1, G, ncp, HEAD_DIM), lambda b: (b, 0, 0, 0)),
                   pl.BlockSpec((1, G, HEAD_DIM, ncp), lambda b: (b, 0, 0, 0))],
        out_shape=[jax.ShapeDtypeStruct((B, G, ncp, HEAD_DIM), BF16),
                   jax.ShapeDtypeStruct((B, G, HEAD_DIM, ncp), BF16)],
        compiler_params=_cparams("parallel"),
        name="nsa_compress",
    )(xk, xv, pe, w1, b1, w2k, w2vt, knw, c, sa, sb)


def _flash_steps(qats, k_tiles, vt_tiles, states, mask=None):
    n = len(qats)
    ahead = 2
    scores = [_dot(k_tiles[i], qats[i]) for i in range(min(ahead, n))]
    out = []
    for i in range(n):
        m, acc = states[i]
        s = scores[i]
        if mask is not None:
            s = jnp.where(mask, s, -MASK_BIG)
        m_new = jnp.maximum(m, jnp.max(s, axis=0, keepdims=True))
        p = jnp.exp2(s - m_new).astype(BF16)
        out.append((m_new, jnp.exp2(m - m_new) * acc + _dot(vt_tiles[i], p)))
        if i + ahead < n:
            scores.append(_dot(k_tiles[i + ahead], qats[i + ahead]))
    return out


def _nsa_attn_kernel(qt_ref, kc_ref, vct_ref, ks_ref, vst_ref, kw_ref, vwt_ref, gt_ref, ovl_ref, o_ref,
                     ksa_ref, kwa_ref, qat_ref, part_ref):
    S = ks_ref.shape[0]
    ncp = kc_ref.shape[2]
    nblk = S // SEL_BLOCK
    G, HP, dh, tq = NSA_G, NSA_HPG, HEAD_DIM, NSA_TQ
    cols = HP * tq
    qi = pl.program_id(1)
    q0 = qi * tq

    @pl.when(qi == 0)
    def _():
        rblk = lax.broadcasted_iota(jnp.int32, (S, dh), 0) >> SEL_SHIFT
        lane = lax.broadcasted_iota(jnp.int32, (S, dh), 1)
        onehot = jnp.where(rblk == lane, 1.0, 0.0).astype(BF16)
        zeros = jnp.zeros((S, dh), BF16)
        for g in range(G):
            ksa_ref[g, :, 0:dh] = ks_ref[:, dh * g:dh * (g + 1)]
            ksa_ref[g, :, dh:2 * dh] = onehot
            kwa_ref[g, :, 0:dh] = kw_ref[:, dh * g:dh * (g + 1)]
            kwa_ref[g, :, dh:2 * dh] = zeros

    t_cols = q0 + (lax.broadcasted_iota(jnp.int32, (1, cols), 1) & (tq - 1))
    t_q = t_cols[:, 0:tq]
    k_loc = lax.broadcasted_iota(jnp.int32, (tq, 1), 0)

    cmp_end = lax.broadcasted_iota(jnp.int32, (ncp, 1), 0) * CMP_STRIDE + (CMP_BLOCK - 1)
    cmask = cmp_end <= t_cols
    jb = lax.broadcasted_iota(jnp.int32, (nblk, tq), 0)
    tblk = t_q >> SEL_SHIFT
    forced = (jb == 0) | (jb == tblk) | (jb == tblk - 1)
    gt = gt_ref[0]

    def gate_row(g, branch):
        return jnp.concatenate([gt[3 * (g * HP + h) + branch:3 * (g * HP + h) + branch + 1] for h in range(HP)],
                               axis=1)

    init = (jnp.full((1, cols), M_INIT, F32), jnp.zeros((dh + V_PAD, cols), F32))
    ones_rows = jnp.where(lax.broadcasted_iota(jnp.int32, (V_PAD, tq), 0) == 0, 1.0, 0.0).astype(BF16)

    def v_aug(v_ref, kt, g):
        return jnp.concatenate([v_ref[kt, dh * g:dh * (g + 1), :], ones_rows], axis=0)

    def normalised(acc):
        return acc[0:dh] * (1.0 / acc[dh:dh + 1])

    causal = (q0 + k_loc) <= t_cols
    n_back = (WINDOW + tq - 1) // tq
    back = []
    for dk in range(1, n_back + 1):
        kt = qi - dk
        far = jnp.where(kt < 0, 2 * WINDOW + S, 0)
        back.append((jnp.maximum(kt, 0), (t_cols - (kt * tq + k_loc) + far) < WINDOW))

    grp = range(G)
    for g in grp:
        qat_ref[g, dh + nblk:2 * dh, :] = jnp.zeros((dh - nblk, cols), BF16)
        for h in range(HP):
            hd = g * HP + h
            qat_ref[g, 0:dh, h * tq:(h + 1) * tq] = qt_ref[0, dh * hd:dh * (hd + 1), :]

    sc = [_dot(kc_ref[0, g], qat_ref[g, 0:dh, :]) for g in grp]
    pc = []
    for g in grp:
        s = jnp.where(cmask, sc[g], -MASK_BIG)
        m = jnp.max(s, axis=0, keepdims=True)
        p = jnp.where(cmask, jnp.exp2(s - m), 0.0)
        l = jnp.sum(p, axis=0, keepdims=True)
        pc.append(p * jnp.where(l > 0.0, 1.0 / l, 0.0))
    oc = [_dot(vct_ref[0, g], pc[g].astype(BF16)) for g in grp]
    ovl = ovl_ref[...]
    imp = []
    for g in grp:
        psum = pc[g][:, 0:tq]
        for h in range(1, HP):
            psum = psum + pc[g][:, h * tq:(h + 1) * tq]
        p1, p2, p3 = _split3(psum)
        imp.append((_dot(ovl, p1) + _dot(ovl, p2) + _dot(ovl, p3))[0:nblk])
    for g in grp:
        v = jnp.where(forced, MASK_BIG, jnp.where(jb > tblk, -MASK_BIG, imp[g]))
        cnt = jnp.zeros((nblk, tq), jnp.int32)
        for j in range(nblk):
            rj = v[j:j + 1, :]
            beats = (rj > v) | ((rj == v) & (jb > j))
            cnt = cnt + jnp.where(beats, 1, 0)
        selneg = jnp.where(cnt < SEL_TOPK, 0.0, -MASK_BIG).astype(BF16)
        for h in range(HP):
            qat_ref[g, dh:dh + nblk, h * tq:(h + 1) * tq] = selneg
    qats = [qat_ref[g] for g in grp]

    states = _flash_steps(qats, [kwa_ref[g, pl.ds(q0, tq), :] for g in grp],
                          [v_aug(vwt_ref, qi, g) for g in grp], [init] * G, causal)
    for kt, inside in back:
        k0 = pl.multiple_of(kt * tq, tq)
        states = _flash_steps(qats, [kwa_ref[g, pl.ds(k0, tq), :] for g in grp],
                              [v_aug(vwt_ref, kt, g) for g in grp], states, inside)
    for g in grp:
        part_ref[g] = gate_row(g, 0) * oc[g] + gate_row(g, 2) * normalised(states[g][1])

    sel_state = _flash_steps(qats, [ksa_ref[g, pl.ds(q0, tq), :] for g in grp],
                             [v_aug(vst_ref, qi, g) for g in grp], [init] * G, causal)

    def sel_body(kt, states):
        k0 = pl.multiple_of(kt * tq, tq)
        return tuple(_flash_steps([qat_ref[g] for g in grp], [ksa_ref[g, pl.ds(k0, tq), :] for g in grp],
                                  [v_aug(vst_ref, kt, g) for g in grp], states))

    sel_state = lax.fori_loop(0, qi, sel_body, tuple(sel_state))

    for g in range(G):
        og = part_ref[g] + gate_row(g, 1) * normalised(sel_state[g][1])
        og_t = jnp.concatenate([og[:, h * tq:(h + 1) * tq] for h in range(HP)], axis=0)
        o_ref[:, HP * dh * g:HP * dh * (g + 1)] = og_t.T.astype(BF16)


def _nsa_attn(qt, kc, vct, ks, vst, kw, vwt, gt, ovl, B, S):
    T = B * S
    tq = NSA_TQ
    nq = S // tq
    G, dh = NSA_G, HEAD_DIM
    ncp = kc.shape[2]
    qspec = pl.BlockSpec((1, NSA_Q_WIDTH, tq), lambda b, i: (b * nq + i, 0, 0))
    gspec = pl.BlockSpec((1, NSA_GATE_PAD, tq), lambda b, i: (b * nq + i, 0, 0))
    kcspec = pl.BlockSpec((1, G, ncp, dh), lambda b, i: (b, 0, 0, 0))
    vcspec = pl.BlockSpec((1, G, dh, ncp), lambda b, i: (b, 0, 0, 0))
    kspec = pl.BlockSpec((S, NSA_KV_WIDTH), lambda b, i: (b, 0))
    vspec = pl.BlockSpec((nq, NSA_KV_WIDTH, tq), lambda b, i: (b, 0, 0))
    ovspec = pl.BlockSpec(ovl.shape, lambda b, i: (0, 0))
    return pl.pallas_call(
        _nsa_attn_kernel,
        grid=(B, nq),
        in_specs=[qspec, kcspec, vcspec, kspec, vspec, kspec, vspec, gspec, ovspec],
        out_specs=pl.BlockSpec((tq, NSA_Q_WIDTH), lambda b, i: (b * nq + i, 0)),
        out_shape=jax.ShapeDtypeStruct((T, NSA_Q_WIDTH), BF16),
        scratch_shapes=[pltpu.VMEM((G, S, 2 * dh), BF16), pltpu.VMEM((G, S, 2 * dh), BF16),
                        pltpu.VMEM((G, 2 * dh, NSA_HPG * tq), BF16),
                        pltpu.VMEM((G, dh, NSA_HPG * tq), F32)],
        compiler_params=_cparams("arbitrary", "arbitrary"),
        name="nsa_attn",
    )(qt, kc, vct, ks, vst, kw, vwt, gt, ovl)


def _ffn_kernel(x_ref, y_ref, wo_ref, nw_ref, wg_ref, wu_ref, wd_ref, o_ref, xn_ref, acc_ref):
    k = pl.program_id(1)

    @pl.when(k == 0)
    def _():
        h1 = x_ref[...] + _dot(y_ref[...], wo_ref[...])
        xn_ref[...] = _rms_rows(h1, nw_ref[...]).astype(BF16)
        acc_ref[...] = h1

    xn = xn_ref[...]
    a = (_silu(_dot(xn, wg_ref[...])) * _dot(xn, wu_ref[...])).astype(BF16)
    acc_ref[...] += _dot(a, wd_ref[...])

    @pl.when(k == pl.num_programs(1) - 1)
    def _():
        o_ref[...] = acc_ref[...]


def _ffn(h, y, wo, nw, wg, wu, wd, tm=512, th=1408):
    T = h.shape[0]
    H = wg.shape[1]
    K = y.shape[1]
    return pl.pallas_call(
        _ffn_kernel,
        grid=(T // tm, H // th),
        in_specs=[pl.BlockSpec((tm, D_MODEL), lambda i, k: (i, 0)),
                  pl.BlockSpec((tm, K), lambda i, k: (i, 0)),
                  pl.BlockSpec((K, D_MODEL), lambda i, k: (0, 0)),
                  pl.BlockSpec((1, D_MODEL), lambda i, k: (0, 0)),
                  pl.BlockSpec((D_MODEL, th), lambda i, k: (0, k)),
                  pl.BlockSpec((D_MODEL, th), lambda i, k: (0, k)),
                  pl.BlockSpec((th, D_MODEL), lambda i, k: (k, 0))],
        out_specs=pl.BlockSpec((tm, D_MODEL), lambda i, k: (i, 0)),
        out_shape=jax.ShapeDtypeStruct((T, D_MODEL), F32),
        scratch_shapes=[pltpu.VMEM((tm, D_MODEL), BF16), pltpu.VMEM((tm, D_MODEL), F32)],
        compiler_params=_cparams("parallel", "arbitrary"),
        name="ffn",
    )(h, y, wo, nw, wg, wu, wd)


def _ssm_inproj_kernel(x_ref, nw_ref, w_ref, cw_ref, cb_ref, dtb_ref, zs_ref, xbc_ref, dt_ref, ext_ref,
                       *, tiles_per_seq):
    tm = x_ref.shape[0]
    halo = SSM_HALO
    chunk = 512
    xn = _rms_rows(x_ref[...], nw_ref[...]).astype(BF16)

    @pl.when(pl.program_id(0) % tiles_per_seq == 0)
    def _():
        ext_ref[...] = jnp.zeros_like(ext_ref)

    for lo in range(0, SSM_D_INNER, chunk):
        zs_ref[:, lo:lo + chunk] = _silu(_dot(xn, w_ref[:, lo:lo + chunk])).astype(BF16)
    row = lax.broadcasted_iota(jnp.int32, (halo, chunk), 0)
    for lo in range(0, SSM_CONV_DIM, chunk):
        x = _dot(xn, w_ref[:, SSM_D_INNER + lo:SSM_D_INNER + lo + chunk])
        prev = ext_ref[:, lo:lo + chunk]
        acc = cb_ref[:, lo:lo + chunk] + cw_ref[SSM_CONV - 1:SSM_CONV, lo:lo + chunk] * x
        for k in range(SSM_CONV - 1):
            sh = SSM_CONV - 1 - k
            r = pltpu.roll(x, sh, 0)
            top = jnp.where(row < sh, pltpu.roll(prev, sh, 0), r[0:halo])
            acc = acc + cw_ref[k:k + 1, lo:lo + chunk] * jnp.concatenate([top, r[halo:]], axis=0)
        ext_ref[:, lo:lo + chunk] = x[tm - halo:tm]
        xbc_ref[:, lo:lo + chunk] = _silu(acc).astype(BF16)
    base = SSM_D_INNER + SSM_CONV_DIM
    dtl = _dot(xn, w_ref[:, base:base + SSM_DT_PAD]) + dtb_ref[...]
    dt_ref[...] = jnp.maximum(dtl, 0.0) + jnp.log(1.0 + jnp.exp(-jnp.abs(dtl)))


def _ssm_inproj(h, nw, w, cw, cb, dtb, S, tm=256):
    T = h.shape[0]
    row = lambda width: pl.BlockSpec((tm, width), lambda i: (i, 0))
    full = lambda a: pl.BlockSpec(a.shape, lambda i: (0,) * a.ndim)
    outs = ((SSM_D_INNER, BF16), (SSM_CONV_DIM, BF16), (SSM_DT_PAD, F32))
    return pl.pallas_call(
        functools.partial(_ssm_inproj_kernel, tiles_per_seq=S // tm),
        grid=(T // tm,),
        in_specs=[row(D_MODEL), full(nw), full(w), full(cw), full(cb), full(dtb)],
        out_specs=[row(wd) for wd, _ in outs],
        out_shape=[jax.ShapeDtypeStruct((T, wd), dt) for wd, dt in outs],
        scratch_shapes=[pltpu.VMEM((SSM_HALO, SSM_CONV_DIM), F32)],
        compiler_params=_cparams("arbitrary"),
        name="ssm_inproj",
    )(h, nw, w, cw, cb, dtb)


def _ssd_kernel(xbc_ref, zs_ref, dt_ref, alog_ref, dsk_ref, nw_ref, tri_ref, y_ref, state_ref):
    Q, P, N, G, HPG = SSM_CHUNK, SSM_P, SSM_N, SSM_G, SSM_HPG
    gw = SSM_D_INNER // G

    @pl.when(pl.program_id(1) == 0)
    def _():
        state_ref[...] = jnp.zeros_like(state_ref)

    dt = dt_ref[...]
    a = dt * (-jnp.exp(alog_ref[...]))
    a1, a2, a3 = _split3(a)
    tri = tri_ref[...]
    cum = _dot(tri, a1) + _dot(tri, a2) + _dot(tri, a3)
    cum_t = cum.T
    dt_t = dt.T
    row_i = lax.broadcasted_iota(jnp.int32, (Q, Q), 0)
    col_i = lax.broadcasted_iota(jnp.int32, (Q, Q), 1)
    tril = row_i >= col_i
    eye = row_i == col_i

    b_off = SSM_D_INNER
    c_off = SSM_D_INNER + G * N
    for g in range(G):
        cg = xbc_ref[:, c_off + N * g:c_off + N * (g + 1)]
        bg = xbc_ref[:, b_off + N * g:b_off + N * (g + 1)]
        cb = _dot_nt(cg, bg)
        cg_f = cg.astype(F32)
        bg_t = bg.astype(F32).T
        st_g = state_ref[g]
        st_b = st_g.astype(BF16)
        lhs, wgt, rhs, keep = [], [], [], []
        for hh in range(HPG):
            h = g * HPG + hh
            cum_b = jnp.broadcast_to(cum[:, h:h + 1], (Q, Q))
            cum_row = cum_t[h:h + 1, :]
            dt_row = dt_t[h:h + 1, :]
            cum_last = cum_row[:, Q - 1:Q]
            mm = cb * jnp.exp(jnp.where(tril, cum_b - cum_row, -jnp.inf)) * dt_row
            mm = jnp.where(eye, mm + dsk_ref[:, h:h + 1], mm)
            lhs.append(jnp.concatenate([mm.astype(BF16), (jnp.exp(cum_b) * cg_f).astype(BF16)], axis=1))
            wgt.append((bg_t * (dt_row * jnp.exp(cum_last - cum_row))).astype(BF16))
            rhs.append(jnp.concatenate([xbc_ref[:, P * h:P * (h + 1)], st_b[:, P * hh:P * (hh + 1)]], axis=0))
            keep.append(jnp.broadcast_to(jnp.exp(cum_last), (1, P)))
        ys = [_dot(lhs[hh], rhs[hh]) for hh in range(HPG)]
        upd = [_dot(wgt[hh], rhs[hh][0:Q]) for hh in range(HPG)]
        state_ref[g] = st_g * jnp.concatenate(keep, axis=1) + jnp.concatenate(upd, axis=1)

        yg = jnp.concatenate(ys, axis=1) * zs_ref[:, gw * g:gw * (g + 1)].astype(F32)
        yg = yg * lax.rsqrt(jnp.mean(yg * yg, axis=-1, keepdims=True) + EPS)
        y_ref[:, gw * g:gw * (g + 1)] = (yg * nw_ref[:, gw * g:gw * (g + 1)]).astype(BF16)


def _ssd(xbc, zs, dt, alog, dsk, nw, tri, B, S):
    Q = SSM_CHUNK
    nch = S // Q
    row = lambda width: pl.BlockSpec((Q, width), lambda b, c: (b * nch + c, 0))
    full = lambda a: pl.BlockSpec(a.shape, lambda b, c: (0,) * a.ndim)
    return pl.pallas_call(
        _ssd_kernel,
        grid=(B, nch),
        in_specs=[row(SSM_CONV_DIM), row(SSM_D_INNER), row(SSM_DT_PAD), full(alog), full(dsk), full(nw),
                  full(tri)],
        out_specs=row(SSM_D_INNER),
        out_shape=jax.ShapeDtypeStruct((B * S, SSM_D_INNER), BF16),
        scratch_shapes=[pltpu.VMEM((SSM_G, SSM_N, SSM_HPG * SSM_P), F32)],
        compiler_params=_cparams("arbitrary", "arbitrary"),
        name="ssd_scan",
    )(xbc, zs, dt, alog, dsk, nw, tri)


def _block_diag_mean():
    i = np.arange(LANES)
    return jnp.asarray((i[:, None] // HEAD_DIM == i[None, :] // HEAD_DIM) / HEAD_DIM, BF16)


def _overlap_t(S):
    nc = (S - CMP_BLOCK) // CMP_STRIDE + 1
    ncp = S // CMP_STRIDE
    nblk = S // SEL_BLOCK
    starts = np.arange(ncp) * CMP_STRIDE
    js = np.arange(nblk)[:, None] * SEL_BLOCK
    ov = (starts[None, :] < js + SEL_BLOCK) & (starts[None, :] + CMP_BLOCK > js) & (np.arange(ncp)[None, :] < nc)
    out = np.zeros((LANES, ncp), np.float32)
    out[:nblk] = ov
    return jnp.asarray(out, BF16)


def _pad_cols(w, width):
    return jnp.pad(w, ((0, 0), (0, width - w.shape[1])))


def _nsa_layer(h, tabs, tabs_t, tabs_c, B, S, nw, w_in, q_norm, k_norm, cmp_pe, cmp_w1, cmp_b1, cmp_w2, w_out):
    G, dh, kvw = NSA_G, HEAD_DIM, NSA_KV_WIDTH
    ncp = S // CMP_STRIDE
    cut = lambda i: w_in[:, NSA_Q_WIDTH + i * kvw:NSA_Q_WIDTH + (i + 1) * kvw]
    wqt = w_in[:, :NSA_Q_WIDTH].T.astype(BF16)
    wk = jnp.concatenate([cut(0), cut(1), cut(2), cut(4)], axis=1).astype(BF16)
    wvt = jnp.concatenate([cut(3), cut(5)], axis=1).T.astype(BF16)
    wgt = _pad_cols(w_in[:, NSA_Q_WIDTH + 6 * kvw:], NSA_GATE_PAD).T.astype(BF16)
    qn = jnp.broadcast_to(q_norm[:, None], (dh, NSA_TQ))
    ksn = jnp.tile(k_norm[1], G)[None, :]
    kwn = jnp.tile(k_norm[2], G)[None, :]
    qt, kc_raw, vc_raw, ks, kw, vst, vwt, gt = _nsa_inproj(h, nw[None, :], wqt, wk, wvt, wgt, tabs, tabs_t,
                                                           qn, ksn, kwn, _block_diag_mean())

    def to_chunks(t):
        t = t.reshape(B, ncp, CMP_STRIDE, G, dh).transpose(0, 3, 1, 2, 4)
        return t.reshape(B, G, ncp, CMP_STRIDE * dh)

    half = CMP_STRIDE * dh
    pe = cmp_pe.reshape(2, 2, half)
    w1 = cmp_w1.reshape(2, 2, half, CMP_HIDDEN).astype(BF16)
    b1 = cmp_b1[:, None, :]
    w2k = _pad_cols(cmp_w2[0], LANES).astype(BF16)
    w2vt = cmp_w2[1].T.astype(BF16)
    knw = _pad_cols(k_norm[0][None, :], LANES)
    kc, vct = _nsa_compress(to_chunks(kc_raw), to_chunks(vc_raw), pe, w1, b1, w2k, w2vt, knw, tabs_c)

    o = _nsa_attn(qt, kc, vct, ks, vst, kw, vwt, gt, _overlap_t(S), B, S)
    return o, w_out.astype(BF16)


def _ssd_layer(h, B, S, nw, w_in, conv_w, conv_b, dt_bias, a_log, d_skip, norm_w, w_out):
    w = _pad_cols(w_in, SSM_IN_PAD).astype(BF16)
    pad1 = lambda v: _pad_cols(v[None, :], SSM_DT_PAD)
    zs, xbc, dt = _ssm_inproj(h, nw[None, :], w, conv_w, conv_b[None, :], pad1(dt_bias), S)
    tri = jnp.asarray(np.tril(np.ones((SSM_CHUNK, SSM_CHUNK), np.float32)), BF16)
    y = _ssd(xbc, zs, dt, pad1(a_log), pad1(d_skip), norm_w[None, :], tri, B, S)
    return y, w_out.astype(BF16)


def kernel(x, positions, mix_norm_w, ffn_norm_w, ffn_w_gate, ffn_w_up, ffn_w_down, nsa_w_in, nsa_q_norm, nsa_k_norm, nsa_cmp_pe, nsa_cmp_w1, nsa_cmp_b1, nsa_cmp_w2, nsa_w_out, ssm_w_in, ssm_conv_w, ssm_conv_b, ssm_dt_bias, ssm_a_log, ssm_d, ssm_norm_w, ssm_w_out):
    B, S, D = x.shape
    T = B * S
    h = x.reshape(T, D)
    ncp = S // CMP_STRIDE
    tabs = _rope_tables(positions.reshape(T, 1), 1024)
    tabs_t = _rope_tables_t(positions.reshape(1, T), 2048)
    pos_c = jnp.pad(positions[:, CMP_BLOCK - 1::CMP_STRIDE], ((0, 0), (0, 1)))[:, :ncp]
    tabs_c = _rope_tables(pos_c.reshape(B * ncp, 1), ncp)
    for i in range(DEPTH):
        j = i // 2
        if i % 2 == 0:
            y, wo = _nsa_layer(h, tabs, tabs_t, tabs_c, B, S, mix_norm_w[i], nsa_w_in[j], nsa_q_norm[j],
                               nsa_k_norm[j], nsa_cmp_pe[j], nsa_cmp_w1[j], nsa_cmp_b1[j], nsa_cmp_w2[j],
                               nsa_w_out[j])
        else:
            y, wo = _ssd_layer(h, B, S, mix_norm_w[i], ssm_w_in[j], ssm_conv_w[j], ssm_conv_b[j], ssm_dt_bias[j],
                               ssm_a_log[j], ssm_d[j], ssm_norm_w[j], ssm_w_out[j])
        h = _ffn(h, y, wo, ffn_norm_w[i][None, :], ffn_w_gate[i].astype(BF16), ffn_w_up[i].astype(BF16),
                 ffn_w_down[i].astype(BF16))
    return h.reshape(B, S, D)
```

```python
import functools
import math

import numpy as np
import jax
import jax.numpy as jnp
from jax import lax
from jax.experimental import pallas as pl
from jax.experimental.pallas import tpu as pltpu

F32 = jnp.float32
BF16 = jnp.bfloat16

D_MODEL = 1024
DEPTH = 4
EPS = 1e-6

NSA_HEADS = 16
NSA_G = 4
NSA_HPG = NSA_HEADS // NSA_G
HEAD_DIM = 64
CMP_BLOCK = 32
CMP_STRIDE = 16
CMP_HIDDEN = 256
SEL_BLOCK = 64
SEL_SHIFT = 6
SEL_TOPK = 8
WINDOW = 512
ROPE_THETA = 500000.0
ROPE_DIM = HEAD_DIM // 4
ROPE_HALF = ROPE_DIM // 2
NSA_Q_WIDTH = NSA_HEADS * HEAD_DIM
NSA_KV_WIDTH = NSA_G * HEAD_DIM
NSA_GATE_PAD = 128
NSA_TQ = 256
V_PAD = 16
QSCALE = HEAD_DIM ** -0.5 * math.log2(math.e)

SSM_D_INNER = 2 * D_MODEL
SSM_P = 64
SSM_HEADS = SSM_D_INNER // SSM_P
SSM_G = 4
SSM_HPG = SSM_HEADS // SSM_G
SSM_N = 128
SSM_CONV = 4
SSM_CHUNK = 128
SSM_CONV_DIM = SSM_D_INNER + 2 * SSM_G * SSM_N
SSM_DT_PAD = 128
SSM_HALO = 8
SSM_IN_PAD = SSM_D_INNER + SSM_CONV_DIM + SSM_DT_PAD

FFN_HIDDEN = -(-8 * D_MODEL // (3 * 256)) * 256

LANES = 128
VMEM_LIMIT_BYTES = 52 * 1024 * 1024

MASK_BIG = 1e30
KEY_FORCED = 0x7F000000
M_INIT = -3e38

_NT = (((1,), (1,)), ((), ()))


def _cparams(*sem, flags=None):
    return pltpu.CompilerParams(dimension_semantics=sem, vmem_limit_bytes=VMEM_LIMIT_BYTES, flags=flags)


def _dot(a, b):
    return jnp.dot(a, b, preferred_element_type=F32)


def _dot_nt(a, b):
    return lax.dot_general(a, b, _NT, preferred_element_type=F32)


def _split3(x):
    a = x.astype(BF16)
    r = x - a.astype(F32)
    b = r.astype(BF16)
    c = (r - b.astype(F32)).astype(BF16)
    return a, b, c


def _rms_rows(x, w):
    return x * lax.rsqrt(jnp.mean(x * x, axis=-1, keepdims=True) + EPS) * w


def _silu(x):
    h = 0.5 * x
    return h + h * jnp.tanh(h)


def _rope_table_kernel(pos_ref, c_ref, sa_ref, sb_ref):
    pos = pos_ref[...].astype(F32)
    lane = lax.broadcasted_iota(jnp.int32, (1, LANES), 1)
    d = lane & (HEAD_DIM - 1)
    f = d & (ROPE_HALF - 1)
    inv = jnp.zeros((1, LANES), F32)
    for i in range(ROPE_HALF):
        inv = jnp.where(f == i, float(np.power(np.float32(ROPE_THETA), np.float32(-i / ROPE_HALF))), inv)
    ang = pos * inv
    cos, sin = jnp.cos(ang), jnp.sin(ang)
    c_ref[...] = jnp.where(d < ROPE_DIM, cos, 1.0)
    sa_ref[...] = jnp.where(d < ROPE_HALF, -sin, 0.0)
    sb_ref[...] = jnp.where((d >= ROPE_HALF) & (d < ROPE_DIM), sin, 0.0)


def _rope_tables(pos_col, tm):
    n = pos_col.shape[0]
    out = jax.ShapeDtypeStruct((n, LANES), F32)
    spec = pl.BlockSpec((tm, LANES), lambda i: (i, 0))
    return pl.pallas_call(
        _rope_table_kernel,
        grid=(n // tm,),
        in_specs=[pl.BlockSpec((tm, 1), lambda i: (i, 0))],
        out_specs=[spec, spec, spec],
        out_shape=[out, out, out],
        compiler_params=_cparams("parallel"),
        name="rope_tables",
    )(pos_col)


def _rope_table_t_kernel(pos_ref, c_ref, s_ref):
    pos = pos_ref[...].astype(F32)
    f = lax.broadcasted_iota(jnp.int32, (ROPE_HALF, 1), 0)
    inv = jnp.zeros((ROPE_HALF, 1), F32)
    for i in range(ROPE_HALF):
        inv = jnp.where(f == i, float(np.power(np.float32(ROPE_THETA), np.float32(-i / ROPE_HALF))), inv)
    ang = inv * pos
    c_ref[...] = jnp.cos(ang)
    s_ref[...] = jnp.sin(ang)


def _rope_tables_t(pos_row, tm):
    n = pos_row.shape[1]
    out = jax.ShapeDtypeStruct((ROPE_HALF, n), F32)
    spec = pl.BlockSpec((ROPE_HALF, tm), lambda i: (0, i))
    return pl.pallas_call(
        _rope_table_t_kernel,
        grid=(n // tm,),
        in_specs=[pl.BlockSpec((1, tm), lambda i: (0, i))],
        out_specs=[spec, spec],
        out_shape=[out, out],
        compiler_params=_cparams("parallel"),
        name="rope_tables_t",
    )(pos_row)


def _rope_lanes(x, c, sa, sb):
    return x * c + pltpu.roll(x, LANES - ROPE_HALF, 1) * sa + pltpu.roll(x, ROPE_HALF, 1) * sb


def _head_norm_rope(y, w, bd, c, sa, sb, scale):
    outs = []
    for j in range(y.shape[1] // LANES):
        yc = y[:, LANES * j:LANES * (j + 1)]
        sq = yc * yc
        hi = sq.astype(BF16)
        lo = (sq - hi.astype(F32)).astype(BF16)
        ms = _dot(hi, bd) + _dot(lo, bd)
        yn = yc * lax.rsqrt(ms + EPS) * w[:, LANES * j:LANES * (j + 1)]
        outs.append(_rope_lanes(yn, c, sa, sb) * scale)
    return jnp.concatenate(outs, axis=1)


def _nsa_inproj_kernel(x_ref, nw_ref, wqt_ref, wk_ref, wvt_ref, wgt_ref, c_ref, sa_ref, sb_ref, ct_ref, st_ref,
                       qn_ref, ksn_ref, kwn_ref, bd_ref,
                       qt_ref, kc_ref, vc_ref, ks_ref, kw_ref, vst_ref, vwt_ref, gt_ref, raw_ref):
    xn = _rms_rows(x_ref[...], nw_ref[...]).astype(BF16)
    c, sa, sb, bd = c_ref[...], sa_ref[...], sb_ref[...], bd_ref[...]
    kvw, dh, tq = NSA_KV_WIDTH, HEAD_DIM, NSA_TQ
    nrow = x_ref.shape[0] // CMP_STRIDE

    def proj(lo):
        return _dot(xn, wk_ref[:, lo:lo + kvw])

    raw = _dot(xn, wk_ref[:, 0:2 * kvw])
    for j in range(2 * kvw // LANES):
        raw_ref[j] = raw[:, LANES * j:LANES * (j + 1)]
    for which, out_ref in enumerate((kc_ref, vc_ref)):
        for g in range(NSA_G):
            j, off = divmod(which * kvw + dh * g, LANES)
            for l in range(CMP_STRIDE):
                rows = raw_ref[j, pl.ds(l, nrow, stride=CMP_STRIDE), :]
                out_ref[0, g, :, dh * l:dh * (l + 1)] = rows[:, off:off + dh]
    ks_ref[...] = _head_norm_rope(proj(2 * kvw), ksn_ref[...], bd, c, sa, sb, 1.0).astype(BF16)
    kw_ref[...] = _head_norm_rope(proj(3 * kvw), kwn_ref[...], bd, c, sa, sb, 1.0).astype(BF16)

    qn = qn_ref[...]
    for ch in range(x_ref.shape[0] // tq):
        xc = xn[ch * tq:(ch + 1) * tq]
        cos, sin = ct_ref[:, ch * tq:(ch + 1) * tq], st_ref[:, ch * tq:(ch + 1) * tq]
        yt = _dot_nt(wqt_ref[...], xc)
        for hd in range(NSA_HEADS):
            yh = yt[dh * hd:dh * (hd + 1)]
            yn = yh * lax.rsqrt(jnp.mean(yh * yh, axis=0, keepdims=True) + EPS) * qn
            x1, x2 = yn[0:ROPE_HALF], yn[ROPE_HALF:ROPE_DIM]
            rot = jnp.concatenate([x1 * cos - x2 * sin, x2 * cos + x1 * sin, yn[ROPE_DIM:]], axis=0)
            qt_ref[ch, dh * hd:dh * (hd + 1), :] = (rot * QSCALE).astype(BF16)
        vt = _dot_nt(wvt_ref[...], xc)
        vst_ref[ch] = vt[0:kvw].astype(BF16)
        vwt_ref[ch] = vt[kvw:2 * kvw].astype(BF16)
        gt_ref[ch] = 1.0 / (1.0 + jnp.exp(-_dot_nt(wgt_ref[...], xc)))


def _nsa_inproj(h, nw, wqt, wk, wvt, wgt, tabs, tabs_t, qn, ksn, kwn, bd, S, tm=512):
    T = h.shape[0]
    tq, kvw = NSA_TQ, NSA_KV_WIDTH
    per_seq = S // tm
    width = CMP_STRIDE * HEAD_DIM
    chunked = pl.BlockSpec((1, NSA_G, tm // CMP_STRIDE, width), lambda i: (i // per_seq, 0, i % per_seq, 0))
    cshape = jax.ShapeDtypeStruct((T // S, NSA_G, S // CMP_STRIDE, width), F32)
    row = lambda width: pl.BlockSpec((tm, width), lambda i: (i, 0))
    full = lambda a: pl.BlockSpec(a.shape, lambda i: (0,) * a.ndim)
    colt = pl.BlockSpec((ROPE_HALF, tm), lambda i: (0, i))
    tile = lambda ch: pl.BlockSpec((tm // tq, ch, tq), lambda i: (i, 0, 0))
    tshape = lambda ch, dt: jax.ShapeDtypeStruct((T // tq, ch, tq), dt)
    c, sa, sb = tabs
    ct, st = tabs_t
    return pl.pallas_call(
        _nsa_inproj_kernel,
        grid=(T // tm,),
        in_specs=[row(D_MODEL), full(nw), full(wqt), full(wk), full(wvt), full(wgt),
                  row(LANES), row(LANES), row(LANES), colt, colt,
                  full(qn), full(ksn), full(kwn), full(bd)],
        out_specs=[tile(NSA_Q_WIDTH), chunked, chunked, row(kvw), row(kvw), tile(kvw), tile(kvw),
                   tile(NSA_GATE_PAD)],
        out_shape=[tshape(NSA_Q_WIDTH, BF16), cshape, cshape, jax.ShapeDtypeStruct((T, kvw), BF16),
                   jax.ShapeDtypeStruct((T, kvw), BF16), tshape(kvw, BF16), tshape(kvw, BF16),
                   tshape(NSA_GATE_PAD, F32)],
        scratch_shapes=[pltpu.VMEM((2 * kvw // LANES, tm, LANES), F32)],
        compiler_params=_cparams("parallel"),
        name="nsa_inproj",
    )(h, nw, wqt, wk, wvt, wgt, c, sa, sb, ct, st, qn, ksn, kwn, bd)


def _nsa_compress_kernel(xk_ref, xv_ref, pe_ref, w1_ref, b1_ref, w2k_ref, w2vt_ref, knw_ref,
                         c_ref, sa_ref, sb_ref, kc_ref, vct_ref):
    ncp = xk_ref.shape[2]

    def hidden(x, which):
        lo = _dot((x + pe_ref[which, 0:1, :]).astype(BF16), w1_ref[which, 0])
        hi = _dot((x + pe_ref[which, 1:2, :]).astype(BF16), w1_ref[which, 1])
        return _silu(lo + pltpu.roll(hi, ncp - 1, 0) + b1_ref[which]).astype(BF16)

    for g in range(NSA_G):
        kc = _dot(hidden(xk_ref[0, g], 0), w2k_ref[...])
        ms = jnp.sum(kc * kc, axis=-1, keepdims=True) * (1.0 / HEAD_DIM)
        kn = kc * lax.rsqrt(ms + EPS) * knw_ref[...]
        kn = _rope_lanes(kn, c_ref[...], sa_ref[...], sb_ref[...])
        kc_ref[0, g] = kn[:, :HEAD_DIM].astype(BF16)
        vct_ref[0, g] = _dot_nt(w2vt_ref[...], hidden(xv_ref[0, g], 1)).astype(BF16)


def _nsa_compress(xk, xv, pe, w1, b1, w2k, w2vt, knw, tabs):
    B, G, ncp, width = xk.shape
    c, sa, sb = tabs
    xspec = pl.BlockSpec((1, G, ncp, width), lambda b: (b, 0, 0, 0))
    tspec = pl.BlockSpec((ncp, LANES), lambda b: (b, 0))
    full = lambda a: pl.BlockSpec(a.shape, lambda b: (0,) * a.ndim)
    return pl.pallas_call(
        _nsa_compress_kernel,
        grid=(B,),
        in_specs=[xspec, xspec, full(pe), full(w1), full(b1), full(w2k), full(w2vt), full(knw),
                  tspec, tspec, tspec],
        out_specs=[pl.BlockSpec((1, G, ncp, HEAD_DIM), lambda b: (b, 0, 0, 0)),
                   pl.BlockSpec((1, G, HEAD_DIM, ncp), lambda b: (b, 0, 0, 0))],
        out_shape=[jax.ShapeDtypeStruct((B, G, ncp, HEAD_DIM), BF16),
                   jax.ShapeDtypeStruct((B, G, HEAD_DIM, ncp), BF16)],
        compiler_params=_cparams("parallel"),
        name="nsa_compress",
    )(xk, xv, pe, w1, b1, w2k, w2vt, knw, c, sa, sb)


def _flash_steps(qats, k_tiles, vt_tiles, states, mask=None):
    n = len(qats)
    ahead = 2
    scores = [_dot(k_tiles[i], qats[i]) for i in range(min(ahead, n))]
    out = []
    for i in range(n):
        m, acc = states[i]
        s = scores[i]
        if mask is not None:
            s = jnp.where(mask, s, -MASK_BIG)
        m_new = jnp.maximum(m, jnp.max(s, axis=0, keepdims=True))
        p = jnp.exp2(s - m_new).astype(BF16)
        out.append((m_new, jnp.exp2(m - m_new) * acc + _dot(vt_tiles[i], p)))
        if i + ahead < n:
            scores.append(_dot(k_tiles[i + ahead], qats[i + ahead]))
    return out


def _nsa_attn_kernel(qt_ref, kc_ref, vct_ref, ks_ref, vst_ref, kw_ref, vwt_ref, gt_ref, ovl_ref, o_ref,
                     ksa_ref, kwa_ref, qat_ref, part_ref):
    S = ks_ref.shape[0]
    ncp = kc_ref.shape[2]
    nblk = S // SEL_BLOCK
    G, HP, dh, tq = NSA_G, NSA_HPG, HEAD_DIM, NSA_TQ
    cols = HP * tq
    qi = pl.program_id(1)
    q0 = qi * tq

    @pl.when(qi == 0)
    def _():
        rblk = lax.broadcasted_iota(jnp.int32, (S, dh), 0) >> SEL_SHIFT
        lane = lax.broadcasted_iota(jnp.int32, (S, dh), 1)
        onehot = jnp.where(rblk == lane, 1.0, 0.0).astype(BF16)
        zeros = jnp.zeros((S, dh), BF16)
        for g in range(G):
            ksa_ref[g, :, 0:dh] = ks_ref[:, dh * g:dh * (g + 1)]
            ksa_ref[g, :, dh:2 * dh] = onehot
            kwa_ref[g, :, 0:dh] = kw_ref[:, dh * g:dh * (g + 1)]
            kwa_ref[g, :, dh:2 * dh] = zeros

    t_cols = q0 + (lax.broadcasted_iota(jnp.int32, (1, cols), 1) & (tq - 1))
    t_q = t_cols[:, 0:tq]
    k_loc = lax.broadcasted_iota(jnp.int32, (tq, 1), 0)

    cmp_end = lax.broadcasted_iota(jnp.int32, (ncp, 1), 0) * CMP_STRIDE + (CMP_BLOCK - 1)
    cmask = cmp_end <= t_cols
    jb = lax.broadcasted_iota(jnp.int32, (nblk, tq), 0)
    jb_col = lax.broadcasted_iota(jnp.int32, (nblk, 1), 0)
    tblk = t_q >> SEL_SHIFT
    forced = (jb == 0) | (jb == tblk) | (jb == tblk - 1)
    gt = gt_ref[0]

    def gate_row(g, branch):
        return jnp.concatenate([gt[3 * (g * HP + h) + branch:3 * (g * HP + h) + branch + 1] for h in range(HP)],
                               axis=1)

    init = (jnp.full((1, cols), M_INIT, F32), jnp.zeros((dh + V_PAD, cols), F32))
    ones_rows = jnp.where(lax.broadcasted_iota(jnp.int32, (V_PAD, tq), 0) == 0, 1.0, 0.0).astype(BF16)

    def v_aug(v_ref, kt, g):
        return jnp.concatenate([v_ref[kt, dh * g:dh * (g + 1), :], ones_rows], axis=0)

    def normalised(acc):
        return acc[0:dh] * (1.0 / acc[dh:dh + 1])

    causal = (q0 + k_loc) <= t_cols
    n_back = (WINDOW + tq - 1) // tq
    back = []
    for dk in range(1, n_back + 1):
        kt = qi - dk
        far = jnp.where(kt < 0, 2 * WINDOW + S, 0)
        back.append((jnp.maximum(kt, 0), (t_cols - (kt * tq + k_loc) + far) < WINDOW))

    grp = range(G)
    for g in grp:
        qat_ref[g, dh + nblk:2 * dh, :] = jnp.zeros((dh - nblk, cols), BF16)
        for h in range(HP):
            hd = g * HP + h
            qat_ref[g, 0:dh, h * tq:(h + 1) * tq] = qt_ref[0, dh * hd:dh * (hd + 1), :]

    sc = [_dot(kc_ref[0, g], qat_ref[g, 0:dh, :]) for g in grp]
    pc = []
    for g in grp:
        s = jnp.where(cmask, sc[g], -MASK_BIG)
        m = jnp.max(s, axis=0, keepdims=True)
        p = jnp.where(cmask, jnp.exp2(s - m), 0.0)
        l = jnp.sum(p, axis=0, keepdims=True)
        pc.append(p * jnp.where(l > 0.0, 1.0 / l, 0.0))
    oc = [_dot(vct_ref[0, g], pc[g].astype(BF16)) for g in grp]
    ovl = ovl_ref[...]
    imp = []
    for g in grp:
        psum = pc[g][:, 0:tq]
        for h in range(1, HP):
            psum = psum + pc[g][:, h * tq:(h + 1) * tq]
        p1, p2, p3 = _split3(psum)
        imp.append((_dot(ovl, p1) + _dot(ovl, p2) + _dot(ovl, p3))[0:nblk])
    for g in grp:
        key = jnp.where(forced, KEY_FORCED, jnp.where(jb > tblk, -1, pltpu.bitcast(imp[g], jnp.int32)))
        cnt = jnp.zeros((nblk, tq), jnp.int32)
        for j in range(nblk):
            rj = key[j:j + 1, :] + jnp.where(jb_col > j, 1, 0)
            cnt = cnt + jnp.where(rj > key, 1, 0)
        selneg = jnp.where(cnt < SEL_TOPK, 0.0, -MASK_BIG).astype(BF16)
        for h in range(HP):
            qat_ref[g, dh:dh + nblk, h * tq:(h + 1) * tq] = selneg
    qats = [qat_ref[g] for g in grp]

    states = _flash_steps(qats, [kwa_ref[g, pl.ds(q0, tq), :] for g in grp],
                          [v_aug(vwt_ref, qi, g) for g in grp], [init] * G, causal)
    for kt, inside in back:
        k0 = pl.multiple_of(kt * tq, tq)
        states = _flash_steps(qats, [kwa_ref[g, pl.ds(k0, tq), :] for g in grp],
                              [v_aug(vwt_ref, kt, g) for g in grp], states, inside)
    for g in grp:
        part_ref[g] = gate_row(g, 0) * oc[g] + gate_row(g, 2) * normalised(states[g][1])

    sel_state = _flash_steps(qats, [ksa_ref[g, pl.ds(q0, tq), :] for g in grp],
                             [v_aug(vst_ref, qi, g) for g in grp], [init] * G, causal)

    def sel_body(kt, states):
        k0 = pl.multiple_of(kt * tq, tq)
        return tuple(_flash_steps([qat_ref[g] for g in grp], [ksa_ref[g, pl.ds(k0, tq), :] for g in grp],
                                  [v_aug(vst_ref, kt, g) for g in grp], states))

    sel_state = lax.fori_loop(0, qi, sel_body, tuple(sel_state))

    for g in range(G):
        og = part_ref[g] + gate_row(g, 1) * normalised(sel_state[g][1])
        og_t = jnp.concatenate([og[:, h * tq:(h + 1) * tq] for h in range(HP)], axis=0)
        o_ref[:, HP * dh * g:HP * dh * (g + 1)] = og_t.T.astype(BF16)


def _nsa_attn(qt, kc, vct, ks, vst, kw, vwt, gt, ovl, B, S):
    T = B * S
    tq = NSA_TQ
    nq = S // tq
    G, dh = NSA_G, HEAD_DIM
    ncp = kc.shape[2]
    qspec = pl.BlockSpec((1, NSA_Q_WIDTH, tq), lambda b, i: (b * nq + i, 0, 0))
    gspec = pl.BlockSpec((1, NSA_GATE_PAD, tq), lambda b, i: (b * nq + i, 0, 0))
    kcspec = pl.BlockSpec((1, G, ncp, dh), lambda b, i: (b, 0, 0, 0))
    vcspec = pl.BlockSpec((1, G, dh, ncp), lambda b, i: (b, 0, 0, 0))
    kspec = pl.BlockSpec((S, NSA_KV_WIDTH), lambda b, i: (b, 0))
    vspec = pl.BlockSpec((nq, NSA_KV_WIDTH, tq), lambda b, i: (b, 0, 0))
    ovspec = pl.BlockSpec(ovl.shape, lambda b, i: (0, 0))
    return pl.pallas_call(
        _nsa_attn_kernel,
        grid=(B, nq),
        in_specs=[qspec, kcspec, vcspec, kspec, vspec, kspec, vspec, gspec, ovspec],
        out_specs=pl.BlockSpec((tq, NSA_Q_WIDTH), lambda b, i: (b * nq + i, 0)),
        out_shape=jax.ShapeDtypeStruct((T, NSA_Q_WIDTH), BF16),
        scratch_shapes=[pltpu.VMEM((G, S, 2 * dh), BF16), pltpu.VMEM((G, S, 2 * dh), BF16),
                        pltpu.VMEM((G, 2 * dh, NSA_HPG * tq), BF16),
                        pltpu.VMEM((G, dh, NSA_HPG * tq), F32)],
        compiler_params=_cparams("arbitrary", "arbitrary"),
        name="nsa_attn",
    )(qt, kc, vct, ks, vst, kw, vwt, gt, ovl)


def _ffn_kernel(x_ref, y_ref, wo_ref, nw_ref, wg_ref, wu_ref, wd_ref, o_ref, xn_ref, acc_ref):
    k = pl.program_id(1)

    @pl.when(k == 0)
    def _():
        h1 = x_ref[...] + _dot(y_ref[...], wo_ref[...])
        xn_ref[...] = _rms_rows(h1, nw_ref[...]).astype(BF16)
        acc_ref[...] = h1

    xn = xn_ref[...]
    a = (_silu(_dot(xn, wg_ref[...])) * _dot(xn, wu_ref[...])).astype(BF16)
    acc_ref[...] += _dot(a, wd_ref[...])

    @pl.when(k == pl.num_programs(1) - 1)
    def _():
        o_ref[...] = acc_ref[...]


def _ffn(h, y, wo, nw, wg, wu, wd, tm=512, th=1408):
    T = h.shape[0]
    H = wg.shape[1]
    K = y.shape[1]
    return pl.pallas_call(
        _ffn_kernel,
        grid=(T // tm, H // th),
        in_specs=[pl.BlockSpec((tm, D_MODEL), lambda i, k: (i, 0)),
                  pl.BlockSpec((tm, K), lambda i, k: (i, 0)),
                  pl.BlockSpec((K, D_MODEL), lambda i, k: (0, 0)),
                  pl.BlockSpec((1, D_MODEL), lambda i, k: (0, 0)),
                  pl.BlockSpec((D_MODEL, th), lambda i, k: (0, k)),
                  pl.BlockSpec((D_MODEL, th), lambda i, k: (0, k)),
                  pl.BlockSpec((th, D_MODEL), lambda i, k: (k, 0))],
        out_specs=pl.BlockSpec((tm, D_MODEL), lambda i, k: (i, 0)),
        out_shape=jax.ShapeDtypeStruct((T, D_MODEL), F32),
        scratch_shapes=[pltpu.VMEM((tm, D_MODEL), BF16), pltpu.VMEM((tm, D_MODEL), F32)],
        compiler_params=_cparams("parallel", "arbitrary"),
        name="ffn",
    )(h, y, wo, nw, wg, wu, wd)


def _ssm_inproj_kernel(x_ref, nw_ref, w_ref, cw_ref, cb_ref, dtb_ref, alog_ref, tri_ref,
                       zs_ref, xbc_ref, cum_ref, cumt_ref, dtt_ref, ext_ref, *, tiles_per_seq):
    tm = x_ref.shape[0]
    halo, Q = SSM_HALO, SSM_CHUNK
    chunk = 256
    xn = _rms_rows(x_ref[...], nw_ref[...]).astype(BF16)

    @pl.when(pl.program_id(0) % tiles_per_seq == 0)
    def _():
        ext_ref[...] = jnp.zeros_like(ext_ref)

    row = lax.broadcasted_iota(jnp.int32, (halo, chunk), 0)

    def gate_out(lo, y):
        zs_ref[:, lo:lo + chunk] = _silu(y).astype(BF16)

    def conv_out(lo, x):
        prev = ext_ref[:, lo:lo + chunk]
        acc = cb_ref[:, lo:lo + chunk] + cw_ref[SSM_CONV - 1:SSM_CONV, lo:lo + chunk] * x
        for k in range(SSM_CONV - 1):
            sh = SSM_CONV - 1 - k
            r = pltpu.roll(x, sh, 0)
            top = jnp.where(row < sh, pltpu.roll(prev, sh, 0), r[0:halo])
            acc = acc + cw_ref[k:k + 1, lo:lo + chunk] * jnp.concatenate([top, r[halo:]], axis=0)
        ext_ref[:, lo:lo + chunk] = x[tm - halo:tm]
        xbc_ref[:, lo:lo + chunk] = _silu(acc).astype(BF16)

    gate_jobs = [(lo, lo, gate_out) for lo in range(0, SSM_D_INNER, chunk)]
    conv_jobs = [(SSM_D_INNER + lo, lo, conv_out) for lo in range(0, SSM_CONV_DIM, chunk)]
    jobs = []
    for j in range(max(len(gate_jobs), len(conv_jobs))):
        jobs += conv_jobs[j:j + 1] + gate_jobs[j:j + 1]
    proj = lambda j: _dot(xn, w_ref[:, jobs[j][0]:jobs[j][0] + chunk])
    nxt = proj(0)
    for j in range(len(jobs)):
        cur = nxt
        if j + 1 < len(jobs):
            nxt = proj(j + 1)
        jobs[j][2](jobs[j][1], cur)

    base = SSM_D_INNER + SSM_CONV_DIM
    dtl = _dot(xn, w_ref[:, base:base + SSM_DT_PAD]) + dtb_ref[...]
    dt = jnp.maximum(dtl, 0.0) + jnp.log(1.0 + jnp.exp(-jnp.abs(dtl)))
    a = dt * (-jnp.exp(alog_ref[...]))
    tri = tri_ref[...]
    for c in range(tm // Q):
        a1, a2, a3 = _split3(a[Q * c:Q * (c + 1)])
        cum = _dot(tri, a1) + _dot(tri, a2) + _dot(tri, a3)
        cum_ref[Q * c:Q * (c + 1), :] = cum
        cumt_ref[c] = cum.T
        dtt_ref[c] = dt[Q * c:Q * (c + 1)].T


def _ssm_inproj(h, nw, w, cw, cb, dtb, alog, tri, S, tm=256):
    T = h.shape[0]
    Q = SSM_CHUNK
    row = lambda width: pl.BlockSpec((tm, width), lambda i: (i, 0))
    full = lambda a: pl.BlockSpec(a.shape, lambda i: (0,) * a.ndim)
    tile = pl.BlockSpec((tm // Q, SSM_DT_PAD, Q), lambda i: (i, 0, 0))
    tshape = jax.ShapeDtypeStruct((T // Q, SSM_DT_PAD, Q), F32)
    return pl.pallas_call(
        functools.partial(_ssm_inproj_kernel, tiles_per_seq=S // tm),
        grid=(T // tm,),
        in_specs=[row(D_MODEL), full(nw), full(w), full(cw), full(cb), full(dtb), full(alog), full(tri)],
        out_specs=[row(SSM_D_INNER), row(SSM_CONV_DIM), row(SSM_DT_PAD), tile, tile],
        out_shape=[jax.ShapeDtypeStruct((T, SSM_D_INNER), BF16), jax.ShapeDtypeStruct((T, SSM_CONV_DIM), BF16),
                   jax.ShapeDtypeStruct((T, SSM_DT_PAD), F32), tshape, tshape],
        scratch_shapes=[pltpu.VMEM((SSM_HALO, SSM_CONV_DIM), F32)],
        compiler_params=_cparams("arbitrary"),
        name="ssm_inproj",
    )(h, nw, w, cw, cb, dtb, alog, tri)


def _ssd_kernel(xbc_ref, zs_ref, cum_ref, cumt_ref, dtt_ref, dsk_ref, nw_ref, y_ref, state_ref):
    Q, P, N, G, HPG = SSM_CHUNK, SSM_P, SSM_N, SSM_G, SSM_HPG
    gw = SSM_D_INNER // G

    @pl.when(pl.program_id(1) == 0)
    def _():
        state_ref[...] = jnp.zeros_like(state_ref)

    cum = cum_ref[...]
    cum_t = cumt_ref[0]
    dt_t = dtt_ref[0]
    row_i = lax.broadcasted_iota(jnp.int32, (Q, Q), 0)
    col_i = lax.broadcasted_iota(jnp.int32, (Q, Q), 1)
    tril = row_i >= col_i
    eye = row_i == col_i
    b_off = SSM_D_INNER
    c_off = SSM_D_INNER + G * N

    def elementwise(g):
        cg = xbc_ref[:, c_off + N * g:c_off + N * (g + 1)]
        bg = xbc_ref[:, b_off + N * g:b_off + N * (g + 1)]
        cb = _dot_nt(cg, bg)
        cg_f = cg.astype(F32)
        bg_t = bg.astype(F32).T
        st_g = state_ref[g]
        st_b = st_g.astype(BF16)
        lhs, wgt, rhs, keep = [], [], [], []
        for hh in range(HPG):
            h = g * HPG + hh
            cum_b = jnp.broadcast_to(cum[:, h:h + 1], (Q, Q))
            cum_row = cum_t[h:h + 1, :]
            dt_row = dt_t[h:h + 1, :]
            cum_last = cum_row[:, Q - 1:Q]
            mm = cb * jnp.exp(jnp.where(tril, cum_b - cum_row, -jnp.inf)) * dt_row
            mm = jnp.where(eye, mm + dsk_ref[:, h:h + 1], mm)
            lhs.append(jnp.concatenate([mm.astype(BF16), (jnp.exp(cum_b) * cg_f).astype(BF16)], axis=1))
            wgt.append((bg_t * (dt_row * jnp.exp(cum_last - cum_row))).astype(BF16))
            rhs.append(jnp.concatenate([xbc_ref[:, P * h:P * (h + 1)], st_b[:, P * hh:P * (hh + 1)]], axis=0))
            keep.append(jnp.broadcast_to(jnp.exp(cum_last), (1, P)))
        return lhs, wgt, rhs, st_g * jnp.concatenate(keep, axis=1)

    def matmuls(g, ops):
        lhs, wgt, rhs, kept = ops
        ys = [_dot(lhs[hh], rhs[hh]) for hh in range(HPG)]
        upd = [_dot(wgt[hh], rhs[hh][0:Q]) for hh in range(HPG)]
        state_ref[g] = kept + jnp.concatenate(upd, axis=1)
        yg = jnp.concatenate(ys, axis=1) * zs_ref[:, gw * g:gw * (g + 1)].astype(F32)
        yg = yg * lax.rsqrt(jnp.mean(yg * yg, axis=-1, keepdims=True) + EPS)
        y_ref[:, gw * g:gw * (g + 1)] = (yg * nw_ref[:, gw * g:gw * (g + 1)]).astype(BF16)

    ops = elementwise(0)
    for g in range(G):
        nxt = elementwise(g + 1) if g + 1 < G else None
        matmuls(g, ops)
        ops = nxt


def _ssd(xbc, zs, cum, cumt, dtt, dsk, nw, B, S):
    Q = SSM_CHUNK
    nch = S // Q
    row = lambda width: pl.BlockSpec((Q, width), lambda b, c: (b * nch + c, 0))
    full = lambda a: pl.BlockSpec(a.shape, lambda b, c: (0,) * a.ndim)
    tile = pl.BlockSpec((1, SSM_DT_PAD, Q), lambda b, c: (b * nch + c, 0, 0))
    return pl.pallas_call(
        _ssd_kernel,
        grid=(B, nch),
        in_specs=[row(SSM_CONV_DIM), row(SSM_D_INNER), row(SSM_DT_PAD), tile, tile, full(dsk), full(nw)],
        out_specs=row(SSM_D_INNER),
        out_shape=jax.ShapeDtypeStruct((B * S, SSM_D_INNER), BF16),
        scratch_shapes=[pltpu.VMEM((SSM_G, SSM_N, SSM_HPG * SSM_P), F32)],
        compiler_params=_cparams("arbitrary", "arbitrary"),
        name="ssd_scan",
    )(xbc, zs, cum, cumt, dtt, dsk, nw)


def _block_diag_mean():
    i = np.arange(LANES)
    return jnp.asarray((i[:, None] // HEAD_DIM == i[None, :] // HEAD_DIM) / HEAD_DIM, BF16)


def _overlap_t(S):
    nc = (S - CMP_BLOCK) // CMP_STRIDE + 1
    ncp = S // CMP_STRIDE
    nblk = S // SEL_BLOCK
    starts = np.arange(ncp) * CMP_STRIDE
    js = np.arange(nblk)[:, None] * SEL_BLOCK
    ov = (starts[None, :] < js + SEL_BLOCK) & (starts[None, :] + CMP_BLOCK > js) & (np.arange(ncp)[None, :] < nc)
    out = np.zeros((LANES, ncp), np.float32)
    out[:nblk] = ov
    return jnp.asarray(out, BF16)


def _pad_cols(w, width):
    return jnp.pad(w, ((0, 0), (0, width - w.shape[1])))


def _nsa_layer(h, tabs, tabs_t, tabs_c, B, S, nw, w_in, q_norm, k_norm, cmp_pe, cmp_w1, cmp_b1, cmp_w2, w_out):
    G, dh, kvw = NSA_G, HEAD_DIM, NSA_KV_WIDTH
    ncp = S // CMP_STRIDE
    cut = lambda i: w_in[:, NSA_Q_WIDTH + i * kvw:NSA_Q_WIDTH + (i + 1) * kvw]
    wqt = w_in[:, :NSA_Q_WIDTH].T.astype(BF16)
    wk = jnp.concatenate([cut(0), cut(1), cut(2), cut(4)], axis=1).astype(BF16)
    wvt = jnp.concatenate([cut(3), cut(5)], axis=1).T.astype(BF16)
    wgt = _pad_cols(w_in[:, NSA_Q_WIDTH + 6 * kvw:], NSA_GATE_PAD).T.astype(BF16)
    qn = jnp.broadcast_to(q_norm[:, None], (dh, NSA_TQ))
    ksn = jnp.tile(k_norm[1], G)[None, :]
    kwn = jnp.tile(k_norm[2], G)[None, :]
    qt, kc_raw, vc_raw, ks, kw, vst, vwt, gt = _nsa_inproj(h, nw[None, :], wqt, wk, wvt, wgt, tabs, tabs_t,
                                                           qn, ksn, kwn, _block_diag_mean(), S)
    half = CMP_STRIDE * dh
    pe = cmp_pe.reshape(2, 2, half)
    w1 = cmp_w1.reshape(2, 2, half, CMP_HIDDEN).astype(BF16)
    b1 = cmp_b1[:, None, :]
    w2k = _pad_cols(cmp_w2[0], LANES).astype(BF16)
    w2vt = cmp_w2[1].T.astype(BF16)
    knw = _pad_cols(k_norm[0][None, :], LANES)
    kc, vct = _nsa_compress(kc_raw, vc_raw, pe, w1, b1, w2k, w2vt, knw, tabs_c)

    o = _nsa_attn(qt, kc, vct, ks, vst, kw, vwt, gt, _overlap_t(S), B, S)
    return o, w_out.astype(BF16)


def _ssd_layer(h, B, S, nw, w_in, conv_w, conv_b, dt_bias, a_log, d_skip, norm_w, w_out):
    w = _pad_cols(w_in, SSM_IN_PAD).astype(BF16)
    pad1 = lambda v: _pad_cols(v[None, :], SSM_DT_PAD)
    tri = jnp.asarray(np.tril(np.ones((SSM_CHUNK, SSM_CHUNK), np.float32)), BF16)
    zs, xbc, cum, cumt, dtt = _ssm_inproj(h, nw[None, :], w, conv_w, conv_b[None, :], pad1(dt_bias), pad1(a_log),
                                          tri, S)
    y = _ssd(xbc, zs, cum, cumt, dtt, pad1(d_skip), norm_w[None, :], B, S)
    return y, w_out.astype(BF16)


def kernel(x, positions, mix_norm_w, ffn_norm_w, ffn_w_gate, ffn_w_up, ffn_w_down, nsa_w_in, nsa_q_norm, nsa_k_norm, nsa_cmp_pe, nsa_cmp_w1, nsa_cmp_b1, nsa_cmp_w2, nsa_w_out, ssm_w_in, ssm_conv_w, ssm_conv_b, ssm_dt_bias, ssm_a_log, ssm_d, ssm_norm_w, ssm_w_out):
    B, S, D = x.shape
    T = B * S
    h = x.reshape(T, D)
    ncp = S // CMP_STRIDE
    tabs = _rope_tables(positions.reshape(T, 1), 1024)
    tabs_t = _rope_tables_t(positions.reshape(1, T), 2048)
    pos_c = jnp.pad(positions[:, CMP_BLOCK - 1::CMP_STRIDE], ((0, 0), (0, 1)))[:, :ncp]
    tabs_c = _rope_tables(pos_c.reshape(B * ncp, 1), ncp)
    for i in range(DEPTH):
        j = i // 2
        if i % 2 == 0:
            y, wo = _nsa_layer(h, tabs, tabs_t, tabs_c, B, S, mix_norm_w[i], nsa_w_in[j], nsa_q_norm[j],
                               nsa_k_norm[j], nsa_cmp_pe[j], nsa_cmp_w1[j], nsa_cmp_b1[j], nsa_cmp_w2[j],
                               nsa_w_out[j])
        else:
            y, wo = _ssd_layer(h, B, S, mix_norm_w[i], ssm_w_in[j], ssm_conv_w[j], ssm_conv_b[j], ssm_dt_bias[j],
                               ssm_a_log[j], ssm_d[j], ssm_norm_w[j], ssm_w_out[j])
        h = _ffn(h, y, wo, ffn_norm_w[i][None, :], ffn_w_gate[i].astype(BF16), ffn_w_up[i].astype(BF16),
                 ffn_w_down[i].astype(BF16))
    return h.reshape(B, S, D)
```

```python
import functools
import math

import numpy as np
import jax
import jax.numpy as jnp
from jax import lax
from jax.experimental import pallas as pl
from jax.experimental.pallas import tpu as pltpu

F32 = jnp.float32
BF16 = jnp.bfloat16

D_MODEL = 1024
DEPTH = 4
EPS = 1e-6

NSA_HEADS = 16
NSA_G = 4
NSA_HPG = NSA_HEADS // NSA_G
HEAD_DIM = 64
CMP_BLOCK = 32
CMP_STRIDE = 16
CMP_HIDDEN = 256
SEL_BLOCK = 64
SEL_SHIFT = 6
SEL_TOPK = 8
WINDOW = 512
ROPE_THETA = 500000.0
ROPE_DIM = HEAD_DIM // 4
ROPE_HALF = ROPE_DIM // 2
NSA_Q_WIDTH = NSA_HEADS * HEAD_DIM
NSA_KV_WIDTH = NSA_G * HEAD_DIM
NSA_GATE_PAD = 128
NSA_TQ = 256
V_PAD = 16
QSCALE = HEAD_DIM ** -0.5 * math.log2(math.e)

SSM_D_INNER = 2 * D_MODEL
SSM_P = 64
SSM_HEADS = SSM_D_INNER // SSM_P
SSM_G = 4
SSM_HPG = SSM_HEADS // SSM_G
SSM_N = 128
SSM_CONV = 4
SSM_CHUNK = 128
SSM_CONV_DIM = SSM_D_INNER + 2 * SSM_G * SSM_N
SSM_DT_PAD = 128
SSM_HALO = 8
SSM_IN_PAD = SSM_D_INNER + SSM_CONV_DIM + SSM_DT_PAD

FFN_HIDDEN = -(-8 * D_MODEL // (3 * 256)) * 256
FFN_PIECE = 768

LANES = 128
VMEM_LIMIT_BYTES = 52 * 1024 * 1024

MASK_BIG = 1e30
KEY_FORCED = 0x7F000000
M_INIT = -3e38

_NT = (((1,), (1,)), ((), ()))


def _cparams(*sem, flags=None):
    return pltpu.CompilerParams(dimension_semantics=sem, vmem_limit_bytes=VMEM_LIMIT_BYTES, flags=flags)


def _dot(a, b):
    return jnp.dot(a, b, preferred_element_type=F32)


def _dot_nt(a, b):
    return lax.dot_general(a, b, _NT, preferred_element_type=F32)


def _split3(x):
    a = x.astype(BF16)
    r = x - a.astype(F32)
    b = r.astype(BF16)
    c = (r - b.astype(F32)).astype(BF16)
    return a, b, c


def _rms_rows(x, w):
    return x * lax.rsqrt(jnp.mean(x * x, axis=-1, keepdims=True) + EPS) * w


def _silu(x):
    h = 0.5 * x
    return h + h * jnp.tanh(h)


def _rope_table_kernel(pos_ref, c_ref, sa_ref, sb_ref):
    pos = pos_ref[...].astype(F32)
    lane = lax.broadcasted_iota(jnp.int32, (1, LANES), 1)
    d = lane & (HEAD_DIM - 1)
    f = d & (ROPE_HALF - 1)
    inv = jnp.zeros((1, LANES), F32)
    for i in range(ROPE_HALF):
        inv = jnp.where(f == i, float(np.power(np.float32(ROPE_THETA), np.float32(-i / ROPE_HALF))), inv)
    ang = pos * inv
    cos, sin = jnp.cos(ang), jnp.sin(ang)
    c_ref[...] = jnp.where(d < ROPE_DIM, cos, 1.0)
    sa_ref[...] = jnp.where(d < ROPE_HALF, -sin, 0.0)
    sb_ref[...] = jnp.where((d >= ROPE_HALF) & (d < ROPE_DIM), sin, 0.0)


def _rope_tables(pos_col, tm):
    n = pos_col.shape[0]
    out = jax.ShapeDtypeStruct((n, LANES), F32)
    spec = pl.BlockSpec((tm, LANES), lambda i: (i, 0))
    return pl.pallas_call(
        _rope_table_kernel,
        grid=(n // tm,),
        in_specs=[pl.BlockSpec((tm, 1), lambda i: (i, 0))],
        out_specs=[spec, spec, spec],
        out_shape=[out, out, out],
        compiler_params=_cparams("parallel"),
        name="rope_tables",
    )(pos_col)


def _rope_table_t_kernel(pos_ref, c_ref, s_ref):
    pos = pos_ref[...].astype(F32)
    f = lax.broadcasted_iota(jnp.int32, (ROPE_HALF, 1), 0)
    inv = jnp.zeros((ROPE_HALF, 1), F32)
    for i in range(ROPE_HALF):
        inv = jnp.where(f == i, float(np.power(np.float32(ROPE_THETA), np.float32(-i / ROPE_HALF))), inv)
    ang = inv * pos
    c_ref[...] = jnp.cos(ang)
    s_ref[...] = jnp.sin(ang)


def _rope_tables_t(pos_row, tm):
    n = pos_row.shape[1]
    out = jax.ShapeDtypeStruct((ROPE_HALF, n), F32)
    spec = pl.BlockSpec((ROPE_HALF, tm), lambda i: (0, i))
    return pl.pallas_call(
        _rope_table_t_kernel,
        grid=(n // tm,),
        in_specs=[pl.BlockSpec((1, tm), lambda i: (0, i))],
        out_specs=[spec, spec],
        out_shape=[out, out],
        compiler_params=_cparams("parallel"),
        name="rope_tables_t",
    )(pos_row)


def _rope_lanes(x, c, sa, sb):
    return x * c + pltpu.roll(x, LANES - ROPE_HALF, 1) * sa + pltpu.roll(x, ROPE_HALF, 1) * sb


def _head_norm_rope(y, w, bd, c, sa, sb, scale):
    outs = []
    for j in range(y.shape[1] // LANES):
        yc = y[:, LANES * j:LANES * (j + 1)]
        sq = yc * yc
        hi = sq.astype(BF16)
        lo = (sq - hi.astype(F32)).astype(BF16)
        ms = _dot(hi, bd) + _dot(lo, bd)
        yn = yc * lax.rsqrt(ms + EPS) * w[:, LANES * j:LANES * (j + 1)]
        outs.append(_rope_lanes(yn, c, sa, sb) * scale)
    return jnp.concatenate(outs, axis=1)


def _nsa_inproj_kernel(x_ref, nw_ref, wqt_ref, wk_ref, wvt_ref, wgt_ref, c_ref, sa_ref, sb_ref, ct_ref, st_ref,
                       qn_ref, ksn_ref, kwn_ref, bd_ref,
                       qt_ref, kc_ref, vc_ref, ks_ref, kw_ref, vst_ref, vwt_ref, gt_ref, raw_ref):
    xn = _rms_rows(x_ref[...], nw_ref[...]).astype(BF16)
    c, sa, sb, bd = c_ref[...], sa_ref[...], sb_ref[...], bd_ref[...]
    kvw, dh, tq = NSA_KV_WIDTH, HEAD_DIM, NSA_TQ
    nrow = x_ref.shape[0] // CMP_STRIDE

    qn = qn_ref[...]

    def raw_out(raw):
        for j in range(2 * kvw // LANES):
            raw_ref[j] = raw[:, LANES * j:LANES * (j + 1)]
        for which, out_ref in enumerate((kc_ref, vc_ref)):
            for g in range(NSA_G):
                j, off = divmod(which * kvw + dh * g, LANES)
                for l in range(CMP_STRIDE):
                    rows = raw_ref[j, pl.ds(l, nrow, stride=CMP_STRIDE), :]
                    out_ref[0, g, :, dh * l:dh * (l + 1)] = rows[:, off:off + dh]

    def key_out(out_ref, gain_ref):
        def write(y):
            out_ref[...] = _head_norm_rope(y, gain_ref[...], bd, c, sa, sb, 1.0).astype(BF16)
        return write

    def q_out(ch):
        def write(yt):
            cos, sin = ct_ref[:, ch * tq:(ch + 1) * tq], st_ref[:, ch * tq:(ch + 1) * tq]
            for hd in range(NSA_HEADS):
                yh = yt[dh * hd:dh * (hd + 1)]
                yn = yh * lax.rsqrt(jnp.mean(yh * yh, axis=0, keepdims=True) + EPS) * qn
                x1, x2 = yn[0:ROPE_HALF], yn[ROPE_HALF:ROPE_DIM]
                rot = jnp.concatenate([x1 * cos - x2 * sin, x2 * cos + x1 * sin, yn[ROPE_DIM:]], axis=0)
                qt_ref[ch, dh * hd:dh * (hd + 1), :] = (rot * QSCALE).astype(BF16)
        return write

    def v_out(ch):
        def write(vt):
            vst_ref[ch] = vt[0:kvw].astype(BF16)
            vwt_ref[ch] = vt[kvw:2 * kvw].astype(BF16)
        return write

    def g_out(ch):
        def write(gl):
            gt_ref[ch] = 1.0 / (1.0 + jnp.exp(-gl))
        return write

    jobs = [(lambda: _dot(xn, wk_ref[:, 0:2 * kvw]), raw_out),
            (lambda: _dot(xn, wk_ref[:, 2 * kvw:3 * kvw]), key_out(ks_ref, ksn_ref)),
            (lambda: _dot(xn, wk_ref[:, 3 * kvw:4 * kvw]), key_out(kw_ref, kwn_ref))]
    for ch in range(x_ref.shape[0] // tq):
        xc = xn[ch * tq:(ch + 1) * tq]
        jobs += [(functools.partial(_dot_nt, wqt_ref[...], xc), q_out(ch)),
                 (functools.partial(_dot_nt, wvt_ref[...], xc), v_out(ch)),
                 (functools.partial(_dot_nt, wgt_ref[...], xc), g_out(ch))]
    nxt = jobs[0][0]()
    for j, (_, epilogue) in enumerate(jobs):
        cur = nxt
        if j + 1 < len(jobs):
            nxt = jobs[j + 1][0]()
        epilogue(cur)


def _nsa_inproj(h, nw, wqt, wk, wvt, wgt, tabs, tabs_t, qn, ksn, kwn, bd, S, tm=512):
    T = h.shape[0]
    tq, kvw = NSA_TQ, NSA_KV_WIDTH
    per_seq = S // tm
    width = CMP_STRIDE * HEAD_DIM
    chunked = pl.BlockSpec((1, NSA_G, tm // CMP_STRIDE, width), lambda i: (i // per_seq, 0, i % per_seq, 0))
    cshape = jax.ShapeDtypeStruct((T // S, NSA_G, S // CMP_STRIDE, width), F32)
    row = lambda width: pl.BlockSpec((tm, width), lambda i: (i, 0))
    full = lambda a: pl.BlockSpec(a.shape, lambda i: (0,) * a.ndim)
    colt = pl.BlockSpec((ROPE_HALF, tm), lambda i: (0, i))
    tile = lambda ch: pl.BlockSpec((tm // tq, ch, tq), lambda i: (i, 0, 0))
    tshape = lambda ch, dt: jax.ShapeDtypeStruct((T // tq, ch, tq), dt)
    c, sa, sb = tabs
    ct, st = tabs_t
    return pl.pallas_call(
        _nsa_inproj_kernel,
        grid=(T // tm,),
        in_specs=[row(D_MODEL), full(nw), full(wqt), full(wk), full(wvt), full(wgt),
                  row(LANES), row(LANES), row(LANES), colt, colt,
                  full(qn), full(ksn), full(kwn), full(bd)],
        out_specs=[tile(NSA_Q_WIDTH), chunked, chunked, row(kvw), row(kvw), tile(kvw), tile(kvw),
                   tile(NSA_GATE_PAD)],
        out_shape=[tshape(NSA_Q_WIDTH, BF16), cshape, cshape, jax.ShapeDtypeStruct((T, kvw), BF16),
                   jax.ShapeDtypeStruct((T, kvw), BF16), tshape(kvw, BF16), tshape(kvw, BF16),
                   tshape(NSA_GATE_PAD, F32)],
        scratch_shapes=[pltpu.VMEM((2 * kvw // LANES, tm, LANES), F32)],
        compiler_params=_cparams("parallel"),
        name="nsa_inproj",
    )(h, nw, wqt, wk, wvt, wgt, c, sa, sb, ct, st, qn, ksn, kwn, bd)


def _nsa_compress_kernel(xk_ref, xv_ref, pe_ref, w1_ref, b1_ref, w2k_ref, w2vt_ref, knw_ref,
                         c_ref, sa_ref, sb_ref, kc_ref, vct_ref):
    ncp = xk_ref.shape[2]

    def hidden(x, which):
        lo = _dot((x + pe_ref[which, 0:1, :]).astype(BF16), w1_ref[which, 0])
        hi = _dot((x + pe_ref[which, 1:2, :]).astype(BF16), w1_ref[which, 1])
        return _silu(lo + pltpu.roll(hi, ncp - 1, 0) + b1_ref[which]).astype(BF16)

    for g in range(NSA_G):
        kc = _dot(hidden(xk_ref[0, g], 0), w2k_ref[...])
        ms = jnp.sum(kc * kc, axis=-1, keepdims=True) * (1.0 / HEAD_DIM)
        kn = kc * lax.rsqrt(ms + EPS) * knw_ref[...]
        kn = _rope_lanes(kn, c_ref[...], sa_ref[...], sb_ref[...])
        kc_ref[0, g] = kn[:, :HEAD_DIM].astype(BF16)
        vct_ref[0, g] = _dot_nt(w2vt_ref[...], hidden(xv_ref[0, g], 1)).astype(BF16)


def _nsa_compress(xk, xv, pe, w1, b1, w2k, w2vt, knw, tabs):
    B, G, ncp, width = xk.shape
    c, sa, sb = tabs
    xspec = pl.BlockSpec((1, G, ncp, width), lambda b: (b, 0, 0, 0))
    tspec = pl.BlockSpec((ncp, LANES), lambda b: (b, 0))
    full = lambda a: pl.BlockSpec(a.shape, lambda b: (0,) * a.ndim)
    return pl.pallas_call(
        _nsa_compress_kernel,
        grid=(B,),
        in_specs=[xspec, xspec, full(pe), full(w1), full(b1), full(w2k), full(w2vt), full(knw),
                  tspec, tspec, tspec],
        out_specs=[pl.BlockSpec((1, G, ncp, HEAD_DIM), lambda b: (b, 0, 0, 0)),
                   pl.BlockSpec((1, G, HEAD_DIM, ncp), lambda b: (b, 0, 0, 0))],
        out_shape=[jax.ShapeDtypeStruct((B, G, ncp, HEAD_DIM), BF16),
                   jax.ShapeDtypeStruct((B, G, HEAD_DIM, ncp), BF16)],
        compiler_params=_cparams("parallel"),
        name="nsa_compress",
    )(xk, xv, pe, w1, b1, w2k, w2vt, knw, c, sa, sb)


def _flash_steps(qats, k_tiles, vt_tiles, states, mask=None):
    n = len(qats)
    ahead = 2
    scores = [_dot(k_tiles[i], qats[i]) for i in range(min(ahead, n))]
    out = []
    for i in range(n):
        m, acc = states[i]
        s = scores[i]
        if mask is not None:
            s = jnp.where(mask, s, -MASK_BIG)
        m_new = jnp.maximum(m, jnp.max(s, axis=0, keepdims=True))
        p = jnp.exp2(s - m_new).astype(BF16)
        out.append((m_new, jnp.exp2(m - m_new) * acc + _dot(vt_tiles[i], p)))
        if i + ahead < n:
            scores.append(_dot(k_tiles[i + ahead], qats[i + ahead]))
    return out


def _nsa_attn_kernel(qt_ref, kc_ref, vct_ref, ks_ref, vst_ref, kw_ref, vwt_ref, gt_ref, ovl_ref, o_ref,
                     ksa_ref, kwa_ref, qat_ref, part_ref):
    S = ks_ref.shape[0]
    ncp = kc_ref.shape[2]
    nblk = S // SEL_BLOCK
    G, HP, dh, tq = NSA_G, NSA_HPG, HEAD_DIM, NSA_TQ
    cols = HP * tq
    qi = pl.program_id(1)
    q0 = qi * tq

    @pl.when(qi == 0)
    def _():
        rblk = lax.broadcasted_iota(jnp.int32, (S, dh), 0) >> SEL_SHIFT
        lane = lax.broadcasted_iota(jnp.int32, (S, dh), 1)
        onehot = jnp.where(rblk == lane, 1.0, 0.0).astype(BF16)
        zeros = jnp.zeros((S, dh), BF16)
        for g in range(G):
            ksa_ref[g, :, 0:dh] = ks_ref[:, dh * g:dh * (g + 1)]
            ksa_ref[g, :, dh:2 * dh] = onehot
            kwa_ref[g, :, 0:dh] = kw_ref[:, dh * g:dh * (g + 1)]
            kwa_ref[g, :, dh:2 * dh] = zeros

    t_cols = q0 + (lax.broadcasted_iota(jnp.int32, (1, cols), 1) & (tq - 1))
    t_q = t_cols[:, 0:tq]
    k_loc = lax.broadcasted_iota(jnp.int32, (tq, 1), 0)

    cmp_end = lax.broadcasted_iota(jnp.int32, (ncp, 1), 0) * CMP_STRIDE + (CMP_BLOCK - 1)
    cmask = cmp_end <= t_cols
    jb = lax.broadcasted_iota(jnp.int32, (nblk, tq), 0)
    jb_col = lax.broadcasted_iota(jnp.int32, (nblk, 1), 0)
    tblk = t_q >> SEL_SHIFT
    forced = (jb == 0) | (jb == tblk) | (jb == tblk - 1)
    gt = gt_ref[0]

    def gate_row(g, branch):
        return jnp.concatenate([gt[3 * (g * HP + h) + branch:3 * (g * HP + h) + branch + 1] for h in range(HP)],
                               axis=1)

    init = (jnp.full((1, cols), M_INIT, F32), jnp.zeros((dh + V_PAD, cols), F32))
    ones_rows = jnp.where(lax.broadcasted_iota(jnp.int32, (V_PAD, tq), 0) == 0, 1.0, 0.0).astype(BF16)

    def v_aug(v_ref, kt, g):
        return jnp.concatenate([v_ref[kt, dh * g:dh * (g + 1), :], ones_rows], axis=0)

    def normalised(acc):
        return acc[0:dh] * (1.0 / acc[dh:dh + 1])

    causal = (q0 + k_loc) <= t_cols
    n_back = (WINDOW + tq - 1) // tq
    back = []
    for dk in range(1, n_back + 1):
        kt = qi - dk
        far = jnp.where(kt < 0, 2 * WINDOW + S, 0)
        back.append((jnp.maximum(kt, 0), (t_cols - (kt * tq + k_loc) + far) < WINDOW))

    grp = range(G)
    for g in grp:
        qat_ref[g, dh + nblk:2 * dh, :] = jnp.zeros((dh - nblk, cols), BF16)
        for h in range(HP):
            hd = g * HP + h
            qat_ref[g, 0:dh, h * tq:(h + 1) * tq] = qt_ref[0, dh * hd:dh * (hd + 1), :]

    sc = [_dot(kc_ref[0, g], qat_ref[g, 0:dh, :]) for g in grp]
    pc = []
    for g in grp:
        s = jnp.where(cmask, sc[g], -MASK_BIG)
        m = jnp.max(s, axis=0, keepdims=True)
        p = jnp.where(cmask, jnp.exp2(s - m), 0.0)
        l = jnp.sum(p, axis=0, keepdims=True)
        pc.append(p * jnp.where(l > 0.0, 1.0 / l, 0.0))
    oc = [_dot(vct_ref[0, g], pc[g].astype(BF16)) for g in grp]
    ovl = ovl_ref[...]
    imp = []
    for g in grp:
        psum = pc[g][:, 0:tq]
        for h in range(1, HP):
            psum = psum + pc[g][:, h * tq:(h + 1) * tq]
        p1, p2, p3 = _split3(psum)
        imp.append((_dot(ovl, p1) + _dot(ovl, p2) + _dot(ovl, p3))[0:nblk])
    for g in grp:
        key = jnp.where(forced, KEY_FORCED, jnp.where(jb > tblk, -1, pltpu.bitcast(imp[g], jnp.int32)))
        cnt = jnp.zeros((nblk, tq), jnp.int32)
        for j in range(nblk):
            rj = key[j:j + 1, :] + jnp.where(jb_col > j, 1, 0)
            cnt = cnt + jnp.where(rj > key, 1, 0)
        selneg = jnp.where(cnt < SEL_TOPK, 0.0, -MASK_BIG).astype(BF16)
        for h in range(HP):
            qat_ref[g, dh:dh + nblk, h * tq:(h + 1) * tq] = selneg
    qats = [qat_ref[g] for g in grp]

    states = _flash_steps(qats, [kwa_ref[g, pl.ds(q0, tq), :] for g in grp],
                          [v_aug(vwt_ref, qi, g) for g in grp], [init] * G, causal)
    for kt, inside in back:
        k0 = pl.multiple_of(kt * tq, tq)
        states = _flash_steps(qats, [kwa_ref[g, pl.ds(k0, tq), :] for g in grp],
                              [v_aug(vwt_ref, kt, g) for g in grp], states, inside)
    for g in grp:
        part_ref[g] = gate_row(g, 0) * oc[g] + gate_row(g, 2) * normalised(states[g][1])

    sel_state = _flash_steps(qats, [ksa_ref[g, pl.ds(q0, tq), :] for g in grp],
                             [v_aug(vst_ref, qi, g) for g in grp], [init] * G, causal)

    def sel_body(kt, states):
        k0 = pl.multiple_of(kt * tq, tq)
        return tuple(_flash_steps([qat_ref[g] for g in grp], [ksa_ref[g, pl.ds(k0, tq), :] for g in grp],
                                  [v_aug(vst_ref, kt, g) for g in grp], states))

    sel_state = lax.fori_loop(0, qi, sel_body, tuple(sel_state))

    for g in range(G):
        og = part_ref[g] + gate_row(g, 1) * normalised(sel_state[g][1])
        og_t = jnp.concatenate([og[:, h * tq:(h + 1) * tq] for h in range(HP)], axis=0)
        o_ref[:, HP * dh * g:HP * dh * (g + 1)] = og_t.T.astype(BF16)


def _nsa_attn(qt, kc, vct, ks, vst, kw, vwt, gt, ovl, B, S):
    T = B * S
    tq = NSA_TQ
    nq = S // tq
    G, dh = NSA_G, HEAD_DIM
    ncp = kc.shape[2]
    qspec = pl.BlockSpec((1, NSA_Q_WIDTH, tq), lambda b, i: (b * nq + i, 0, 0))
    gspec = pl.BlockSpec((1, NSA_GATE_PAD, tq), lambda b, i: (b * nq + i, 0, 0))
    kcspec = pl.BlockSpec((1, G, ncp, dh), lambda b, i: (b, 0, 0, 0))
    vcspec = pl.BlockSpec((1, G, dh, ncp), lambda b, i: (b, 0, 0, 0))
    kspec = pl.BlockSpec((S, NSA_KV_WIDTH), lambda b, i: (b, 0))
    vspec = pl.BlockSpec((nq, NSA_KV_WIDTH, tq), lambda b, i: (b, 0, 0))
    ovspec = pl.BlockSpec(ovl.shape, lambda b, i: (0, 0))
    return pl.pallas_call(
        _nsa_attn_kernel,
        grid=(B, nq),
        in_specs=[qspec, kcspec, vcspec, kspec, vspec, kspec, vspec, gspec, ovspec],
        out_specs=pl.BlockSpec((tq, NSA_Q_WIDTH), lambda b, i: (b * nq + i, 0)),
        out_shape=jax.ShapeDtypeStruct((T, NSA_Q_WIDTH), BF16),
        scratch_shapes=[pltpu.VMEM((G, S, 2 * dh), BF16), pltpu.VMEM((G, S, 2 * dh), BF16),
                        pltpu.VMEM((G, 2 * dh, NSA_HPG * tq), BF16),
                        pltpu.VMEM((G, dh, NSA_HPG * tq), F32)],
        compiler_params=_cparams("arbitrary", "arbitrary"),
        name="nsa_attn",
    )(qt, kc, vct, ks, vst, kw, vwt, gt, ovl)


def _ffn_kernel(x_ref, y_ref, wo_ref, nw_ref, wg_ref, wu_ref, wd_ref, o_ref):
    h1 = x_ref[...] + _dot(y_ref[...], wo_ref[...])
    xn = _rms_rows(h1, nw_ref[...]).astype(BF16)
    hidden = wg_ref.shape[1]
    cuts = list(range(0, hidden, FFN_PIECE)) + [hidden]
    pieces = list(zip(cuts[:-1], cuts[1:]))
    gate_up = lambda a, b: (_dot(xn, wg_ref[:, a:b]), _dot(xn, wu_ref[:, a:b]))
    nxt = gate_up(*pieces[0])
    out = h1
    for i, (a, b) in enumerate(pieces):
        g, u = nxt
        if i + 1 < len(pieces):
            nxt = gate_up(*pieces[i + 1])
        out = out + _dot((_silu(g) * u).astype(BF16), wd_ref[a:b, :])
    o_ref[...] = out


def _ffn(h, y, wo, nw, wg, wu, wd, tm=512):
    T = h.shape[0]
    H = wg.shape[1]
    K = y.shape[1]
    row = lambda width: pl.BlockSpec((tm, width), lambda i: (i, 0))
    resident = lambda a: pl.BlockSpec(a.shape, lambda i: (0,) * a.ndim, pipeline_mode=pl.Buffered(1))
    return pl.pallas_call(
        _ffn_kernel,
        grid=(T // tm,),
        in_specs=[row(D_MODEL), row(K), resident(wo), resident(nw), resident(wg), resident(wu), resident(wd)],
        out_specs=row(D_MODEL),
        out_shape=jax.ShapeDtypeStruct((T, D_MODEL), F32),
        compiler_params=_cparams("parallel"),
        name="ffn",
    )(h, y, wo, nw, wg, wu, wd)


def _ssm_inproj_kernel(x_ref, nw_ref, w_ref, cw_ref, cb_ref, dtb_ref, alog_ref, tri_ref,
                       zs_ref, xbc_ref, cum_ref, cumt_ref, dtt_ref, ext_ref, *, tiles_per_seq):
    tm = x_ref.shape[0]
    halo, Q = SSM_HALO, SSM_CHUNK
    chunk = 256
    xn = _rms_rows(x_ref[...], nw_ref[...]).astype(BF16)

    @pl.when(pl.program_id(0) % tiles_per_seq == 0)
    def _():
        ext_ref[...] = jnp.zeros_like(ext_ref)

    row = lax.broadcasted_iota(jnp.int32, (halo, chunk), 0)

    def gate_out(lo, y):
        zs_ref[:, lo:lo + chunk] = _silu(y).astype(BF16)

    def conv_out(lo, x):
        prev = ext_ref[:, lo:lo + chunk]
        acc = cb_ref[:, lo:lo + chunk] + cw_ref[SSM_CONV - 1:SSM_CONV, lo:lo + chunk] * x
        for k in range(SSM_CONV - 1):
            sh = SSM_CONV - 1 - k
            r = pltpu.roll(x, sh, 0)
            top = jnp.where(row < sh, pltpu.roll(prev, sh, 0), r[0:halo])
            acc = acc + cw_ref[k:k + 1, lo:lo + chunk] * jnp.concatenate([top, r[halo:]], axis=0)
        ext_ref[:, lo:lo + chunk] = x[tm - halo:tm]
        xbc_ref[:, lo:lo + chunk] = _silu(acc).astype(BF16)

    gate_jobs = [(lo, lo, gate_out) for lo in range(0, SSM_D_INNER, chunk)]
    conv_jobs = [(SSM_D_INNER + lo, lo, conv_out) for lo in range(0, SSM_CONV_DIM, chunk)]
    jobs = []
    for j in range(max(len(gate_jobs), len(conv_jobs))):
        jobs += conv_jobs[j:j + 1] + gate_jobs[j:j + 1]
    proj = lambda j: _dot(xn, w_ref[:, jobs[j][0]:jobs[j][0] + chunk])
    nxt = proj(0)
    for j in range(len(jobs)):
        cur = nxt
        if j + 1 < len(jobs):
            nxt = proj(j + 1)
        jobs[j][2](jobs[j][1], cur)

    base = SSM_D_INNER + SSM_CONV_DIM
    dtl = _dot(xn, w_ref[:, base:base + SSM_DT_PAD]) + dtb_ref[...]
    dt = jnp.maximum(dtl, 0.0) + jnp.log(1.0 + jnp.exp(-jnp.abs(dtl)))
    a = dt * (-jnp.exp(alog_ref[...]))
    tri = tri_ref[...]
    for c in range(tm // Q):
        a1, a2, a3 = _split3(a[Q * c:Q * (c + 1)])
        cum = _dot(tri, a1) + _dot(tri, a2) + _dot(tri, a3)
        cum_ref[Q * c:Q * (c + 1), :] = cum
        cumt_ref[c] = cum.T
        dtt_ref[c] = dt[Q * c:Q * (c + 1)].T


def _ssm_inproj(h, nw, w, cw, cb, dtb, alog, tri, S, tm=256):
    T = h.shape[0]
    Q = SSM_CHUNK
    row = lambda width: pl.BlockSpec((tm, width), lambda i: (i, 0))
    full = lambda a: pl.BlockSpec(a.shape, lambda i: (0,) * a.ndim)
    tile = pl.BlockSpec((tm // Q, SSM_DT_PAD, Q), lambda i: (i, 0, 0))
    tshape = jax.ShapeDtypeStruct((T // Q, SSM_DT_PAD, Q), F32)
    return pl.pallas_call(
        functools.partial(_ssm_inproj_kernel, tiles_per_seq=S // tm),
        grid=(T // tm,),
        in_specs=[row(D_MODEL), full(nw), full(w), full(cw), full(cb), full(dtb), full(alog), full(tri)],
        out_specs=[row(SSM_D_INNER), row(SSM_CONV_DIM), row(SSM_DT_PAD), tile, tile],
        out_shape=[jax.ShapeDtypeStruct((T, SSM_D_INNER), BF16), jax.ShapeDtypeStruct((T, SSM_CONV_DIM), BF16),
                   jax.ShapeDtypeStruct((T, SSM_DT_PAD), F32), tshape, tshape],
        scratch_shapes=[pltpu.VMEM((SSM_HALO, SSM_CONV_DIM), F32)],
        compiler_params=_cparams("arbitrary"),
        name="ssm_inproj",
    )(h, nw, w, cw, cb, dtb, alog, tri)


def _ssd_kernel(xbc_ref, zs_ref, cum_ref, cumt_ref, dtt_ref, dsk_ref, nw_ref, y_ref, state_ref):
    Q, P, N, G, HPG = SSM_CHUNK, SSM_P, SSM_N, SSM_G, SSM_HPG
    gw = SSM_D_INNER // G

    @pl.when(pl.program_id(1) == 0)
    def _():
        state_ref[...] = jnp.zeros_like(state_ref)

    cum = cum_ref[...]
    cum_t = cumt_ref[0]
    dt_t = dtt_ref[0]
    row_i = lax.broadcasted_iota(jnp.int32, (Q, Q), 0)
    col_i = lax.broadcasted_iota(jnp.int32, (Q, Q), 1)
    tril = row_i >= col_i
    eye = row_i == col_i
    b_off = SSM_D_INNER
    c_off = SSM_D_INNER + G * N

    def elementwise(g):
        cg = xbc_ref[:, c_off + N * g:c_off + N * (g + 1)]
        bg = xbc_ref[:, b_off + N * g:b_off + N * (g + 1)]
        cb = _dot_nt(cg, bg)
        cg_f = cg.astype(F32)
        bg_t = bg.astype(F32).T
        st_g = state_ref[g]
        st_b = st_g.astype(BF16)
        lhs, wgt, rhs, keep = [], [], [], []
        for hh in range(HPG):
            h = g * HPG + hh
            cum_b = jnp.broadcast_to(cum[:, h:h + 1], (Q, Q))
            cum_row = cum_t[h:h + 1, :]
            dt_row = dt_t[h:h + 1, :]
            cum_last = cum_row[:, Q - 1:Q]
            mm = cb * jnp.exp(jnp.where(tril, cum_b - cum_row, -jnp.inf)) * dt_row
            mm = jnp.where(eye, mm + dsk_ref[:, h:h + 1], mm)
            lhs.append(jnp.concatenate([mm.astype(BF16), (jnp.exp(cum_b) * cg_f).astype(BF16)], axis=1))
            wgt.append((bg_t * (dt_row * jnp.exp(cum_last - cum_row))).astype(BF16))
            rhs.append(jnp.concatenate([xbc_ref[:, P * h:P * (h + 1)], st_b[:, P * hh:P * (hh + 1)]], axis=0))
            keep.append(jnp.broadcast_to(jnp.exp(cum_last), (1, P)))
        return lhs, wgt, rhs, st_g * jnp.concatenate(keep, axis=1)

    def matmuls(g, ops):
        lhs, wgt, rhs, kept = ops
        ys = [_dot(lhs[hh], rhs[hh]) for hh in range(HPG)]
        upd = [_dot(wgt[hh], rhs[hh][0:Q]) for hh in range(HPG)]
        state_ref[g] = kept + jnp.concatenate(upd, axis=1)
        yg = jnp.concatenate(ys, axis=1) * zs_ref[:, gw * g:gw * (g + 1)].astype(F32)
        yg = yg * lax.rsqrt(jnp.mean(yg * yg, axis=-1, keepdims=True) + EPS)
        y_ref[:, gw * g:gw * (g + 1)] = (yg * nw_ref[:, gw * g:gw * (g + 1)]).astype(BF16)

    ops = elementwise(0)
    for g in range(G):
        nxt = elementwise(g + 1) if g + 1 < G else None
        matmuls(g, ops)
        ops = nxt


def _ssd(xbc, zs, cum, cumt, dtt, dsk, nw, B, S):
    Q = SSM_CHUNK
    nch = S // Q
    row = lambda width: pl.BlockSpec((Q, width), lambda b, c: (b * nch + c, 0))
    full = lambda a: pl.BlockSpec(a.shape, lambda b, c: (0,) * a.ndim)
    tile = pl.BlockSpec((1, SSM_DT_PAD, Q), lambda b, c: (b * nch + c, 0, 0))
    return pl.pallas_call(
        _ssd_kernel,
        grid=(B, nch),
        in_specs=[row(SSM_CONV_DIM), row(SSM_D_INNER), row(SSM_DT_PAD), tile, tile, full(dsk), full(nw)],
        out_specs=row(SSM_D_INNER),
        out_shape=jax.ShapeDtypeStruct((B * S, SSM_D_INNER), BF16),
        scratch_shapes=[pltpu.VMEM((SSM_G, SSM_N, SSM_HPG * SSM_P), F32)],
        compiler_params=_cparams("arbitrary", "arbitrary"),
        name="ssd_scan",
    )(xbc, zs, cum, cumt, dtt, dsk, nw)


def _block_diag_mean():
    i = np.arange(LANES)
    return jnp.asarray((i[:, None] // HEAD_DIM == i[None, :] // HEAD_DIM) / HEAD_DIM, BF16)


def _overlap_t(S):
    nc = (S - CMP_BLOCK) // CMP_STRIDE + 1
    ncp = S // CMP_STRIDE
    nblk = S // SEL_BLOCK
    starts = np.arange(ncp) * CMP_STRIDE
    js = np.arange(nblk)[:, None] * SEL_BLOCK
    ov = (starts[None, :] < js + SEL_BLOCK) & (starts[None, :] + CMP_BLOCK > js) & (np.arange(ncp)[None, :] < nc)
    out = np.zeros((LANES, ncp), np.float32)
    out[:nblk] = ov
    return jnp.asarray(out, BF16)


def _pad_cols(w, width):
    return jnp.pad(w, ((0, 0), (0, width - w.shape[1])))


def _nsa_layer(h, tabs, tabs_t, tabs_c, B, S, nw, w_in, q_norm, k_norm, cmp_pe, cmp_w1, cmp_b1, cmp_w2, w_out):
    G, dh, kvw = NSA_G, HEAD_DIM, NSA_KV_WIDTH
    ncp = S // CMP_STRIDE
    cut = lambda i: w_in[:, NSA_Q_WIDTH + i * kvw:NSA_Q_WIDTH + (i + 1) * kvw]
    wqt = w_in[:, :NSA_Q_WIDTH].T.astype(BF16)
    wk = jnp.concatenate([cut(0), cut(1), cut(2), cut(4)], axis=1).astype(BF16)
    wvt = jnp.concatenate([cut(3), cut(5)], axis=1).T.astype(BF16)
    wgt = _pad_cols(w_in[:, NSA_Q_WIDTH + 6 * kvw:], NSA_GATE_PAD).T.astype(BF16)
    qn = jnp.broadcast_to(q_norm[:, None], (dh, NSA_TQ))
    ksn = jnp.tile(k_norm[1], G)[None, :]
    kwn = jnp.tile(k_norm[2], G)[None, :]
    qt, kc_raw, vc_raw, ks, kw, vst, vwt, gt = _nsa_inproj(h, nw[None, :], wqt, wk, wvt, wgt, tabs, tabs_t,
                                                           qn, ksn, kwn, _block_diag_mean(), S)
    half = CMP_STRIDE * dh
    pe = cmp_pe.reshape(2, 2, half)
    w1 = cmp_w1.reshape(2, 2, half, CMP_HIDDEN).astype(BF16)
    b1 = cmp_b1[:, None, :]
    w2k = _pad_cols(cmp_w2[0], LANES).astype(BF16)
    w2vt = cmp_w2[1].T.astype(BF16)
    knw = _pad_cols(k_norm[0][None, :], LANES)
    kc, vct = _nsa_compress(kc_raw, vc_raw, pe, w1, b1, w2k, w2vt, knw, tabs_c)

    o = _nsa_attn(qt, kc, vct, ks, vst, kw, vwt, gt, _overlap_t(S), B, S)
    return o, w_out.astype(BF16)


def _ssd_layer(h, B, S, nw, w_in, conv_w, conv_b, dt_bias, a_log, d_skip, norm_w, w_out):
    w = _pad_cols(w_in, SSM_IN_PAD).astype(BF16)
    pad1 = lambda v: _pad_cols(v[None, :], SSM_DT_PAD)
    tri = jnp.asarray(np.tril(np.ones((SSM_CHUNK, SSM_CHUNK), np.float32)), BF16)
    zs, xbc, cum, cumt, dtt = _ssm_inproj(h, nw[None, :], w, conv_w, conv_b[None, :], pad1(dt_bias), pad1(a_log),
                                          tri, S)
    y = _ssd(xbc, zs, cum, cumt, dtt, pad1(d_skip), norm_w[None, :], B, S)
    return y, w_out.astype(BF16)


def kernel(x, positions, mix_norm_w, ffn_norm_w, ffn_w_gate, ffn_w_up, ffn_w_down, nsa_w_in, nsa_q_norm, nsa_k_norm, nsa_cmp_pe, nsa_cmp_w1, nsa_cmp_b1, nsa_cmp_w2, nsa_w_out, ssm_w_in, ssm_conv_w, ssm_conv_b, ssm_dt_bias, ssm_a_log, ssm_d, ssm_norm_w, ssm_w_out):
    B, S, D = x.shape
    T = B * S
    h = x.reshape(T, D)
    ncp = S // CMP_STRIDE
    tabs = _rope_tables(positions.reshape(T, 1), 1024)
    tabs_t = _rope_tables_t(positions.reshape(1, T), 2048)
    pos_c = jnp.pad(positions[:, CMP_BLOCK - 1::CMP_STRIDE], ((0, 0), (0, 1)))[:, :ncp]
    tabs_c = _rope_tables(pos_c.reshape(B * ncp, 1), ncp)
    for i in range(DEPTH):
        j = i // 2
        if i % 2 == 0:
            y, wo = _nsa_layer(h, tabs, tabs_t, tabs_c, B, S, mix_norm_w[i], nsa_w_in[j], nsa_q_norm[j],
                               nsa_k_norm[j], nsa_cmp_pe[j], nsa_cmp_w1[j], nsa_cmp_b1[j], nsa_cmp_w2[j],
                               nsa_w_out[j])
        else:
            y, wo = _ssd_layer(h, B, S, mix_norm_w[i], ssm_w_in[j], ssm_conv_w[j], ssm_conv_b[j], ssm_dt_bias[j],
                               ssm_a_log[j], ssm_d[j], ssm_norm_w[j], ssm_w_out[j])
        h = _ffn(h, y, wo, ffn_norm_w[i][None, :], ffn_w_gate[i].astype(BF16), ffn_w_up[i].astype(BF16),
                 ffn_w_down[i].astype(BF16))
    return h.reshape(B, S, D)
```

```python
import functools
import math

import numpy as np
import jax
import jax.numpy as jnp
from jax import lax
from jax.experimental import pallas as pl
from jax.experimental.pallas import tpu as pltpu

F32 = jnp.float32
BF16 = jnp.bfloat16

D_MODEL = 1024
DEPTH = 4
EPS = 1e-6

NSA_HEADS = 16
NSA_G = 4
NSA_HPG = NSA_HEADS // NSA_G
HEAD_DIM = 64
CMP_BLOCK = 32
CMP_STRIDE = 16
CMP_HIDDEN = 256
SEL_BLOCK = 64
SEL_SHIFT = 6
SEL_TOPK = 8
WINDOW = 512
ROPE_THETA = 500000.0
ROPE_DIM = HEAD_DIM // 4
ROPE_HALF = ROPE_DIM // 2
NSA_Q_WIDTH = NSA_HEADS * HEAD_DIM
NSA_KV_WIDTH = NSA_G * HEAD_DIM
NSA_GATE_PAD = 128
NSA_TQ = 256
V_PAD = 16
QSCALE = HEAD_DIM ** -0.5 * math.log2(math.e)

SSM_D_INNER = 2 * D_MODEL
SSM_P = 64
SSM_HEADS = SSM_D_INNER // SSM_P
SSM_G = 4
SSM_HPG = SSM_HEADS // SSM_G
SSM_N = 128
SSM_CONV = 4
SSM_CHUNK = 128
SSM_CONV_DIM = SSM_D_INNER + 2 * SSM_G * SSM_N
SSM_DT_PAD = 128
SSM_HALO = 8
SSM_IN_PAD = SSM_D_INNER + SSM_CONV_DIM + SSM_DT_PAD

FFN_HIDDEN = -(-8 * D_MODEL // (3 * 256)) * 256
FFN_PIECE = 768

LANES = 128
VMEM_LIMIT_BYTES = 52 * 1024 * 1024

MASK_BIG = 1e30
KEY_FORCED = 0x7F000000
M_INIT = -3e38

_NT = (((1,), (1,)), ((), ()))


def _cparams(*sem, flags=None):
    return pltpu.CompilerParams(dimension_semantics=sem, vmem_limit_bytes=VMEM_LIMIT_BYTES, flags=flags)


def _dot(a, b):
    return jnp.dot(a, b, preferred_element_type=F32)


def _dot_nt(a, b):
    return lax.dot_general(a, b, _NT, preferred_element_type=F32)


def _split3(x):
    a = x.astype(BF16)
    r = x - a.astype(F32)
    b = r.astype(BF16)
    c = (r - b.astype(F32)).astype(BF16)
    return a, b, c


def _rms_rows(x, w):
    return x * lax.rsqrt(jnp.mean(x * x, axis=-1, keepdims=True) + EPS) * w


def _silu(x):
    h = 0.5 * x
    return h + h * jnp.tanh(h)


def _rope_table_kernel(pos_ref, c_ref, sa_ref, sb_ref):
    pos = pos_ref[...].astype(F32)
    lane = lax.broadcasted_iota(jnp.int32, (1, LANES), 1)
    d = lane & (HEAD_DIM - 1)
    f = d & (ROPE_HALF - 1)
    inv = jnp.zeros((1, LANES), F32)
    for i in range(ROPE_HALF):
        inv = jnp.where(f == i, float(np.power(np.float32(ROPE_THETA), np.float32(-i / ROPE_HALF))), inv)
    ang = pos * inv
    cos, sin = jnp.cos(ang), jnp.sin(ang)
    c_ref[...] = jnp.where(d < ROPE_DIM, cos, 1.0)
    sa_ref[...] = jnp.where(d < ROPE_HALF, -sin, 0.0)
    sb_ref[...] = jnp.where((d >= ROPE_HALF) & (d < ROPE_DIM), sin, 0.0)


def _rope_tables(pos_col, tm):
    n = pos_col.shape[0]
    out = jax.ShapeDtypeStruct((n, LANES), F32)
    spec = pl.BlockSpec((tm, LANES), lambda i: (i, 0))
    return pl.pallas_call(
        _rope_table_kernel,
        grid=(n // tm,),
        in_specs=[pl.BlockSpec((tm, 1), lambda i: (i, 0))],
        out_specs=[spec, spec, spec],
        out_shape=[out, out, out],
        compiler_params=_cparams("parallel"),
        name="rope_tables",
    )(pos_col)


def _rope_table_t_kernel(pos_ref, c_ref, s_ref):
    pos = pos_ref[...].astype(F32)
    f = lax.broadcasted_iota(jnp.int32, (ROPE_HALF, 1), 0)
    inv = jnp.zeros((ROPE_HALF, 1), F32)
    for i in range(ROPE_HALF):
        inv = jnp.where(f == i, float(np.power(np.float32(ROPE_THETA), np.float32(-i / ROPE_HALF))), inv)
    ang = inv * pos
    c_ref[...] = jnp.cos(ang)
    s_ref[...] = jnp.sin(ang)


def _rope_tables_t(pos_row, tm):
    n = pos_row.shape[1]
    out = jax.ShapeDtypeStruct((ROPE_HALF, n), F32)
    spec = pl.BlockSpec((ROPE_HALF, tm), lambda i: (0, i))
    return pl.pallas_call(
        _rope_table_t_kernel,
        grid=(n // tm,),
        in_specs=[pl.BlockSpec((1, tm), lambda i: (0, i))],
        out_specs=[spec, spec],
        out_shape=[out, out],
        compiler_params=_cparams("parallel"),
        name="rope_tables_t",
    )(pos_row)


def _rope_lanes(x, c, sa, sb):
    return x * c + pltpu.roll(x, LANES - ROPE_HALF, 1) * sa + pltpu.roll(x, ROPE_HALF, 1) * sb


def _head_norm_rope(y, w, bd, c, sa, sb, scale):
    outs = []
    for j in range(y.shape[1] // LANES):
        yc = y[:, LANES * j:LANES * (j + 1)]
        sq = yc * yc
        hi = sq.astype(BF16)
        lo = (sq - hi.astype(F32)).astype(BF16)
        ms = _dot(hi, bd) + _dot(lo, bd)
        yn = yc * lax.rsqrt(ms + EPS) * w[:, LANES * j:LANES * (j + 1)]
        outs.append(_rope_lanes(yn, c, sa, sb) * scale)
    return jnp.concatenate(outs, axis=1)


def _nsa_inproj_kernel(x_ref, nw_ref, wqt_ref, wk_ref, wvt_ref, wgt_ref, c_ref, sa_ref, sb_ref, ct_ref, st_ref,
                       qn_ref, ksn_ref, kwn_ref, bd_ref,
                       qt_ref, kc_ref, vc_ref, ks_ref, kw_ref, vst_ref, vwt_ref, gt_ref, raw_ref):
    xn = _rms_rows(x_ref[...], nw_ref[...]).astype(BF16)
    c, sa, sb, bd = c_ref[...], sa_ref[...], sb_ref[...], bd_ref[...]
    kvw, dh, tq = NSA_KV_WIDTH, HEAD_DIM, NSA_TQ
    nrow = x_ref.shape[0] // CMP_STRIDE

    qn = qn_ref[...]

    def raw_out(raw):
        for j in range(2 * kvw // LANES):
            raw_ref[j] = raw[:, LANES * j:LANES * (j + 1)]
        for which, out_ref in enumerate((kc_ref, vc_ref)):
            for g in range(NSA_G):
                j, off = divmod(which * kvw + dh * g, LANES)
                for l in range(CMP_STRIDE):
                    rows = raw_ref[j, pl.ds(l, nrow, stride=CMP_STRIDE), :]
                    out_ref[0, g, :, dh * l:dh * (l + 1)] = rows[:, off:off + dh]

    def key_out(out_ref, gain_ref):
        def write(y):
            out_ref[...] = _head_norm_rope(y, gain_ref[...], bd, c, sa, sb, 1.0).astype(BF16)
        return write

    def q_out(ch):
        def write(yt):
            cos, sin = ct_ref[:, ch * tq:(ch + 1) * tq], st_ref[:, ch * tq:(ch + 1) * tq]
            for hd in range(NSA_HEADS):
                yh = yt[dh * hd:dh * (hd + 1)]
                yn = yh * lax.rsqrt(jnp.mean(yh * yh, axis=0, keepdims=True) + EPS) * qn
                x1, x2 = yn[0:ROPE_HALF], yn[ROPE_HALF:ROPE_DIM]
                rot = jnp.concatenate([x1 * cos - x2 * sin, x2 * cos + x1 * sin, yn[ROPE_DIM:]], axis=0)
                qt_ref[ch, dh * hd:dh * (hd + 1), :] = (rot * QSCALE).astype(BF16)
        return write

    def v_out(ch):
        def write(vt):
            vst_ref[ch] = vt[0:kvw].astype(BF16)
            vwt_ref[ch] = vt[kvw:2 * kvw].astype(BF16)
        return write

    def g_out(ch):
        def write(gl):
            gt_ref[ch] = 1.0 / (1.0 + jnp.exp(-gl))
        return write

    jobs = [(lambda: _dot(xn, wk_ref[:, 0:2 * kvw]), raw_out),
            (lambda: _dot(xn, wk_ref[:, 2 * kvw:3 * kvw]), key_out(ks_ref, ksn_ref)),
            (lambda: _dot(xn, wk_ref[:, 3 * kvw:4 * kvw]), key_out(kw_ref, kwn_ref))]
    for ch in range(x_ref.shape[0] // tq):
        xc = xn[ch * tq:(ch + 1) * tq]
        jobs += [(functools.partial(_dot_nt, wqt_ref[...], xc), q_out(ch)),
                 (functools.partial(_dot_nt, wvt_ref[...], xc), v_out(ch)),
                 (functools.partial(_dot_nt, wgt_ref[...], xc), g_out(ch))]
    nxt = jobs[0][0]()
    for j, (_, epilogue) in enumerate(jobs):
        cur = nxt
        if j + 1 < len(jobs):
            nxt = jobs[j + 1][0]()
        epilogue(cur)


def _nsa_inproj(h, nw, wqt, wk, wvt, wgt, tabs, tabs_t, qn, ksn, kwn, bd, S, tm=512):
    T = h.shape[0]
    tq, kvw = NSA_TQ, NSA_KV_WIDTH
    per_seq = S // tm
    width = CMP_STRIDE * HEAD_DIM
    chunked = pl.BlockSpec((1, NSA_G, tm // CMP_STRIDE, width), lambda i: (i // per_seq, 0, i % per_seq, 0))
    cshape = jax.ShapeDtypeStruct((T // S, NSA_G, S // CMP_STRIDE, width), F32)
    row = lambda width: pl.BlockSpec((tm, width), lambda i: (i, 0))
    full = lambda a: pl.BlockSpec(a.shape, lambda i: (0,) * a.ndim)
    colt = pl.BlockSpec((ROPE_HALF, tm), lambda i: (0, i))
    tile = lambda ch: pl.BlockSpec((tm // tq, ch, tq), lambda i: (i, 0, 0))
    tshape = lambda ch, dt: jax.ShapeDtypeStruct((T // tq, ch, tq), dt)
    c, sa, sb = tabs
    ct, st = tabs_t
    return pl.pallas_call(
        _nsa_inproj_kernel,
        grid=(T // tm,),
        in_specs=[row(D_MODEL), full(nw), full(wqt), full(wk), full(wvt), full(wgt),
                  row(LANES), row(LANES), row(LANES), colt, colt,
                  full(qn), full(ksn), full(kwn), full(bd)],
        out_specs=[tile(NSA_Q_WIDTH), chunked, chunked, row(kvw), row(kvw), tile(kvw), tile(kvw),
                   tile(NSA_GATE_PAD)],
        out_shape=[tshape(NSA_Q_WIDTH, BF16), cshape, cshape, jax.ShapeDtypeStruct((T, kvw), BF16),
                   jax.ShapeDtypeStruct((T, kvw), BF16), tshape(kvw, BF16), tshape(kvw, BF16),
                   tshape(NSA_GATE_PAD, F32)],
        scratch_shapes=[pltpu.VMEM((2 * kvw // LANES, tm, LANES), F32)],
        compiler_params=_cparams("parallel"),
        name="nsa_inproj",
    )(h, nw, wqt, wk, wvt, wgt, c, sa, sb, ct, st, qn, ksn, kwn, bd)


def _nsa_compress_kernel(xk_ref, xv_ref, pe_ref, w1_ref, b1_ref, w2k_ref, w2vt_ref, knw_ref,
                         c_ref, sa_ref, sb_ref, kc_ref, vct_ref):
    ncp = xk_ref.shape[2]

    def hidden(x, which):
        lo = _dot((x + pe_ref[which, 0:1, :]).astype(BF16), w1_ref[which, 0])
        hi = _dot((x + pe_ref[which, 1:2, :]).astype(BF16), w1_ref[which, 1])
        return _silu(lo + pltpu.roll(hi, ncp - 1, 0) + b1_ref[which]).astype(BF16)

    for g in range(NSA_G):
        kc = _dot(hidden(xk_ref[0, g], 0), w2k_ref[...])
        ms = jnp.sum(kc * kc, axis=-1, keepdims=True) * (1.0 / HEAD_DIM)
        kn = kc * lax.rsqrt(ms + EPS) * knw_ref[...]
        kn = _rope_lanes(kn, c_ref[...], sa_ref[...], sb_ref[...])
        kc_ref[0, g] = kn[:, :HEAD_DIM].astype(BF16)
        vct_ref[0, g] = _dot_nt(w2vt_ref[...], hidden(xv_ref[0, g], 1)).astype(BF16)


def _nsa_compress(xk, xv, pe, w1, b1, w2k, w2vt, knw, tabs):
    B, G, ncp, width = xk.shape
    c, sa, sb = tabs
    xspec = pl.BlockSpec((1, G, ncp, width), lambda b: (b, 0, 0, 0))
    tspec = pl.BlockSpec((ncp, LANES), lambda b: (b, 0))
    full = lambda a: pl.BlockSpec(a.shape, lambda b: (0,) * a.ndim)
    return pl.pallas_call(
        _nsa_compress_kernel,
        grid=(B,),
        in_specs=[xspec, xspec, full(pe), full(w1), full(b1), full(w2k), full(w2vt), full(knw),
                  tspec, tspec, tspec],
        out_specs=[pl.BlockSpec((1, G, ncp, HEAD_DIM), lambda b: (b, 0, 0, 0)),
                   pl.BlockSpec((1, G, HEAD_DIM, ncp), lambda b: (b, 0, 0, 0))],
        out_shape=[jax.ShapeDtypeStruct((B, G, ncp, HEAD_DIM), BF16),
                   jax.ShapeDtypeStruct((B, G, HEAD_DIM, ncp), BF16)],
        compiler_params=_cparams("parallel"),
        name="nsa_compress",
    )(xk, xv, pe, w1, b1, w2k, w2vt, knw, c, sa, sb)


def _flash_steps(qats, k_tiles, vt_tiles, states, mask=None):
    n = len(qats)
    ahead = 2
    scores = [_dot(k_tiles[i], qats[i]) for i in range(min(ahead, n))]
    out = []
    for i in range(n):
        m, acc = states[i]
        s = scores[i]
        if mask is not None:
            s = jnp.where(mask, s, -MASK_BIG)
        m_new = jnp.maximum(m, jnp.max(s, axis=0, keepdims=True))
        p = jnp.exp2(s - m_new).astype(BF16)
        out.append((m_new, jnp.exp2(m - m_new) * acc + _dot(vt_tiles[i], p)))
        if i + ahead < n:
            scores.append(_dot(k_tiles[i + ahead], qats[i + ahead]))
    return out


def _nsa_attn_kernel(qt_ref, kc_ref, vct_ref, ks_ref, vst_ref, kw_ref, vwt_ref, gt_ref, ovl_ref, o_ref,
                     ksa_ref, kwa_ref, qat_ref, part_ref):
    S = ks_ref.shape[0]
    ncp = kc_ref.shape[2]
    nblk = S // SEL_BLOCK
    G, HP, dh, tq = NSA_G, NSA_HPG, HEAD_DIM, NSA_TQ
    cols = HP * tq
    qi = pl.program_id(1)
    q0 = qi * tq

    @pl.when(qi == 0)
    def _():
        rblk = lax.broadcasted_iota(jnp.int32, (S, dh), 0) >> SEL_SHIFT
        lane = lax.broadcasted_iota(jnp.int32, (S, dh), 1)
        onehot = jnp.where(rblk == lane, 1.0, 0.0).astype(BF16)
        zeros = jnp.zeros((S, dh), BF16)
        for g in range(G):
            ksa_ref[g, :, 0:dh] = ks_ref[:, dh * g:dh * (g + 1)]
            ksa_ref[g, :, dh:2 * dh] = onehot
            kwa_ref[g, :, 0:dh] = kw_ref[:, dh * g:dh * (g + 1)]
            kwa_ref[g, :, dh:2 * dh] = zeros

    t_cols = q0 + (lax.broadcasted_iota(jnp.int32, (1, cols), 1) & (tq - 1))
    t_q = t_cols[:, 0:tq]
    k_loc = lax.broadcasted_iota(jnp.int32, (tq, 1), 0)

    cmp_end = lax.broadcasted_iota(jnp.int32, (ncp, 1), 0) * CMP_STRIDE + (CMP_BLOCK - 1)
    cmask = cmp_end <= t_cols
    jb = lax.broadcasted_iota(jnp.int32, (nblk, tq), 0)
    jb_col = lax.broadcasted_iota(jnp.int32, (nblk, 1), 0)
    tblk = t_q >> SEL_SHIFT
    forced = (jb == 0) | (jb == tblk) | (jb == tblk - 1)
    gt = gt_ref[0]

    def gate_row(g, branch):
        return jnp.concatenate([gt[3 * (g * HP + h) + branch:3 * (g * HP + h) + branch + 1] for h in range(HP)],
                               axis=1)

    init = (jnp.full((1, cols), M_INIT, F32), jnp.zeros((dh + V_PAD, cols), F32))
    ones_rows = jnp.where(lax.broadcasted_iota(jnp.int32, (V_PAD, tq), 0) == 0, 1.0, 0.0).astype(BF16)

    def v_aug(v_ref, kt, g):
        return jnp.concatenate([v_ref[kt, dh * g:dh * (g + 1), :], ones_rows], axis=0)

    def normalised(acc):
        return acc[0:dh] * (1.0 / acc[dh:dh + 1])

    causal = (q0 + k_loc) <= t_cols
    n_back = (WINDOW + tq - 1) // tq
    back = []
    c_loc = t_cols - q0
    for dk in range(1, n_back + 1):
        kt = qi - dk
        far = jnp.where(kt < 0, 2 * WINDOW + S, 0)
        back.append((jnp.maximum(kt, 0), (c_loc + (dk * tq - WINDOW + far)) < k_loc))

    grp = range(G)
    for g in grp:
        qat_ref[g, dh + nblk:2 * dh, :] = jnp.zeros((dh - nblk, cols), BF16)
        for h in range(HP):
            hd = g * HP + h
            qat_ref[g, 0:dh, h * tq:(h + 1) * tq] = qt_ref[0, dh * hd:dh * (hd + 1), :]

    sc = [_dot(kc_ref[0, g], qat_ref[g, 0:dh, :]) for g in grp]
    pc = []
    for g in grp:
        s = jnp.where(cmask, sc[g], -MASK_BIG)
        m = jnp.max(s, axis=0, keepdims=True)
        p = jnp.where(cmask, jnp.exp2(s - m), 0.0)
        l = jnp.sum(p, axis=0, keepdims=True)
        pc.append(p * jnp.where(l > 0.0, 1.0 / l, 0.0))
    oc = [_dot(vct_ref[0, g], pc[g].astype(BF16)) for g in grp]
    ovl = ovl_ref[...]
    imp = []
    for g in grp:
        psum = pc[g][:, 0:tq]
        for h in range(1, HP):
            psum = psum + pc[g][:, h * tq:(h + 1) * tq]
        p1, p2, p3 = _split3(psum)
        imp.append((_dot(ovl, p1) + _dot(ovl, p2) + _dot(ovl, p3))[0:nblk])
    for g in grp:
        key = jnp.where(forced, KEY_FORCED, jnp.where(jb > tblk, -1, pltpu.bitcast(imp[g], jnp.int32)))
        cnt = jnp.zeros((nblk, tq), jnp.int32)
        for j in range(nblk):
            rj = key[j:j + 1, :] + jnp.where(jb_col > j, 1, 0)
            cnt = cnt + jnp.where(rj > key, 1, 0)
        selneg = jnp.where(cnt < SEL_TOPK, 0.0, -MASK_BIG).astype(BF16)
        for h in range(HP):
            qat_ref[g, dh:dh + nblk, h * tq:(h + 1) * tq] = selneg
    qats = [qat_ref[g] for g in grp]

    states = _flash_steps(qats, [kwa_ref[g, pl.ds(q0, tq), :] for g in grp],
                          [v_aug(vwt_ref, qi, g) for g in grp], [init] * G, causal)
    for kt, inside in back:
        k0 = pl.multiple_of(kt * tq, tq)
        states = _flash_steps(qats, [kwa_ref[g, pl.ds(k0, tq), :] for g in grp],
                              [v_aug(vwt_ref, kt, g) for g in grp], states, inside)
    for g in grp:
        part_ref[g] = gate_row(g, 0) * oc[g] + gate_row(g, 2) * normalised(states[g][1])

    sel_state = _flash_steps(qats, [ksa_ref[g, pl.ds(q0, tq), :] for g in grp],
                             [v_aug(vst_ref, qi, g) for g in grp], [init] * G, causal)

    def sel_body(kt, states):
        k0 = pl.multiple_of(kt * tq, tq)
        return tuple(_flash_steps([qat_ref[g] for g in grp], [ksa_ref[g, pl.ds(k0, tq), :] for g in grp],
                                  [v_aug(vst_ref, kt, g) for g in grp], states))

    sel_state = lax.fori_loop(0, qi, sel_body, tuple(sel_state))

    for g in range(G):
        og = part_ref[g] + gate_row(g, 1) * normalised(sel_state[g][1])
        og_t = jnp.concatenate([og[:, h * tq:(h + 1) * tq] for h in range(HP)], axis=0)
        o_ref[:, HP * dh * g:HP * dh * (g + 1)] = og_t.T.astype(BF16)


def _nsa_attn(qt, kc, vct, ks, vst, kw, vwt, gt, ovl, B, S):
    T = B * S
    tq = NSA_TQ
    nq = S // tq
    G, dh = NSA_G, HEAD_DIM
    ncp = kc.shape[2]
    qspec = pl.BlockSpec((1, NSA_Q_WIDTH, tq), lambda b, i: (b * nq + i, 0, 0))
    gspec = pl.BlockSpec((1, NSA_GATE_PAD, tq), lambda b, i: (b * nq + i, 0, 0))
    kcspec = pl.BlockSpec((1, G, ncp, dh), lambda b, i: (b, 0, 0, 0))
    vcspec = pl.BlockSpec((1, G, dh, ncp), lambda b, i: (b, 0, 0, 0))
    kspec = pl.BlockSpec((S, NSA_KV_WIDTH), lambda b, i: (b, 0))
    vspec = pl.BlockSpec((nq, NSA_KV_WIDTH, tq), lambda b, i: (b, 0, 0))
    ovspec = pl.BlockSpec(ovl.shape, lambda b, i: (0, 0))
    return pl.pallas_call(
        _nsa_attn_kernel,
        grid=(B, nq),
        in_specs=[qspec, kcspec, vcspec, kspec, vspec, kspec, vspec, gspec, ovspec],
        out_specs=pl.BlockSpec((tq, NSA_Q_WIDTH), lambda b, i: (b * nq + i, 0)),
        out_shape=jax.ShapeDtypeStruct((T, NSA_Q_WIDTH), BF16),
        scratch_shapes=[pltpu.VMEM((G, S, 2 * dh), BF16), pltpu.VMEM((G, S, 2 * dh), BF16),
                        pltpu.VMEM((G, 2 * dh, NSA_HPG * tq), BF16),
                        pltpu.VMEM((G, dh, NSA_HPG * tq), F32)],
        compiler_params=_cparams("arbitrary", "arbitrary"),
        name="nsa_attn",
    )(qt, kc, vct, ks, vst, kw, vwt, gt, ovl)


def _ffn_kernel(x_ref, y_ref, wo_ref, nw_ref, wg_ref, wu_ref, wd_ref, o_ref):
    h1 = x_ref[...] + _dot(y_ref[...], wo_ref[...])
    xn = _rms_rows(h1, nw_ref[...]).astype(BF16)
    hidden = wg_ref.shape[1]
    cuts = list(range(0, hidden, FFN_PIECE)) + [hidden]
    pieces = list(zip(cuts[:-1], cuts[1:]))
    gate_up = lambda a, b: (_dot(xn, wg_ref[:, a:b]), _dot(xn, wu_ref[:, a:b]))
    nxt = gate_up(*pieces[0])
    out = h1
    for i, (a, b) in enumerate(pieces):
        g, u = nxt
        if i + 1 < len(pieces):
            nxt = gate_up(*pieces[i + 1])
        out = out + _dot((_silu(g) * u).astype(BF16), wd_ref[a:b, :])
    o_ref[...] = out


def _ffn(h, y, wo, nw, wg, wu, wd, tm=512):
    T = h.shape[0]
    H = wg.shape[1]
    K = y.shape[1]
    row = lambda width: pl.BlockSpec((tm, width), lambda i: (i, 0))
    resident = lambda a: pl.BlockSpec(a.shape, lambda i: (0,) * a.ndim, pipeline_mode=pl.Buffered(1))
    return pl.pallas_call(
        _ffn_kernel,
        grid=(T // tm,),
        in_specs=[row(D_MODEL), row(K), resident(wo), resident(nw), resident(wg), resident(wu), resident(wd)],
        out_specs=row(D_MODEL),
        out_shape=jax.ShapeDtypeStruct((T, D_MODEL), F32),
        compiler_params=_cparams("parallel"),
        name="ffn",
    )(h, y, wo, nw, wg, wu, wd)


def _ssm_inproj_kernel(x_ref, nw_ref, w_ref, cw_ref, cb_ref, dtb_ref, alog_ref, tri_ref,
                       zs_ref, xbc_ref, cum_ref, cumt_ref, dtt_ref, ext_ref, *, tiles_per_seq):
    tm = x_ref.shape[0]
    halo, Q = SSM_HALO, SSM_CHUNK
    chunk = 256
    xn = _rms_rows(x_ref[...], nw_ref[...]).astype(BF16)

    @pl.when(pl.program_id(0) % tiles_per_seq == 0)
    def _():
        ext_ref[...] = jnp.zeros_like(ext_ref)

    row = lax.broadcasted_iota(jnp.int32, (halo, chunk), 0)

    def gate_out(lo, y):
        zs_ref[:, lo:lo + chunk] = _silu(y).astype(BF16)

    def conv_out(lo, x):
        prev = ext_ref[:, lo:lo + chunk]
        acc = cb_ref[:, lo:lo + chunk] + cw_ref[SSM_CONV - 1:SSM_CONV, lo:lo + chunk] * x
        for k in range(SSM_CONV - 1):
            sh = SSM_CONV - 1 - k
            r = pltpu.roll(x, sh, 0)
            top = jnp.where(row < sh, pltpu.roll(prev, sh, 0), r[0:halo])
            acc = acc + cw_ref[k:k + 1, lo:lo + chunk] * jnp.concatenate([top, r[halo:]], axis=0)
        ext_ref[:, lo:lo + chunk] = x[tm - halo:tm]
        xbc_ref[:, lo:lo + chunk] = _silu(acc).astype(BF16)

    gate_jobs = [(lo, lo, gate_out) for lo in range(0, SSM_D_INNER, chunk)]
    conv_jobs = [(SSM_D_INNER + lo, lo, conv_out) for lo in range(0, SSM_CONV_DIM, chunk)]
    jobs = []
    for j in range(max(len(gate_jobs), len(conv_jobs))):
        jobs += conv_jobs[j:j + 1] + gate_jobs[j:j + 1]
    proj = lambda j: _dot(xn, w_ref[:, jobs[j][0]:jobs[j][0] + chunk])
    nxt = proj(0)
    for j in range(len(jobs)):
        cur = nxt
        if j + 1 < len(jobs):
            nxt = proj(j + 1)
        jobs[j][2](jobs[j][1], cur)

    base = SSM_D_INNER + SSM_CONV_DIM
    dtl = _dot(xn, w_ref[:, base:base + SSM_DT_PAD]) + dtb_ref[...]
    dt = jnp.maximum(dtl, 0.0) + jnp.log(1.0 + jnp.exp(-jnp.abs(dtl)))
    a = dt * (-jnp.exp(alog_ref[...]))
    tri = tri_ref[...]
    for c in range(tm // Q):
        a1, a2, a3 = _split3(a[Q * c:Q * (c + 1)])
        cum = _dot(tri, a1) + _dot(tri, a2) + _dot(tri, a3)
        cum_ref[Q * c:Q * (c + 1), :] = cum
        cumt_ref[c] = cum.T
        dtt_ref[c] = dt[Q * c:Q * (c + 1)].T


def _ssm_inproj(h, nw, w, cw, cb, dtb, alog, tri, S, tm=256):
    T = h.shape[0]
    Q = SSM_CHUNK
    row = lambda width: pl.BlockSpec((tm, width), lambda i: (i, 0))
    full = lambda a: pl.BlockSpec(a.shape, lambda i: (0,) * a.ndim)
    tile = pl.BlockSpec((tm // Q, SSM_DT_PAD, Q), lambda i: (i, 0, 0))
    tshape = jax.ShapeDtypeStruct((T // Q, SSM_DT_PAD, Q), F32)
    return pl.pallas_call(
        functools.partial(_ssm_inproj_kernel, tiles_per_seq=S // tm),
        grid=(T // tm,),
        in_specs=[row(D_MODEL), full(nw), full(w), full(cw), full(cb), full(dtb), full(alog), full(tri)],
        out_specs=[row(SSM_D_INNER), row(SSM_CONV_DIM), row(SSM_DT_PAD), tile, tile],
        out_shape=[jax.ShapeDtypeStruct((T, SSM_D_INNER), BF16), jax.ShapeDtypeStruct((T, SSM_CONV_DIM), BF16),
                   jax.ShapeDtypeStruct((T, SSM_DT_PAD), F32), tshape, tshape],
        scratch_shapes=[pltpu.VMEM((SSM_HALO, SSM_CONV_DIM), F32)],
        compiler_params=_cparams("arbitrary"),
        name="ssm_inproj",
    )(h, nw, w, cw, cb, dtb, alog, tri)


def _ssd_kernel(xbc_ref, zs_ref, cum_ref, cumt_ref, dtt_ref, dsk_ref, nw_ref, y_ref, state_ref):
    Q, P, N, G, HPG = SSM_CHUNK, SSM_P, SSM_N, SSM_G, SSM_HPG
    gw = SSM_D_INNER // G

    @pl.when(pl.program_id(1) == 0)
    def _():
        state_ref[...] = jnp.zeros_like(state_ref)

    cum = cum_ref[...]
    cum_t = cumt_ref[0]
    dt_t = dtt_ref[0]
    row_i = lax.broadcasted_iota(jnp.int32, (Q, Q), 0)
    col_i = lax.broadcasted_iota(jnp.int32, (Q, Q), 1)
    tril = row_i >= col_i
    eye = row_i == col_i
    b_off = SSM_D_INNER
    c_off = SSM_D_INNER + G * N

    def elementwise(g):
        cg = xbc_ref[:, c_off + N * g:c_off + N * (g + 1)]
        bg = xbc_ref[:, b_off + N * g:b_off + N * (g + 1)]
        cb = _dot_nt(cg, bg)
        cg_f = cg.astype(F32)
        bg_t = bg.astype(F32).T
        st_g = state_ref[g]
        st_b = st_g.astype(BF16)
        lhs, wgt, rhs, keep = [], [], [], []
        for hh in range(HPG):
            h = g * HPG + hh
            cum_b = jnp.broadcast_to(cum[:, h:h + 1], (Q, Q))
            cum_row = cum_t[h:h + 1, :]
            dt_row = dt_t[h:h + 1, :]
            cum_last = cum_row[:, Q - 1:Q]
            mm = cb * jnp.exp(jnp.where(tril, cum_b - cum_row, -jnp.inf)) * dt_row
            mm = jnp.where(eye, mm + dsk_ref[:, h:h + 1], mm)
            lhs += [mm.astype(BF16), (jnp.exp(cum_b) * cg_f).astype(BF16)]
            wgt.append((bg_t * (dt_row * jnp.exp(cum_last - cum_row))).astype(BF16))
            keep.append(jnp.broadcast_to(jnp.exp(cum_last), (1, P)))
        low = lax.broadcasted_iota(jnp.int32, (Q, 2 * P), 1) < P
        zero = jnp.zeros((Q, 2 * P), BF16)
        for pr in range(HPG // 2):
            xp = xbc_ref[:, 2 * P * (g * HPG // 2 + pr):2 * P * (g * HPG // 2 + pr + 1)]
            sp = st_b[:, 2 * P * pr:2 * P * (pr + 1)]
            rhs.append((jnp.where(low, xp, zero), jnp.where(low, sp, zero),
                        jnp.where(low, zero, xp), jnp.where(low, zero, sp)))
        return lhs, wgt, rhs, st_g * jnp.concatenate(keep, axis=1)

    def matmuls(g, ops):
        lhs, wgt, rhs, kept = ops
        ys, upd = [], []
        for pr in range(HPG // 2):
            x_lo, s_lo, x_hi, s_hi = rhs[pr]
            ys.append(_dot(jnp.concatenate(lhs[4 * pr:4 * pr + 4], axis=1),
                           jnp.concatenate([x_lo, s_lo, x_hi, s_hi], axis=0)))
            upd.append(_dot(jnp.concatenate(wgt[2 * pr:2 * pr + 2], axis=1),
                            jnp.concatenate([x_lo, x_hi], axis=0)))
        state_ref[g] = kept + jnp.concatenate(upd, axis=1)
        yg = jnp.concatenate(ys, axis=1) * zs_ref[:, gw * g:gw * (g + 1)].astype(F32)
        yg = yg * lax.rsqrt(jnp.mean(yg * yg, axis=-1, keepdims=True) + EPS)
        y_ref[:, gw * g:gw * (g + 1)] = (yg * nw_ref[:, gw * g:gw * (g + 1)]).astype(BF16)

    ops = elementwise(0)
    for g in range(G):
        nxt = elementwise(g + 1) if g + 1 < G else None
        matmuls(g, ops)
        ops = nxt


def _ssd(xbc, zs, cum, cumt, dtt, dsk, nw, B, S):
    Q = SSM_CHUNK
    nch = S // Q
    row = lambda width: pl.BlockSpec((Q, width), lambda b, c: (b * nch + c, 0))
    full = lambda a: pl.BlockSpec(a.shape, lambda b, c: (0,) * a.ndim)
    tile = pl.BlockSpec((1, SSM_DT_PAD, Q), lambda b, c: (b * nch + c, 0, 0))
    return pl.pallas_call(
        _ssd_kernel,
        grid=(B, nch),
        in_specs=[row(SSM_CONV_DIM), row(SSM_D_INNER), row(SSM_DT_PAD), tile, tile, full(dsk), full(nw)],
        out_specs=row(SSM_D_INNER),
        out_shape=jax.ShapeDtypeStruct((B * S, SSM_D_INNER), BF16),
        scratch_shapes=[pltpu.VMEM((SSM_G, SSM_N, SSM_HPG * SSM_P), F32)],
        compiler_params=_cparams("arbitrary", "arbitrary"),
        name="ssd_scan",
    )(xbc, zs, cum, cumt, dtt, dsk, nw)


def _block_diag_mean():
    i = np.arange(LANES)
    return jnp.asarray((i[:, None] // HEAD_DIM == i[None, :] // HEAD_DIM) / HEAD_DIM, BF16)


def _overlap_t(S):
    nc = (S - CMP_BLOCK) // CMP_STRIDE + 1
    ncp = S // CMP_STRIDE
    nblk = S // SEL_BLOCK
    starts = np.arange(ncp) * CMP_STRIDE
    js = np.arange(nblk)[:, None] * SEL_BLOCK
    ov = (starts[None, :] < js + SEL_BLOCK) & (starts[None, :] + CMP_BLOCK > js) & (np.arange(ncp)[None, :] < nc)
    out = np.zeros((LANES, ncp), np.float32)
    out[:nblk] = ov
    return jnp.asarray(out, BF16)


def _pad_cols(w, width):
    return jnp.pad(w, ((0, 0), (0, width - w.shape[1])))


def _nsa_layer(h, tabs, tabs_t, tabs_c, B, S, nw, w_in, q_norm, k_norm, cmp_pe, cmp_w1, cmp_b1, cmp_w2, w_out):
    G, dh, kvw = NSA_G, HEAD_DIM, NSA_KV_WIDTH
    ncp = S // CMP_STRIDE
    cut = lambda i: w_in[:, NSA_Q_WIDTH + i * kvw:NSA_Q_WIDTH + (i + 1) * kvw]
    wqt = w_in[:, :NSA_Q_WIDTH].T.astype(BF16)
    wk = jnp.concatenate([cut(0), cut(1), cut(2), cut(4)], axis=1).astype(BF16)
    wvt = jnp.concatenate([cut(3), cut(5)], axis=1).T.astype(BF16)
    wgt = _pad_cols(w_in[:, NSA_Q_WIDTH + 6 * kvw:], NSA_GATE_PAD).T.astype(BF16)
    qn = jnp.broadcast_to(q_norm[:, None], (dh, NSA_TQ))
    ksn = jnp.tile(k_norm[1], G)[None, :]
    kwn = jnp.tile(k_norm[2], G)[None, :]
    qt, kc_raw, vc_raw, ks, kw, vst, vwt, gt = _nsa_inproj(h, nw[None, :], wqt, wk, wvt, wgt, tabs, tabs_t,
                                                           qn, ksn, kwn, _block_diag_mean(), S)
    half = CMP_STRIDE * dh
    pe = cmp_pe.reshape(2, 2, half)
    w1 = cmp_w1.reshape(2, 2, half, CMP_HIDDEN).astype(BF16)
    b1 = cmp_b1[:, None, :]
    w2k = _pad_cols(cmp_w2[0], LANES).astype(BF16)
    w2vt = cmp_w2[1].T.astype(BF16)
    knw = _pad_cols(k_norm[0][None, :], LANES)
    kc, vct = _nsa_compress(kc_raw, vc_raw, pe, w1, b1, w2k, w2vt, knw, tabs_c)

    o = _nsa_attn(qt, kc, vct, ks, vst, kw, vwt, gt, _overlap_t(S), B, S)
    return o, w_out.astype(BF16)


def _ssd_layer(h, B, S, nw, w_in, conv_w, conv_b, dt_bias, a_log, d_skip, norm_w, w_out):
    w = _pad_cols(w_in, SSM_IN_PAD).astype(BF16)
    pad1 = lambda v: _pad_cols(v[None, :], SSM_DT_PAD)
    tri = jnp.asarray(np.tril(np.ones((SSM_CHUNK, SSM_CHUNK), np.float32)), BF16)
    zs, xbc, cum, cumt, dtt = _ssm_inproj(h, nw[None, :], w, conv_w, conv_b[None, :], pad1(dt_bias), pad1(a_log),
                                          tri, S)
    y = _ssd(xbc, zs, cum, cumt, dtt, pad1(d_skip), norm_w[None, :], B, S)
    return y, w_out.astype(BF16)


def kernel(x, positions, mix_norm_w, ffn_norm_w, ffn_w_gate, ffn_w_up, ffn_w_down, nsa_w_in, nsa_q_norm, nsa_k_norm, nsa_cmp_pe, nsa_cmp_w1, nsa_cmp_b1, nsa_cmp_w2, nsa_w_out, ssm_w_in, ssm_conv_w, ssm_conv_b, ssm_dt_bias, ssm_a_log, ssm_d, ssm_norm_w, ssm_w_out):
    B, S, D = x.shape
    T = B * S
    h = x.reshape(T, D)
    ncp = S // CMP_STRIDE
    tabs = _rope_tables(positions.reshape(T, 1), 1024)
    tabs_t = _rope_tables_t(positions.reshape(1, T), 2048)
    pos_c = jnp.pad(positions[:, CMP_BLOCK - 1::CMP_STRIDE], ((0, 0), (0, 1)))[:, :ncp]
    tabs_c = _rope_tables(pos_c.reshape(B * ncp, 1), ncp)
    for i in range(DEPTH):
        j = i // 2
        if i % 2 == 0:
            y, wo = _nsa_layer(h, tabs, tabs_t, tabs_c, B, S, mix_norm_w[i], nsa_w_in[j], nsa_q_norm[j],
                               nsa_k_norm[j], nsa_cmp_pe[j], nsa_cmp_w1[j], nsa_cmp_b1[j], nsa_cmp_w2[j],
                               nsa_w_out[j])
        else:
            y, wo = _ssd_layer(h, B, S, mix_norm_w[i], ssm_w_in[j], ssm_conv_w[j], ssm_conv_b[j], ssm_dt_bias[j],
                               ssm_a_log[j], ssm_d[j], ssm_norm_w[j], ssm_w_out[j])
        h = _ffn(h, y, wo, ffn_norm_w[i][None, :], ffn_w_gate[i].astype(BF16), ffn_w_up[i].astype(BF16),
                 ffn_w_down[i].astype(BF16))
    return h.reshape(B, S, D)
```

```python
import functools
import math

import numpy as np
import jax
import jax.numpy as jnp
from jax import lax
from jax.experimental import pallas as pl
from jax.experimental.pallas import tpu as pltpu

F32 = jnp.float32
BF16 = jnp.bfloat16

D_MODEL = 1024
DEPTH = 4
EPS = 1e-6

NSA_HEADS = 16
NSA_G = 4
NSA_HPG = NSA_HEADS // NSA_G
HEAD_DIM = 64
CMP_BLOCK = 32
CMP_STRIDE = 16
CMP_HIDDEN = 256
SEL_BLOCK = 64
SEL_SHIFT = 6
SEL_TOPK = 8
WINDOW = 512
ROPE_THETA = 500000.0
ROPE_DIM = HEAD_DIM // 4
ROPE_HALF = ROPE_DIM // 2
NSA_Q_WIDTH = NSA_HEADS * HEAD_DIM
NSA_KV_WIDTH = NSA_G * HEAD_DIM
NSA_GATE_PAD = 128
NSA_TQ = 256
V_PAD = 16
QSCALE = HEAD_DIM ** -0.5 * math.log2(math.e)

SSM_D_INNER = 2 * D_MODEL
SSM_P = 64
SSM_HEADS = SSM_D_INNER // SSM_P
SSM_G = 4
SSM_HPG = SSM_HEADS // SSM_G
SSM_N = 128
SSM_CONV = 4
SSM_CHUNK = 128
SSM_CONV_DIM = SSM_D_INNER + 2 * SSM_G * SSM_N
SSM_DT_PAD = 128
SSM_HALO = 8
SSM_IN_PAD = SSM_D_INNER + SSM_CONV_DIM + SSM_DT_PAD

FFN_HIDDEN = -(-8 * D_MODEL // (3 * 256)) * 256
FFN_PIECE = 768

LANES = 128
VMEM_LIMIT_BYTES = 52 * 1024 * 1024

MASK_BIG = 1e30
KEY_FORCED = 0x7F000000
M_INIT = -3e38

_NT = (((1,), (1,)), ((), ()))


def _cparams(*sem, flags=None):
    return pltpu.CompilerParams(dimension_semantics=sem, vmem_limit_bytes=VMEM_LIMIT_BYTES, flags=flags)


def _dot(a, b):
    return jnp.dot(a, b, preferred_element_type=F32)


def _dot_nt(a, b):
    return lax.dot_general(a, b, _NT, preferred_element_type=F32)


def _split3(x):
    a = x.astype(BF16)
    r = x - a.astype(F32)
    b = r.astype(BF16)
    c = (r - b.astype(F32)).astype(BF16)
    return a, b, c


def _rms_rows(x, w):
    return x * lax.rsqrt(jnp.mean(x * x, axis=-1, keepdims=True) + EPS) * w


def _silu(x):
    h = 0.5 * x
    return h + h * jnp.tanh(h)


def _rope_table_kernel(pos_ref, c_ref, sa_ref, sb_ref):
    pos = pos_ref[...].astype(F32)
    lane = lax.broadcasted_iota(jnp.int32, (1, LANES), 1)
    d = lane & (HEAD_DIM - 1)
    f = d & (ROPE_HALF - 1)
    inv = jnp.zeros((1, LANES), F32)
    for i in range(ROPE_HALF):
        inv = jnp.where(f == i, float(np.power(np.float32(ROPE_THETA), np.float32(-i / ROPE_HALF))), inv)
    ang = pos * inv
    cos, sin = jnp.cos(ang), jnp.sin(ang)
    c_ref[...] = jnp.where(d < ROPE_DIM, cos, 1.0)
    sa_ref[...] = jnp.where(d < ROPE_HALF, -sin, 0.0)
    sb_ref[...] = jnp.where((d >= ROPE_HALF) & (d < ROPE_DIM), sin, 0.0)


def _rope_tables(pos_col, tm):
    n = pos_col.shape[0]
    out = jax.ShapeDtypeStruct((n, LANES), F32)
    spec = pl.BlockSpec((tm, LANES), lambda i: (i, 0))
    return pl.pallas_call(
        _rope_table_kernel,
        grid=(n // tm,),
        in_specs=[pl.BlockSpec((tm, 1), lambda i: (i, 0))],
        out_specs=[spec, spec, spec],
        out_shape=[out, out, out],
        compiler_params=_cparams("parallel"),
        name="rope_tables",
    )(pos_col)


def _rope_table_t_kernel(pos_ref, c_ref, s_ref):
    pos = pos_ref[...].astype(F32)
    f = lax.broadcasted_iota(jnp.int32, (ROPE_HALF, 1), 0)
    inv = jnp.zeros((ROPE_HALF, 1), F32)
    for i in range(ROPE_HALF):
        inv = jnp.where(f == i, float(np.power(np.float32(ROPE_THETA), np.float32(-i / ROPE_HALF))), inv)
    ang = inv * pos
    c_ref[...] = jnp.cos(ang)
    s_ref[...] = jnp.sin(ang)


def _rope_tables_t(pos_row, tm):
    n = pos_row.shape[1]
    out = jax.ShapeDtypeStruct((ROPE_HALF, n), F32)
    spec = pl.BlockSpec((ROPE_HALF, tm), lambda i: (0, i))
    return pl.pallas_call(
        _rope_table_t_kernel,
        grid=(n // tm,),
        in_specs=[pl.BlockSpec((1, tm), lambda i: (0, i))],
        out_specs=[spec, spec],
        out_shape=[out, out],
        compiler_params=_cparams("parallel"),
        name="rope_tables_t",
    )(pos_row)


def _rope_lanes(x, c, sa, sb):
    return x * c + pltpu.roll(x, LANES - ROPE_HALF, 1) * sa + pltpu.roll(x, ROPE_HALF, 1) * sb


def _head_norm_rope(y, w, bd, c, sa, sb, scale):
    outs = []
    for j in range(y.shape[1] // LANES):
        yc = y[:, LANES * j:LANES * (j + 1)]
        sq = yc * yc
        hi = sq.astype(BF16)
        lo = (sq - hi.astype(F32)).astype(BF16)
        ms = _dot(hi, bd) + _dot(lo, bd)
        yn = yc * lax.rsqrt(ms + EPS) * w[:, LANES * j:LANES * (j + 1)]
        outs.append(_rope_lanes(yn, c, sa, sb) * scale)
    return jnp.concatenate(outs, axis=1)


def _nsa_inproj_kernel(x_ref, nw_ref, wqt_ref, wk_ref, wvt_ref, wgt_ref, c_ref, sa_ref, sb_ref, ct_ref, st_ref,
                       qn_ref, ksn_ref, kwn_ref, bd_ref,
                       qt_ref, kc_ref, vc_ref, ks_ref, kw_ref, vst_ref, vwt_ref, gt_ref, raw_ref):
    xn = _rms_rows(x_ref[...], nw_ref[...]).astype(BF16)
    c, sa, sb, bd = c_ref[...], sa_ref[...], sb_ref[...], bd_ref[...]
    kvw, dh, tq = NSA_KV_WIDTH, HEAD_DIM, NSA_TQ
    nrow = x_ref.shape[0] // CMP_STRIDE

    qn = qn_ref[...]

    def raw_out(raw):
        for j in range(2 * kvw // LANES):
            raw_ref[j] = raw[:, LANES * j:LANES * (j + 1)]
        for which, out_ref in enumerate((kc_ref, vc_ref)):
            for g in range(NSA_G):
                j, off = divmod(which * kvw + dh * g, LANES)
                for l in range(CMP_STRIDE):
                    rows = raw_ref[j, pl.ds(l, nrow, stride=CMP_STRIDE), :]
                    out_ref[0, g, :, dh * l:dh * (l + 1)] = rows[:, off:off + dh]

    def key_out(out_ref, gain_ref):
        def write(y):
            out_ref[...] = _head_norm_rope(y, gain_ref[...], bd, c, sa, sb, 1.0).astype(BF16)
        return write

    def q_out(ch):
        def write(yt):
            cos, sin = ct_ref[:, ch * tq:(ch + 1) * tq], st_ref[:, ch * tq:(ch + 1) * tq]
            for hd in range(NSA_HEADS):
                yh = yt[dh * hd:dh * (hd + 1)]
                yn = yh * lax.rsqrt(jnp.mean(yh * yh, axis=0, keepdims=True) + EPS) * qn
                x1, x2 = yn[0:ROPE_HALF], yn[ROPE_HALF:ROPE_DIM]
                rot = jnp.concatenate([x1 * cos - x2 * sin, x2 * cos + x1 * sin, yn[ROPE_DIM:]], axis=0)
                qt_ref[ch, dh * hd:dh * (hd + 1), :] = (rot * QSCALE).astype(BF16)
        return write

    def v_out(ch):
        def write(vt):
            vst_ref[ch] = vt[0:kvw].astype(BF16)
            vwt_ref[ch] = vt[kvw:2 * kvw].astype(BF16)
        return write

    def g_out(ch):
        def write(gl):
            gt_ref[ch] = 1.0 / (1.0 + jnp.exp(-gl))
        return write

    jobs = [(lambda: _dot(xn, wk_ref[:, 0:2 * kvw]), raw_out),
            (lambda: _dot(xn, wk_ref[:, 2 * kvw:3 * kvw]), key_out(ks_ref, ksn_ref)),
            (lambda: _dot(xn, wk_ref[:, 3 * kvw:4 * kvw]), key_out(kw_ref, kwn_ref))]
    for ch in range(x_ref.shape[0] // tq):
        xc = xn[ch * tq:(ch + 1) * tq]
        jobs += [(functools.partial(_dot_nt, wqt_ref[...], xc), q_out(ch)),
                 (functools.partial(_dot_nt, wvt_ref[...], xc), v_out(ch)),
                 (functools.partial(_dot_nt, wgt_ref[...], xc), g_out(ch))]
    nxt = jobs[0][0]()
    for j, (_, epilogue) in enumerate(jobs):
        cur = nxt
        if j + 1 < len(jobs):
            nxt = jobs[j + 1][0]()
        epilogue(cur)


def _nsa_inproj(h, nw, wqt, wk, wvt, wgt, tabs, tabs_t, qn, ksn, kwn, bd, S, tm=512):
    T = h.shape[0]
    tq, kvw = NSA_TQ, NSA_KV_WIDTH
    per_seq = S // tm
    width = CMP_STRIDE * HEAD_DIM
    chunked = pl.BlockSpec((1, NSA_G, tm // CMP_STRIDE, width), lambda i: (i // per_seq, 0, i % per_seq, 0))
    cshape = jax.ShapeDtypeStruct((T // S, NSA_G, S // CMP_STRIDE, width), F32)
    row = lambda width: pl.BlockSpec((tm, width), lambda i: (i, 0))
    full = lambda a: pl.BlockSpec(a.shape, lambda i: (0,) * a.ndim)
    colt = pl.BlockSpec((ROPE_HALF, tm), lambda i: (0, i))
    tile = lambda ch: pl.BlockSpec((tm // tq, ch, tq), lambda i: (i, 0, 0))
    tshape = lambda ch, dt: jax.ShapeDtypeStruct((T // tq, ch, tq), dt)
    c, sa, sb = tabs
    ct, st = tabs_t
    return pl.pallas_call(
        _nsa_inproj_kernel,
        grid=(T // tm,),
        in_specs=[row(D_MODEL), full(nw), full(wqt), full(wk), full(wvt), full(wgt),
                  row(LANES), row(LANES), row(LANES), colt, colt,
                  full(qn), full(ksn), full(kwn), full(bd)],
        out_specs=[tile(NSA_Q_WIDTH), chunked, chunked, row(kvw), row(kvw), tile(kvw), tile(kvw),
                   tile(NSA_GATE_PAD)],
        out_shape=[tshape(NSA_Q_WIDTH, BF16), cshape, cshape, jax.ShapeDtypeStruct((T, kvw), BF16),
                   jax.ShapeDtypeStruct((T, kvw), BF16), tshape(kvw, BF16), tshape(kvw, BF16),
                   tshape(NSA_GATE_PAD, F32)],
        scratch_shapes=[pltpu.VMEM((2 * kvw // LANES, tm, LANES), F32)],
        compiler_params=_cparams("parallel"),
        name="nsa_inproj",
    )(h, nw, wqt, wk, wvt, wgt, c, sa, sb, ct, st, qn, ksn, kwn, bd)


def _nsa_compress_kernel(xk_ref, xv_ref, pe_ref, w1_ref, b1_ref, w2k_ref, w2vt_ref, knw_ref,
                         c_ref, sa_ref, sb_ref, kc_ref, vct_ref):
    ncp = xk_ref.shape[2]

    def hidden(x, which):
        lo = _dot((x + pe_ref[which, 0:1, :]).astype(BF16), w1_ref[which, 0])
        hi = _dot((x + pe_ref[which, 1:2, :]).astype(BF16), w1_ref[which, 1])
        return _silu(lo + pltpu.roll(hi, ncp - 1, 0) + b1_ref[which]).astype(BF16)

    for g in range(NSA_G):
        kc = _dot(hidden(xk_ref[0, g], 0), w2k_ref[...])
        ms = jnp.sum(kc * kc, axis=-1, keepdims=True) * (1.0 / HEAD_DIM)
        kn = kc * lax.rsqrt(ms + EPS) * knw_ref[...]
        kn = _rope_lanes(kn, c_ref[...], sa_ref[...], sb_ref[...])
        kc_ref[0, g] = kn[:, :HEAD_DIM].astype(BF16)
        vct_ref[0, g] = _dot_nt(w2vt_ref[...], hidden(xv_ref[0, g], 1)).astype(BF16)


def _nsa_compress(xk, xv, pe, w1, b1, w2k, w2vt, knw, tabs):
    B, G, ncp, width = xk.shape
    c, sa, sb = tabs
    xspec = pl.BlockSpec((1, G, ncp, width), lambda b: (b, 0, 0, 0))
    tspec = pl.BlockSpec((ncp, LANES), lambda b: (b, 0))
    full = lambda a: pl.BlockSpec(a.shape, lambda b: (0,) * a.ndim)
    return pl.pallas_call(
        _nsa_compress_kernel,
        grid=(B,),
        in_specs=[xspec, xspec, full(pe), full(w1), full(b1), full(w2k), full(w2vt), full(knw),
                  tspec, tspec, tspec],
        out_specs=[pl.BlockSpec((1, G, ncp, HEAD_DIM), lambda b: (b, 0, 0, 0)),
                   pl.BlockSpec((1, G, HEAD_DIM, ncp), lambda b: (b, 0, 0, 0))],
        out_shape=[jax.ShapeDtypeStruct((B, G, ncp, HEAD_DIM), BF16),
                   jax.ShapeDtypeStruct((B, G, HEAD_DIM, ncp), BF16)],
        compiler_params=_cparams("parallel"),
        name="nsa_compress",
    )(xk, xv, pe, w1, b1, w2k, w2vt, knw, c, sa, sb)


def _flash_steps(qats, k_tiles, vt_tiles, states, mask=None):
    n = len(qats)
    ahead = 2
    scores = [_dot(k_tiles[i], qats[i]) for i in range(min(ahead, n))]
    out = []
    for i in range(n):
        m, acc = states[i]
        s = scores[i]
        if mask is not None:
            s = jnp.where(mask, s, -MASK_BIG)
        m_new = jnp.maximum(m, jnp.max(s, axis=0, keepdims=True))
        p = jnp.exp2(s - m_new).astype(BF16)
        out.append((m_new, jnp.exp2(m - m_new) * acc + _dot(vt_tiles[i], p)))
        if i + ahead < n:
            scores.append(_dot(k_tiles[i + ahead], qats[i + ahead]))
    return out


def _nsa_attn_kernel(qt_ref, kc_ref, vct_ref, ks_ref, vst_ref, kw_ref, vwt_ref, gt_ref, ovl_ref, o_ref,
                     ksa_ref, kwa_ref, qat_ref, part_ref):
    S = ks_ref.shape[0]
    ncp = kc_ref.shape[2]
    nblk = S // SEL_BLOCK
    G, HP, dh, tq = NSA_G, NSA_HPG, HEAD_DIM, NSA_TQ
    cols = HP * tq
    qi = pl.program_id(1)
    q0 = qi * tq

    @pl.when(qi == 0)
    def _():
        rblk = lax.broadcasted_iota(jnp.int32, (S, dh), 0) >> SEL_SHIFT
        lane = lax.broadcasted_iota(jnp.int32, (S, dh), 1)
        onehot = jnp.where(rblk == lane, 1.0, 0.0).astype(BF16)
        zeros = jnp.zeros((S, dh), BF16)
        for g in range(G):
            ksa_ref[g, :, 0:dh] = ks_ref[:, dh * g:dh * (g + 1)]
            ksa_ref[g, :, dh:2 * dh] = onehot
            kwa_ref[g, :, 0:dh] = kw_ref[:, dh * g:dh * (g + 1)]
            kwa_ref[g, :, dh:2 * dh] = zeros

    t_cols = q0 + (lax.broadcasted_iota(jnp.int32, (1, cols), 1) & (tq - 1))
    t_q = t_cols[:, 0:tq]
    k_loc = lax.broadcasted_iota(jnp.int32, (tq, 1), 0)

    cmp_end = lax.broadcasted_iota(jnp.int32, (ncp, 1), 0) * CMP_STRIDE + (CMP_BLOCK - 1)
    cmask = cmp_end <= t_cols
    jb = lax.broadcasted_iota(jnp.int32, (nblk, tq), 0)
    jb_col = lax.broadcasted_iota(jnp.int32, (nblk, 1), 0)
    tblk = t_q >> SEL_SHIFT
    forced = (jb == 0) | (jb == tblk) | (jb == tblk - 1)
    gt = gt_ref[0]

    def gate_row(g, branch):
        return jnp.concatenate([gt[3 * (g * HP + h) + branch:3 * (g * HP + h) + branch + 1] for h in range(HP)],
                               axis=1)

    init = (jnp.full((1, cols), M_INIT, F32), jnp.zeros((dh + V_PAD, cols), F32))
    ones_rows = jnp.where(lax.broadcasted_iota(jnp.int32, (V_PAD, tq), 0) == 0, 1.0, 0.0).astype(BF16)

    def v_aug(v_ref, kt, g):
        return jnp.concatenate([v_ref[kt, dh * g:dh * (g + 1), :], ones_rows], axis=0)

    def normalised(acc):
        return acc[0:dh] * (1.0 / acc[dh:dh + 1])

    causal = (q0 + k_loc) <= t_cols
    n_back = (WINDOW + tq - 1) // tq
    back = []
    c_loc = t_cols - q0
    for dk in range(1, n_back + 1):
        kt = qi - dk
        far = jnp.where(kt < 0, 2 * WINDOW + S, 0)
        back.append((jnp.maximum(kt, 0), (c_loc + (dk * tq - WINDOW + far)) < k_loc))

    grp = range(G)
    for g in grp:
        qat_ref[g, dh + nblk:2 * dh, :] = jnp.zeros((dh - nblk, cols), BF16)
        for h in range(HP):
            hd = g * HP + h
            qat_ref[g, 0:dh, h * tq:(h + 1) * tq] = qt_ref[0, dh * hd:dh * (hd + 1), :]

    sc = [_dot(kc_ref[0, g], qat_ref[g, 0:dh, :]) for g in grp]
    pc = []
    for g in grp:
        s = jnp.where(cmask, sc[g], -MASK_BIG)
        m = jnp.max(s, axis=0, keepdims=True)
        p = jnp.where(cmask, jnp.exp2(s - m), 0.0)
        l = jnp.sum(p, axis=0, keepdims=True)
        pc.append(p * jnp.where(l > 0.0, 1.0 / l, 0.0))
    oc = [_dot(vct_ref[0, g], pc[g].astype(BF16)) for g in grp]
    ovl = ovl_ref[...]
    imp = []
    for g in grp:
        psum = pc[g][:, 0:tq]
        for h in range(1, HP):
            psum = psum + pc[g][:, h * tq:(h + 1) * tq]
        p1, p2, p3 = _split3(psum)
        imp.append((_dot(ovl, p1) + _dot(ovl, p2) + _dot(ovl, p3))[0:nblk])
    for g in grp:
        key = jnp.where(forced, KEY_FORCED, jnp.where(jb > tblk, -1, pltpu.bitcast(imp[g], jnp.int32)))
        cnt = jnp.zeros((nblk, tq), jnp.int32)
        for j in range(nblk):
            rj = key[j:j + 1, :] + jnp.where(jb_col > j, 1, 0)
            cnt = cnt + jnp.where(rj > key, 1, 0)
        selneg = jnp.where(cnt < SEL_TOPK, 0.0, -MASK_BIG).astype(BF16)
        for h in range(HP):
            qat_ref[g, dh:dh + nblk, h * tq:(h + 1) * tq] = selneg
    qats = [qat_ref[g] for g in grp]

    states = _flash_steps(qats, [kwa_ref[g, pl.ds(q0, tq), :] for g in grp],
                          [v_aug(vwt_ref, qi, g) for g in grp], [init] * G, causal)
    for kt, inside in back:
        k0 = pl.multiple_of(kt * tq, tq)
        states = _flash_steps(qats, [kwa_ref[g, pl.ds(k0, tq), :] for g in grp],
                              [v_aug(vwt_ref, kt, g) for g in grp], states, inside)
    for g in grp:
        part_ref[g] = gate_row(g, 0) * oc[g] + gate_row(g, 2) * normalised(states[g][1])

    sel_state = _flash_steps(qats, [ksa_ref[g, pl.ds(q0, tq), :] for g in grp],
                             [v_aug(vst_ref, qi, g) for g in grp], [init] * G, causal)

    def sel_body(kt, states):
        k0 = pl.multiple_of(kt * tq, tq)
        return tuple(_flash_steps([qat_ref[g] for g in grp], [ksa_ref[g, pl.ds(k0, tq), :] for g in grp],
                                  [v_aug(vst_ref, kt, g) for g in grp], states))

    sel_state = lax.fori_loop(0, qi, sel_body, tuple(sel_state))

    for g in range(G):
        og = part_ref[g] + gate_row(g, 1) * normalised(sel_state[g][1])
        og_t = jnp.concatenate([og[:, h * tq:(h + 1) * tq] for h in range(HP)], axis=0)
        o_ref[:, HP * dh * g:HP * dh * (g + 1)] = og_t.T.astype(BF16)


def _nsa_attn(qt, kc, vct, ks, vst, kw, vwt, gt, ovl, B, S):
    T = B * S
    tq = NSA_TQ
    nq = S // tq
    G, dh = NSA_G, HEAD_DIM
    ncp = kc.shape[2]
    qspec = pl.BlockSpec((1, NSA_Q_WIDTH, tq), lambda b, i: (b * nq + i, 0, 0))
    gspec = pl.BlockSpec((1, NSA_GATE_PAD, tq), lambda b, i: (b * nq + i, 0, 0))
    kcspec = pl.BlockSpec((1, G, ncp, dh), lambda b, i: (b, 0, 0, 0))
    vcspec = pl.BlockSpec((1, G, dh, ncp), lambda b, i: (b, 0, 0, 0))
    kspec = pl.BlockSpec((S, NSA_KV_WIDTH), lambda b, i: (b, 0))
    vspec = pl.BlockSpec((nq, NSA_KV_WIDTH, tq), lambda b, i: (b, 0, 0))
    ovspec = pl.BlockSpec(ovl.shape, lambda b, i: (0, 0))
    return pl.pallas_call(
        _nsa_attn_kernel,
        grid=(B, nq),
        in_specs=[qspec, kcspec, vcspec, kspec, vspec, kspec, vspec, gspec, ovspec],
        out_specs=pl.BlockSpec((tq, NSA_Q_WIDTH), lambda b, i: (b * nq + i, 0)),
        out_shape=jax.ShapeDtypeStruct((T, NSA_Q_WIDTH), BF16),
        scratch_shapes=[pltpu.VMEM((G, S, 2 * dh), BF16), pltpu.VMEM((G, S, 2 * dh), BF16),
                        pltpu.VMEM((G, 2 * dh, NSA_HPG * tq), BF16),
                        pltpu.VMEM((G, dh, NSA_HPG * tq), F32)],
        compiler_params=_cparams("arbitrary", "arbitrary"),
        name="nsa_attn",
    )(qt, kc, vct, ks, vst, kw, vwt, gt, ovl)


def _ffn_kernel(x_ref, y_ref, wo_ref, nw_ref, wg_ref, wu_ref, wd_ref, o_ref):
    h1 = x_ref[...] + _dot(y_ref[...], wo_ref[...])
    xn = _rms_rows(h1, nw_ref[...]).astype(BF16)
    hidden = wg_ref.shape[1]
    cuts = list(range(0, hidden, FFN_PIECE)) + [hidden]
    pieces = list(zip(cuts[:-1], cuts[1:]))
    gate_up = lambda a, b: (_dot(xn, wg_ref[:, a:b]), _dot(xn, wu_ref[:, a:b]))
    nxt = gate_up(*pieces[0])
    out = h1
    for i, (a, b) in enumerate(pieces):
        g, u = nxt
        if i + 1 < len(pieces):
            nxt = gate_up(*pieces[i + 1])
        out = out + _dot((_silu(g) * u).astype(BF16), wd_ref[a:b, :])
    o_ref[...] = out


def _ffn(h, y, wo, nw, wg, wu, wd, tm=512):
    T = h.shape[0]
    H = wg.shape[1]
    K = y.shape[1]
    row = lambda width: pl.BlockSpec((tm, width), lambda i: (i, 0))
    resident = lambda a: pl.BlockSpec(a.shape, lambda i: (0,) * a.ndim, pipeline_mode=pl.Buffered(1))
    return pl.pallas_call(
        _ffn_kernel,
        grid=(T // tm,),
        in_specs=[row(D_MODEL), row(K), resident(wo), resident(nw), resident(wg), resident(wu), resident(wd)],
        out_specs=row(D_MODEL),
        out_shape=jax.ShapeDtypeStruct((T, D_MODEL), F32),
        compiler_params=_cparams("parallel"),
        name="ffn",
    )(h, y, wo, nw, wg, wu, wd)


def _ssm_inproj_kernel(x_ref, nw_ref, w_ref, cw_ref, cb_ref, dtb_ref, alog_ref, tri_ref,
                       zs_ref, xbc_ref, cum_ref, cumt_ref, dtt_ref, ext_ref, *, tiles_per_seq):
    tm = x_ref.shape[0]
    halo, Q = SSM_HALO, SSM_CHUNK
    chunk = 256
    xn = _rms_rows(x_ref[...], nw_ref[...]).astype(BF16)

    @pl.when(pl.program_id(0) % tiles_per_seq == 0)
    def _():
        ext_ref[...] = jnp.zeros_like(ext_ref)

    row = lax.broadcasted_iota(jnp.int32, (halo, chunk), 0)

    def gate_out(lo, y):
        zs_ref[:, lo:lo + chunk] = _silu(y).astype(BF16)

    def conv_out(lo, x):
        prev = ext_ref[:, lo:lo + chunk]
        acc = cb_ref[:, lo:lo + chunk] + cw_ref[SSM_CONV - 1:SSM_CONV, lo:lo + chunk] * x
        for k in range(SSM_CONV - 1):
            sh = SSM_CONV - 1 - k
            r = pltpu.roll(x, sh, 0)
            top = jnp.where(row < sh, pltpu.roll(prev, sh, 0), r[0:halo])
            acc = acc + cw_ref[k:k + 1, lo:lo + chunk] * jnp.concatenate([top, r[halo:]], axis=0)
        ext_ref[:, lo:lo + chunk] = x[tm - halo:tm]
        xbc_ref[:, lo:lo + chunk] = _silu(acc).astype(BF16)

    gate_jobs = [(lo, lo, gate_out) for lo in range(0, SSM_D_INNER, chunk)]
    conv_jobs = [(SSM_D_INNER + lo, lo, conv_out) for lo in range(0, SSM_CONV_DIM, chunk)]
    jobs = []
    for j in range(max(len(gate_jobs), len(conv_jobs))):
        jobs += conv_jobs[j:j + 1] + gate_jobs[j:j + 1]
    proj = lambda j: _dot(xn, w_ref[:, jobs[j][0]:jobs[j][0] + chunk])
    nxt = proj(0)
    for j in range(len(jobs)):
        cur = nxt
        if j + 1 < len(jobs):
            nxt = proj(j + 1)
        jobs[j][2](jobs[j][1], cur)

    base = SSM_D_INNER + SSM_CONV_DIM
    dtl = _dot(xn, w_ref[:, base:base + SSM_DT_PAD]) + dtb_ref[...]
    dt = jnp.maximum(dtl, 0.0) + jnp.log(1.0 + jnp.exp(-jnp.abs(dtl)))
    a = dt * (-jnp.exp(alog_ref[...]))
    tri = tri_ref[...]
    for c in range(tm // Q):
        a1, a2, a3 = _split3(a[Q * c:Q * (c + 1)])
        cum = _dot(tri, a1) + _dot(tri, a2) + _dot(tri, a3)
        cum_ref[Q * c:Q * (c + 1), :] = cum
        cumt_ref[c] = cum.T
        dtt_ref[c] = dt[Q * c:Q * (c + 1)].T


def _ssm_inproj(h, nw, w, cw, cb, dtb, alog, tri, S, tm=512):
    T = h.shape[0]
    Q = SSM_CHUNK
    row = lambda width: pl.BlockSpec((tm, width), lambda i: (i, 0))
    full = lambda a: pl.BlockSpec(a.shape, lambda i: (0,) * a.ndim)
    tile = pl.BlockSpec((tm // Q, SSM_DT_PAD, Q), lambda i: (i, 0, 0))
    tshape = jax.ShapeDtypeStruct((T // Q, SSM_DT_PAD, Q), F32)
    return pl.pallas_call(
        functools.partial(_ssm_inproj_kernel, tiles_per_seq=S // tm),
        grid=(T // tm,),
        in_specs=[row(D_MODEL), full(nw), full(w), full(cw), full(cb), full(dtb), full(alog), full(tri)],
        out_specs=[row(SSM_D_INNER), row(SSM_CONV_DIM), row(SSM_DT_PAD), tile, tile],
        out_shape=[jax.ShapeDtypeStruct((T, SSM_D_INNER), BF16), jax.ShapeDtypeStruct((T, SSM_CONV_DIM), BF16),
                   jax.ShapeDtypeStruct((T, SSM_DT_PAD), F32), tshape, tshape],
        scratch_shapes=[pltpu.VMEM((SSM_HALO, SSM_CONV_DIM), F32)],
        compiler_params=_cparams("arbitrary"),
        name="ssm_inproj",
    )(h, nw, w, cw, cb, dtb, alog, tri)


def _ssd_kernel(xbc_ref, zs_ref, cum_ref, cumt_ref, dtt_ref, dsk_ref, nw_ref, y_ref, state_ref):
    Q, P, N, G, HPG = SSM_CHUNK, SSM_P, SSM_N, SSM_G, SSM_HPG
    gw = SSM_D_INNER // G

    @pl.when(pl.program_id(1) == 0)
    def _():
        state_ref[...] = jnp.zeros_like(state_ref)

    cum = cum_ref[...]
    cum_t = cumt_ref[0]
    dt_t = dtt_ref[0]
    row_i = lax.broadcasted_iota(jnp.int32, (Q, Q), 0)
    col_i = lax.broadcasted_iota(jnp.int32, (Q, Q), 1)
    tril = row_i >= col_i
    eye = row_i == col_i
    b_off = SSM_D_INNER
    c_off = SSM_D_INNER + G * N

    def elementwise(g):
        cg = xbc_ref[:, c_off + N * g:c_off + N * (g + 1)]
        bg = xbc_ref[:, b_off + N * g:b_off + N * (g + 1)]
        cb = _dot_nt(cg, bg)
        cg_f = cg.astype(F32)
        bg_t = bg.astype(F32).T
        st_g = state_ref[g]
        st_b = st_g.astype(BF16)
        lhs, wgt, rhs, keep = [], [], [], []
        for hh in range(HPG):
            h = g * HPG + hh
            cum_b = jnp.broadcast_to(cum[:, h:h + 1], (Q, Q))
            cum_row = cum_t[h:h + 1, :]
            dt_row = dt_t[h:h + 1, :]
            cum_last = cum_row[:, Q - 1:Q]
            mm = cb * jnp.exp(jnp.where(tril, cum_b - cum_row, -jnp.inf)) * dt_row
            mm = jnp.where(eye, mm + dsk_ref[:, h:h + 1], mm)
            lhs += [mm.astype(BF16), (jnp.exp(cum_b) * cg_f).astype(BF16)]
            wgt.append((bg_t * (dt_row * jnp.exp(cum_last - cum_row))).astype(BF16))
            keep.append(jnp.broadcast_to(jnp.exp(cum_last), (1, P)))
        low = lax.broadcasted_iota(jnp.int32, (Q, 2 * P), 1) < P
        zero = jnp.zeros((Q, 2 * P), BF16)
        for pr in range(HPG // 2):
            xp = xbc_ref[:, 2 * P * (g * HPG // 2 + pr):2 * P * (g * HPG // 2 + pr + 1)]
            sp = st_b[:, 2 * P * pr:2 * P * (pr + 1)]
            rhs.append((jnp.where(low, xp, zero), jnp.where(low, sp, zero),
                        jnp.where(low, zero, xp), jnp.where(low, zero, sp)))
        return lhs, wgt, rhs, st_g * jnp.concatenate(keep, axis=1)

    def matmuls(g, ops):
        lhs, wgt, rhs, kept = ops
        ys, upd = [], []
        for pr in range(HPG // 2):
            x_lo, s_lo, x_hi, s_hi = rhs[pr]
            ys.append(_dot(jnp.concatenate(lhs[4 * pr:4 * pr + 4], axis=1),
                           jnp.concatenate([x_lo, s_lo, x_hi, s_hi], axis=0)))
            upd.append(_dot(jnp.concatenate(wgt[2 * pr:2 * pr + 2], axis=1),
                            jnp.concatenate([x_lo, x_hi], axis=0)))
        state_ref[g] = kept + jnp.concatenate(upd, axis=1)
        yg = jnp.concatenate(ys, axis=1) * zs_ref[:, gw * g:gw * (g + 1)].astype(F32)
        yg = yg * lax.rsqrt(jnp.mean(yg * yg, axis=-1, keepdims=True) + EPS)
        y_ref[:, gw * g:gw * (g + 1)] = (yg * nw_ref[:, gw * g:gw * (g + 1)]).astype(BF16)

    ops = elementwise(0)
    for g in range(G):
        nxt = elementwise(g + 1) if g + 1 < G else None
        matmuls(g, ops)
        ops = nxt


def _ssd(xbc, zs, cum, cumt, dtt, dsk, nw, B, S):
    Q = SSM_CHUNK
    nch = S // Q
    row = lambda width: pl.BlockSpec((Q, width), lambda b, c: (b * nch + c, 0))
    full = lambda a: pl.BlockSpec(a.shape, lambda b, c: (0,) * a.ndim)
    tile = pl.BlockSpec((1, SSM_DT_PAD, Q), lambda b, c: (b * nch + c, 0, 0))
    return pl.pallas_call(
        _ssd_kernel,
        grid=(B, nch),
        in_specs=[row(SSM_CONV_DIM), row(SSM_D_INNER), row(SSM_DT_PAD), tile, tile, full(dsk), full(nw)],
        out_specs=row(SSM_D_INNER),
        out_shape=jax.ShapeDtypeStruct((B * S, SSM_D_INNER), BF16),
        scratch_shapes=[pltpu.VMEM((SSM_G, SSM_N, SSM_HPG * SSM_P), F32)],
        compiler_params=_cparams("arbitrary", "arbitrary"),
        name="ssd_scan",
    )(xbc, zs, cum, cumt, dtt, dsk, nw)


def _block_diag_mean():
    i = np.arange(LANES)
    return jnp.asarray((i[:, None] // HEAD_DIM == i[None, :] // HEAD_DIM) / HEAD_DIM, BF16)


def _overlap_t(S):
    nc = (S - CMP_BLOCK) // CMP_STRIDE + 1
    ncp = S // CMP_STRIDE
    nblk = S // SEL_BLOCK
    starts = np.arange(ncp) * CMP_STRIDE
    js = np.arange(nblk)[:, None] * SEL_BLOCK
    ov = (starts[None, :] < js + SEL_BLOCK) & (starts[None, :] + CMP_BLOCK > js) & (np.arange(ncp)[None, :] < nc)
    out = np.zeros((LANES, ncp), np.float32)
    out[:nblk] = ov
    return jnp.asarray(out, BF16)


def _pad_cols(w, width):
    return jnp.pad(w, ((0, 0), (0, width - w.shape[1])))


def _nsa_layer(h, tabs, tabs_t, tabs_c, B, S, nw, w_in, q_norm, k_norm, cmp_pe, cmp_w1, cmp_b1, cmp_w2, w_out):
    G, dh, kvw = NSA_G, HEAD_DIM, NSA_KV_WIDTH
    ncp = S // CMP_STRIDE
    cut = lambda i: w_in[:, NSA_Q_WIDTH + i * kvw:NSA_Q_WIDTH + (i + 1) * kvw]
    wqt = w_in[:, :NSA_Q_WIDTH].T.astype(BF16)
    wk = jnp.concatenate([cut(0), cut(1), cut(2), cut(4)], axis=1).astype(BF16)
    wvt = jnp.concatenate([cut(3), cut(5)], axis=1).T.astype(BF16)
    wgt = _pad_cols(w_in[:, NSA_Q_WIDTH + 6 * kvw:], NSA_GATE_PAD).T.astype(BF16)
    qn = jnp.broadcast_to(q_norm[:, None], (dh, NSA_TQ))
    ksn = jnp.tile(k_norm[1], G)[None, :]
    kwn = jnp.tile(k_norm[2], G)[None, :]
    qt, kc_raw, vc_raw, ks, kw, vst, vwt, gt = _nsa_inproj(h, nw[None, :], wqt, wk, wvt, wgt, tabs, tabs_t,
                                                           qn, ksn, kwn, _block_diag_mean(), S)
    half = CMP_STRIDE * dh
    pe = cmp_pe.reshape(2, 2, half)
    w1 = cmp_w1.reshape(2, 2, half, CMP_HIDDEN).astype(BF16)
    b1 = cmp_b1[:, None, :]
    w2k = _pad_cols(cmp_w2[0], LANES).astype(BF16)
    w2vt = cmp_w2[1].T.astype(BF16)
    knw = _pad_cols(k_norm[0][None, :], LANES)
    kc, vct = _nsa_compress(kc_raw, vc_raw, pe, w1, b1, w2k, w2vt, knw, tabs_c)

    o = _nsa_attn(qt, kc, vct, ks, vst, kw, vwt, gt, _overlap_t(S), B, S)
    return o, w_out.astype(BF16)


def _ssd_layer(h, B, S, nw, w_in, conv_w, conv_b, dt_bias, a_log, d_skip, norm_w, w_out):
    w = _pad_cols(w_in, SSM_IN_PAD).astype(BF16)
    pad1 = lambda v: _pad_cols(v[None, :], SSM_DT_PAD)
    tri = jnp.asarray(np.tril(np.ones((SSM_CHUNK, SSM_CHUNK), np.float32)), BF16)
    zs, xbc, cum, cumt, dtt = _ssm_inproj(h, nw[None, :], w, conv_w, conv_b[None, :], pad1(dt_bias), pad1(a_log),
                                          tri, S)
    y = _ssd(xbc, zs, cum, cumt, dtt, pad1(d_skip), norm_w[None, :], B, S)
    return y, w_out.astype(BF16)


def kernel(x, positions, mix_norm_w, ffn_norm_w, ffn_w_gate, ffn_w_up, ffn_w_down, nsa_w_in, nsa_q_norm, nsa_k_norm, nsa_cmp_pe, nsa_cmp_w1, nsa_cmp_b1, nsa_cmp_w2, nsa_w_out, ssm_w_in, ssm_conv_w, ssm_conv_b, ssm_dt_bias, ssm_a_log, ssm_d, ssm_norm_w, ssm_w_out):
    B, S, D = x.shape
    T = B * S
    h = x.reshape(T, D)
    ncp = S // CMP_STRIDE
    tabs = _rope_tables(positions.reshape(T, 1), 1024)
    tabs_t = _rope_tables_t(positions.reshape(1, T), 2048)
    pos_c = jnp.pad(positions[:, CMP_BLOCK - 1::CMP_STRIDE], ((0, 0), (0, 1)))[:, :ncp]
    tabs_c = _rope_tables(pos_c.reshape(B * ncp, 1), ncp)
    for i in range(DEPTH):
        j = i // 2
        if i % 2 == 0:
            y, wo = _nsa_layer(h, tabs, tabs_t, tabs_c, B, S, mix_norm_w[i], nsa_w_in[j], nsa_q_norm[j],
                               nsa_k_norm[j], nsa_cmp_pe[j], nsa_cmp_w1[j], nsa_cmp_b1[j], nsa_cmp_w2[j],
                               nsa_w_out[j])
        else:
            y, wo = _ssd_layer(h, B, S, mix_norm_w[i], ssm_w_in[j], ssm_conv_w[j], ssm_conv_b[j], ssm_dt_bias[j],
                               ssm_a_log[j], ssm_d[j], ssm_norm_w[j], ssm_w_out[j])
        h = _ffn(h, y, wo, ffn_norm_w[i][None, :], ffn_w_gate[i].astype(BF16), ffn_w_up[i].astype(BF16),
                 ffn_w_down[i].astype(BF16))
    return h.reshape(B, S, D)
```

```python
import functools
import math

import numpy as np
import jax
import jax.numpy as jnp
from jax import lax
from jax.experimental import pallas as pl
from jax.experimental.pallas import tpu as pltpu

F32 = jnp.float32
BF16 = jnp.bfloat16

D_MODEL = 1024
DEPTH = 4
EPS = 1e-6

NSA_HEADS = 16
NSA_G = 4
NSA_HPG = NSA_HEADS // NSA_G
HEAD_DIM = 64
CMP_BLOCK = 32
CMP_STRIDE = 16
CMP_HIDDEN = 256
SEL_BLOCK = 64
SEL_SHIFT = 6
SEL_TOPK = 8
WINDOW = 512
ROPE_THETA = 500000.0
ROPE_DIM = HEAD_DIM // 4
ROPE_HALF = ROPE_DIM // 2
NSA_Q_WIDTH = NSA_HEADS * HEAD_DIM
NSA_KV_WIDTH = NSA_G * HEAD_DIM
NSA_GATE_PAD = 128
NSA_TQ = 256
V_PAD = 16
QSCALE = HEAD_DIM ** -0.5 * math.log2(math.e)

SSM_D_INNER = 2 * D_MODEL
SSM_P = 64
SSM_HEADS = SSM_D_INNER // SSM_P
SSM_G = 4
SSM_HPG = SSM_HEADS // SSM_G
SSM_N = 128
SSM_CONV = 4
SSM_CHUNK = 128
SSM_CONV_DIM = SSM_D_INNER + 2 * SSM_G * SSM_N
SSM_DT_PAD = 128
SSM_HALO = 8
SSM_IN_PAD = SSM_D_INNER + SSM_CONV_DIM + SSM_DT_PAD

FFN_HIDDEN = -(-8 * D_MODEL // (3 * 256)) * 256
FFN_PIECE = 768

LANES = 128
VMEM_LIMIT_BYTES = 52 * 1024 * 1024

MASK_BIG = 1e30
KEY_FORCED = 0x7F000000
M_INIT = -3e38

_NT = (((1,), (1,)), ((), ()))


def _cparams(*sem, flags=None):
    return pltpu.CompilerParams(dimension_semantics=sem, vmem_limit_bytes=VMEM_LIMIT_BYTES, flags=flags)


def _dot(a, b):
    return jnp.dot(a, b, preferred_element_type=F32)


def _dot_nt(a, b):
    return lax.dot_general(a, b, _NT, preferred_element_type=F32)


def _split3(x):
    a = x.astype(BF16)
    r = x - a.astype(F32)
    b = r.astype(BF16)
    c = (r - b.astype(F32)).astype(BF16)
    return a, b, c


def _rms_rows(x, w):
    return x * lax.rsqrt(jnp.mean(x * x, axis=-1, keepdims=True) + EPS) * w


def _silu(x):
    h = 0.5 * x
    return h + h * jnp.tanh(h)


def _rope_table_kernel(pos_ref, c_ref, sa_ref, sb_ref):
    pos = pos_ref[...].astype(F32)
    lane = lax.broadcasted_iota(jnp.int32, (1, LANES), 1)
    d = lane & (HEAD_DIM - 1)
    f = d & (ROPE_HALF - 1)
    inv = jnp.zeros((1, LANES), F32)
    for i in range(ROPE_HALF):
        inv = jnp.where(f == i, float(np.power(np.float32(ROPE_THETA), np.float32(-i / ROPE_HALF))), inv)
    ang = pos * inv
    cos, sin = jnp.cos(ang), jnp.sin(ang)
    c_ref[...] = jnp.where(d < ROPE_DIM, cos, 1.0)
    sa_ref[...] = jnp.where(d < ROPE_HALF, -sin, 0.0)
    sb_ref[...] = jnp.where((d >= ROPE_HALF) & (d < ROPE_DIM), sin, 0.0)


def _rope_tables(pos_col, tm):
    n = pos_col.shape[0]
    out = jax.ShapeDtypeStruct((n, LANES), F32)
    spec = pl.BlockSpec((tm, LANES), lambda i: (i, 0))
    return pl.pallas_call(
        _rope_table_kernel,
        grid=(n // tm,),
        in_specs=[pl.BlockSpec((tm, 1), lambda i: (i, 0))],
        out_specs=[spec, spec, spec],
        out_shape=[out, out, out],
        compiler_params=_cparams("parallel"),
        name="rope_tables",
    )(pos_col)


def _rope_table_t_kernel(pos_ref, c_ref, s_ref):
    pos = pos_ref[...].astype(F32)
    f = lax.broadcasted_iota(jnp.int32, (ROPE_HALF, 1), 0)
    inv = jnp.zeros((ROPE_HALF, 1), F32)
    for i in range(ROPE_HALF):
        inv = jnp.where(f == i, float(np.power(np.float32(ROPE_THETA), np.float32(-i / ROPE_HALF))), inv)
    ang = inv * pos
    c_ref[...] = jnp.cos(ang)
    s_ref[...] = jnp.sin(ang)


def _rope_tables_t(pos_row, tm):
    n = pos_row.shape[1]
    out = jax.ShapeDtypeStruct((ROPE_HALF, n), F32)
    spec = pl.BlockSpec((ROPE_HALF, tm), lambda i: (0, i))
    return pl.pallas_call(
        _rope_table_t_kernel,
        grid=(n // tm,),
        in_specs=[pl.BlockSpec((1, tm), lambda i: (0, i))],
        out_specs=[spec, spec],
        out_shape=[out, out],
        compiler_params=_cparams("parallel"),
        name="rope_tables_t",
    )(pos_row)


def _rope_lanes(x, c, sa, sb):
    return x * c + pltpu.roll(x, LANES - ROPE_HALF, 1) * sa + pltpu.roll(x, ROPE_HALF, 1) * sb


def _head_norm_rope(y, w, bd, c, sa, sb, scale):
    outs = []
    for j in range(y.shape[1] // LANES):
        yc = y[:, LANES * j:LANES * (j + 1)]
        sq = yc * yc
        hi = sq.astype(BF16)
        lo = (sq - hi.astype(F32)).astype(BF16)
        ms = _dot(hi, bd) + _dot(lo, bd)
        yn = yc * lax.rsqrt(ms + EPS) * w[:, LANES * j:LANES * (j + 1)]
        outs.append(_rope_lanes(yn, c, sa, sb) * scale)
    return jnp.concatenate(outs, axis=1)


def _nsa_inproj_kernel(x_ref, nw_ref, wqt_ref, wk_ref, wvt_ref, wgt_ref, c_ref, sa_ref, sb_ref, ct_ref, st_ref,
                       qn_ref, ksn_ref, kwn_ref, bd_ref,
                       qt_ref, kc_ref, vc_ref, ks_ref, kw_ref, vst_ref, vwt_ref, gt_ref, raw_ref):
    xn = _rms_rows(x_ref[...], nw_ref[...]).astype(BF16)
    c, sa, sb, bd = c_ref[...], sa_ref[...], sb_ref[...], bd_ref[...]
    kvw, dh, tq = NSA_KV_WIDTH, HEAD_DIM, NSA_TQ
    nrow = x_ref.shape[0] // CMP_STRIDE

    qn = qn_ref[...]

    def raw_out(raw):
        for j in range(2 * kvw // LANES):
            raw_ref[j] = raw[:, LANES * j:LANES * (j + 1)]
        for which, out_ref in enumerate((kc_ref, vc_ref)):
            for g in range(NSA_G):
                j, off = divmod(which * kvw + dh * g, LANES)
                for l in range(CMP_STRIDE):
                    rows = raw_ref[j, pl.ds(l, nrow, stride=CMP_STRIDE), :]
                    out_ref[0, g, :, dh * l:dh * (l + 1)] = rows[:, off:off + dh]

    def key_out(out_ref, gain_ref):
        def write(y):
            out_ref[...] = _head_norm_rope(y, gain_ref[...], bd, c, sa, sb, 1.0).astype(BF16)
        return write

    def q_out(ch):
        def write(yt):
            cos, sin = ct_ref[:, ch * tq:(ch + 1) * tq], st_ref[:, ch * tq:(ch + 1) * tq]
            for hd in range(NSA_HEADS):
                yh = yt[dh * hd:dh * (hd + 1)]
                yn = yh * lax.rsqrt(jnp.mean(yh * yh, axis=0, keepdims=True) + EPS) * qn
                x1, x2 = yn[0:ROPE_HALF], yn[ROPE_HALF:ROPE_DIM]
                rot = jnp.concatenate([x1 * cos - x2 * sin, x2 * cos + x1 * sin, yn[ROPE_DIM:]], axis=0)
                qt_ref[ch, dh * hd:dh * (hd + 1), :] = (rot * QSCALE).astype(BF16)
        return write

    def v_out(ch):
        def write(vt):
            vst_ref[ch] = vt[0:kvw].astype(BF16)
            vwt_ref[ch] = vt[kvw:2 * kvw].astype(BF16)
        return write

    def g_out(ch):
        def write(gl):
            gt_ref[ch] = 1.0 / (1.0 + jnp.exp(-gl))
        return write

    jobs = [(lambda: _dot(xn, wk_ref[:, 0:2 * kvw]), raw_out),
            (lambda: _dot(xn, wk_ref[:, 2 * kvw:3 * kvw]), key_out(ks_ref, ksn_ref)),
            (lambda: _dot(xn, wk_ref[:, 3 * kvw:4 * kvw]), key_out(kw_ref, kwn_ref))]
    for ch in range(x_ref.shape[0] // tq):
        xc = xn[ch * tq:(ch + 1) * tq]
        jobs += [(functools.partial(_dot_nt, wqt_ref[...], xc), q_out(ch)),
                 (functools.partial(_dot_nt, wvt_ref[...], xc), v_out(ch)),
                 (functools.partial(_dot_nt, wgt_ref[...], xc), g_out(ch))]
    nxt = jobs[0][0]()
    for j, (_, epilogue) in enumerate(jobs):
        cur = nxt
        if j + 1 < len(jobs):
            nxt = jobs[j + 1][0]()
        epilogue(cur)


def _nsa_inproj(h, nw, wqt, wk, wvt, wgt, tabs, tabs_t, qn, ksn, kwn, bd, S, tm=512):
    T = h.shape[0]
    tq, kvw = NSA_TQ, NSA_KV_WIDTH
    per_seq = S // tm
    width = CMP_STRIDE * HEAD_DIM
    chunked = pl.BlockSpec((1, NSA_G, tm // CMP_STRIDE, width), lambda i: (i // per_seq, 0, i % per_seq, 0))
    cshape = jax.ShapeDtypeStruct((T // S, NSA_G, S // CMP_STRIDE, width), F32)
    row = lambda width: pl.BlockSpec((tm, width), lambda i: (i, 0))
    full = lambda a: pl.BlockSpec(a.shape, lambda i: (0,) * a.ndim)
    colt = pl.BlockSpec((ROPE_HALF, tm), lambda i: (0, i))
    tile = lambda ch: pl.BlockSpec((tm // tq, ch, tq), lambda i: (i, 0, 0))
    tshape = lambda ch, dt: jax.ShapeDtypeStruct((T // tq, ch, tq), dt)
    c, sa, sb = tabs
    ct, st = tabs_t
    return pl.pallas_call(
        _nsa_inproj_kernel,
        grid=(T // tm,),
        in_specs=[row(D_MODEL), full(nw), full(wqt), full(wk), full(wvt), full(wgt),
                  row(LANES), row(LANES), row(LANES), colt, colt,
                  full(qn), full(ksn), full(kwn), full(bd)],
        out_specs=[tile(NSA_Q_WIDTH), chunked, chunked, row(kvw), row(kvw), tile(kvw), tile(kvw),
                   tile(NSA_GATE_PAD)],
        out_shape=[tshape(NSA_Q_WIDTH, BF16), cshape, cshape, jax.ShapeDtypeStruct((T, kvw), BF16),
                   jax.ShapeDtypeStruct((T, kvw), BF16), tshape(kvw, BF16), tshape(kvw, BF16),
                   tshape(NSA_GATE_PAD, F32)],
        scratch_shapes=[pltpu.VMEM((2 * kvw // LANES, tm, LANES), F32)],
        compiler_params=_cparams("parallel"),
        name="nsa_inproj",
    )(h, nw, wqt, wk, wvt, wgt, c, sa, sb, ct, st, qn, ksn, kwn, bd)


def _nsa_compress_kernel(xk_ref, xv_ref, pe_ref, w1_ref, b1_ref, w2k_ref, w2vt_ref, knw_ref,
                         c_ref, sa_ref, sb_ref, kc_ref, vct_ref):
    ncp = xk_ref.shape[2]

    def hidden(x, which):
        lo = _dot((x + pe_ref[which, 0:1, :]).astype(BF16), w1_ref[which, 0])
        hi = _dot((x + pe_ref[which, 1:2, :]).astype(BF16), w1_ref[which, 1])
        return _silu(lo + pltpu.roll(hi, ncp - 1, 0) + b1_ref[which]).astype(BF16)

    for g in range(NSA_G):
        kc = _dot(hidden(xk_ref[0, g], 0), w2k_ref[...])
        ms = jnp.sum(kc * kc, axis=-1, keepdims=True) * (1.0 / HEAD_DIM)
        kn = kc * lax.rsqrt(ms + EPS) * knw_ref[...]
        kn = _rope_lanes(kn, c_ref[...], sa_ref[...], sb_ref[...])
        kc_ref[0, g] = kn[:, :HEAD_DIM].astype(BF16)
        vct_ref[0, g] = _dot_nt(w2vt_ref[...], hidden(xv_ref[0, g], 1)).astype(BF16)


def _nsa_compress(xk, xv, pe, w1, b1, w2k, w2vt, knw, tabs):
    B, G, ncp, width = xk.shape
    c, sa, sb = tabs
    xspec = pl.BlockSpec((1, G, ncp, width), lambda b: (b, 0, 0, 0))
    tspec = pl.BlockSpec((ncp, LANES), lambda b: (b, 0))
    full = lambda a: pl.BlockSpec(a.shape, lambda b: (0,) * a.ndim)
    return pl.pallas_call(
        _nsa_compress_kernel,
        grid=(B,),
        in_specs=[xspec, xspec, full(pe), full(w1), full(b1), full(w2k), full(w2vt), full(knw),
                  tspec, tspec, tspec],
        out_specs=[pl.BlockSpec((1, G, ncp, HEAD_DIM), lambda b: (b, 0, 0, 0)),
                   pl.BlockSpec((1, G, HEAD_DIM, ncp), lambda b: (b, 0, 0, 0))],
        out_shape=[jax.ShapeDtypeStruct((B, G, ncp, HEAD_DIM), BF16),
                   jax.ShapeDtypeStruct((B, G, HEAD_DIM, ncp), BF16)],
        compiler_params=_cparams("parallel"),
        name="nsa_compress",
    )(xk, xv, pe, w1, b1, w2k, w2vt, knw, c, sa, sb)


def _flash_steps(jobs):
    n = len(jobs)
    ahead = 2
    score = lambda i: _dot(jobs[i][1], jobs[i][0])
    scores = [score(i) for i in range(min(ahead, n))]
    out = []
    for i, (_, _, vt, state, mask) in enumerate(jobs):
        m, acc = out[state] if isinstance(state, int) else state
        s = scores[i]
        if mask is not None:
            s = jnp.where(mask, s, -MASK_BIG)
        m_new = jnp.maximum(m, jnp.max(s, axis=0, keepdims=True))
        p = jnp.exp2(s - m_new).astype(BF16)
        out.append((m_new, jnp.exp2(m - m_new) * acc + _dot(vt, p)))
        if i + ahead < n:
            scores.append(score(i + ahead))
    return out


def _nsa_attn_kernel(qt_ref, kc_ref, vct_ref, ks_ref, vst_ref, kw_ref, vwt_ref, gt_ref, ovl_ref, o_ref,
                     ksa_ref, kwa_ref, qat_ref, part_ref):
    S = ks_ref.shape[0]
    ncp = kc_ref.shape[2]
    nblk = S // SEL_BLOCK
    G, HP, dh, tq = NSA_G, NSA_HPG, HEAD_DIM, NSA_TQ
    cols = HP * tq
    qi = pl.program_id(1)
    q0 = qi * tq

    @pl.when(qi == 0)
    def _():
        rblk = lax.broadcasted_iota(jnp.int32, (S, dh), 0) >> SEL_SHIFT
        lane = lax.broadcasted_iota(jnp.int32, (S, dh), 1)
        onehot = jnp.where(rblk == lane, 1.0, 0.0).astype(BF16)
        zeros = jnp.zeros((S, dh), BF16)
        for g in range(G):
            ksa_ref[g, :, 0:dh] = ks_ref[:, dh * g:dh * (g + 1)]
            ksa_ref[g, :, dh:2 * dh] = onehot
            kwa_ref[g, :, 0:dh] = kw_ref[:, dh * g:dh * (g + 1)]
            kwa_ref[g, :, dh:2 * dh] = zeros

    t_cols = q0 + (lax.broadcasted_iota(jnp.int32, (1, cols), 1) & (tq - 1))
    t_q = t_cols[:, 0:tq]
    k_loc = lax.broadcasted_iota(jnp.int32, (tq, 1), 0)

    cmp_end = lax.broadcasted_iota(jnp.int32, (ncp, 1), 0) * CMP_STRIDE + (CMP_BLOCK - 1)
    cmask = cmp_end <= t_cols
    jb = lax.broadcasted_iota(jnp.int32, (nblk, tq), 0)
    jb_col = lax.broadcasted_iota(jnp.int32, (nblk, 1), 0)
    tblk = t_q >> SEL_SHIFT
    forced = (jb == 0) | (jb == tblk) | (jb == tblk - 1)
    gt = gt_ref[0]

    def gate_row(g, branch):
        return jnp.concatenate([gt[3 * (g * HP + h) + branch:3 * (g * HP + h) + branch + 1] for h in range(HP)],
                               axis=1)

    init = (jnp.full((1, cols), M_INIT, F32), jnp.zeros((dh + V_PAD, cols), F32))
    ones_rows = jnp.where(lax.broadcasted_iota(jnp.int32, (V_PAD, tq), 0) == 0, 1.0, 0.0).astype(BF16)

    def v_aug(v_ref, kt, g):
        return jnp.concatenate([v_ref[kt, dh * g:dh * (g + 1), :], ones_rows], axis=0)

    def normalised(acc):
        return acc[0:dh] * (1.0 / acc[dh:dh + 1])

    causal = (q0 + k_loc) <= t_cols
    n_back = (WINDOW + tq - 1) // tq
    back = []
    c_loc = t_cols - q0
    for dk in range(1, n_back + 1):
        kt = qi - dk
        far = jnp.where(kt < 0, 2 * WINDOW + S, 0)
        back.append((jnp.maximum(kt, 0), (c_loc + (dk * tq - WINDOW + far)) < k_loc))

    grp = range(G)
    for g in grp:
        qat_ref[g, dh + nblk:2 * dh, :] = jnp.zeros((dh - nblk, cols), BF16)
        for h in range(HP):
            hd = g * HP + h
            qat_ref[g, 0:dh, h * tq:(h + 1) * tq] = qt_ref[0, dh * hd:dh * (hd + 1), :]

    sc = [_dot(kc_ref[0, g], qat_ref[g, 0:dh, :]) for g in grp]
    pc = []
    for g in grp:
        s = jnp.where(cmask, sc[g], -MASK_BIG)
        m = jnp.max(s, axis=0, keepdims=True)
        p = jnp.where(cmask, jnp.exp2(s - m), 0.0)
        l = jnp.sum(p, axis=0, keepdims=True)
        pc.append(p * jnp.where(l > 0.0, 1.0 / l, 0.0))
    oc = [_dot(vct_ref[0, g], pc[g].astype(BF16)) for g in grp]
    ovl = ovl_ref[...]
    imp = []
    for g in grp:
        psum = pc[g][:, 0:tq]
        for h in range(1, HP):
            psum = psum + pc[g][:, h * tq:(h + 1) * tq]
        p1, p2, p3 = _split3(psum)
        imp.append((_dot(ovl, p1) + _dot(ovl, p2) + _dot(ovl, p3))[0:nblk])
    for g in grp:
        key = jnp.where(forced, KEY_FORCED, jnp.where(jb > tblk, -1, pltpu.bitcast(imp[g], jnp.int32)))
        cnt = jnp.zeros((nblk, tq), jnp.int32)
        for j in range(nblk):
            rj = key[j:j + 1, :] + jnp.where(jb_col > j, 1, 0)
            cnt = cnt + jnp.where(rj > key, 1, 0)
        selneg = jnp.where(cnt < SEL_TOPK, 0.0, -MASK_BIG).astype(BF16)
        for h in range(HP):
            qat_ref[g, dh:dh + nblk, h * tq:(h + 1) * tq] = selneg
    qats = [qat_ref[g] for g in grp]

    jobs = [(qats[g], kwa_ref[g, pl.ds(q0, tq), :], v_aug(vwt_ref, qi, g), init, causal) for g in grp]
    jobs += [(qats[g], ksa_ref[g, pl.ds(q0, tq), :], v_aug(vst_ref, qi, g), init, causal) for g in grp]
    last_win = list(grp)
    for kt, inside in back:
        k0 = pl.multiple_of(kt * tq, tq)
        for g in grp:
            jobs.append((qats[g], kwa_ref[g, pl.ds(k0, tq), :], v_aug(vwt_ref, kt, g), last_win[g], inside))
            last_win[g] = len(jobs) - 1
    res = _flash_steps(jobs)
    for g in grp:
        part_ref[g] = gate_row(g, 0) * oc[g] + gate_row(g, 2) * normalised(res[last_win[g]][1])

    def sel_body(kt, states):
        k0 = pl.multiple_of(kt * tq, tq)
        return tuple(_flash_steps([(qat_ref[g], ksa_ref[g, pl.ds(k0, tq), :], v_aug(vst_ref, kt, g), states[g], None)
                                   for g in grp]))

    sel_state = lax.fori_loop(0, qi, sel_body, tuple(res[G + g] for g in grp))

    for g in range(G):
        og = part_ref[g] + gate_row(g, 1) * normalised(sel_state[g][1])
        og_t = jnp.concatenate([og[:, h * tq:(h + 1) * tq] for h in range(HP)], axis=0)
        o_ref[:, HP * dh * g:HP * dh * (g + 1)] = og_t.T.astype(BF16)


def _nsa_attn(qt, kc, vct, ks, vst, kw, vwt, gt, ovl, B, S):
    T = B * S
    tq = NSA_TQ
    nq = S // tq
    G, dh = NSA_G, HEAD_DIM
    ncp = kc.shape[2]
    qspec = pl.BlockSpec((1, NSA_Q_WIDTH, tq), lambda b, i: (b * nq + i, 0, 0))
    gspec = pl.BlockSpec((1, NSA_GATE_PAD, tq), lambda b, i: (b * nq + i, 0, 0))
    kcspec = pl.BlockSpec((1, G, ncp, dh), lambda b, i: (b, 0, 0, 0))
    vcspec = pl.BlockSpec((1, G, dh, ncp), lambda b, i: (b, 0, 0, 0))
    kspec = pl.BlockSpec((S, NSA_KV_WIDTH), lambda b, i: (b, 0))
    vspec = pl.BlockSpec((nq, NSA_KV_WIDTH, tq), lambda b, i: (b, 0, 0))
    ovspec = pl.BlockSpec(ovl.shape, lambda b, i: (0, 0))
    return pl.pallas_call(
        _nsa_attn_kernel,
        grid=(B, nq),
        in_specs=[qspec, kcspec, vcspec, kspec, vspec, kspec, vspec, gspec, ovspec],
        out_specs=pl.BlockSpec((tq, NSA_Q_WIDTH), lambda b, i: (b * nq + i, 0)),
        out_shape=jax.ShapeDtypeStruct((T, NSA_Q_WIDTH), BF16),
        scratch_shapes=[pltpu.VMEM((G, S, 2 * dh), BF16), pltpu.VMEM((G, S, 2 * dh), BF16),
                        pltpu.VMEM((G, 2 * dh, NSA_HPG * tq), BF16),
                        pltpu.VMEM((G, dh, NSA_HPG * tq), F32)],
        compiler_params=_cparams("arbitrary", "arbitrary"),
        name="nsa_attn",
    )(qt, kc, vct, ks, vst, kw, vwt, gt, ovl)


def _ffn_kernel(x_ref, y_ref, wo_ref, nw_ref, wg_ref, wu_ref, wd_ref, o_ref):
    h1 = x_ref[...] + _dot(y_ref[...], wo_ref[...])
    xn = _rms_rows(h1, nw_ref[...]).astype(BF16)
    hidden = wg_ref.shape[1]
    cuts = list(range(0, hidden, FFN_PIECE)) + [hidden]
    pieces = list(zip(cuts[:-1], cuts[1:]))
    gate_up = lambda a, b: (_dot(xn, wg_ref[:, a:b]), _dot(xn, wu_ref[:, a:b]))
    nxt = gate_up(*pieces[0])
    out = h1
    for i, (a, b) in enumerate(pieces):
        g, u = nxt
        if i + 1 < len(pieces):
            nxt = gate_up(*pieces[i + 1])
        out = out + _dot((_silu(g) * u).astype(BF16), wd_ref[a:b, :])
    o_ref[...] = out


def _ffn(h, y, wo, nw, wg, wu, wd, tm=512):
    T = h.shape[0]
    H = wg.shape[1]
    K = y.shape[1]
    row = lambda width: pl.BlockSpec((tm, width), lambda i: (i, 0))
    resident = lambda a: pl.BlockSpec(a.shape, lambda i: (0,) * a.ndim, pipeline_mode=pl.Buffered(1))
    return pl.pallas_call(
        _ffn_kernel,
        grid=(T // tm,),
        in_specs=[row(D_MODEL), row(K), resident(wo), resident(nw), resident(wg), resident(wu), resident(wd)],
        out_specs=row(D_MODEL),
        out_shape=jax.ShapeDtypeStruct((T, D_MODEL), F32),
        compiler_params=_cparams("parallel"),
        name="ffn",
    )(h, y, wo, nw, wg, wu, wd)


def _ssm_inproj_kernel(x_ref, nw_ref, w_ref, cw_ref, cb_ref, dtb_ref, alog_ref, tri_ref,
                       zs_ref, xbc_ref, cum_ref, cumt_ref, dtt_ref, ext_ref, *, tiles_per_seq):
    tm = x_ref.shape[0]
    halo, Q = SSM_HALO, SSM_CHUNK
    chunk = 256
    xn = _rms_rows(x_ref[...], nw_ref[...]).astype(BF16)

    @pl.when(pl.program_id(0) % tiles_per_seq == 0)
    def _():
        ext_ref[...] = jnp.zeros_like(ext_ref)

    row = lax.broadcasted_iota(jnp.int32, (halo, chunk), 0)

    def gate_out(lo, y):
        zs_ref[:, lo:lo + chunk] = _silu(y).astype(BF16)

    def conv_out(lo, x):
        prev = ext_ref[:, lo:lo + chunk]
        acc = cb_ref[:, lo:lo + chunk] + cw_ref[SSM_CONV - 1:SSM_CONV, lo:lo + chunk] * x
        for k in range(SSM_CONV - 1):
            sh = SSM_CONV - 1 - k
            r = pltpu.roll(x, sh, 0)
            top = jnp.where(row < sh, pltpu.roll(prev, sh, 0), r[0:halo])
            acc = acc + cw_ref[k:k + 1, lo:lo + chunk] * jnp.concatenate([top, r[halo:]], axis=0)
        ext_ref[:, lo:lo + chunk] = x[tm - halo:tm]
        xbc_ref[:, lo:lo + chunk] = _silu(acc).astype(BF16)

    gate_jobs = [(lo, lo, gate_out) for lo in range(0, SSM_D_INNER, chunk)]
    conv_jobs = [(SSM_D_INNER + lo, lo, conv_out) for lo in range(0, SSM_CONV_DIM, chunk)]
    jobs = []
    for j in range(max(len(gate_jobs), len(conv_jobs))):
        jobs += conv_jobs[j:j + 1] + gate_jobs[j:j + 1]
    proj = lambda j: _dot(xn, w_ref[:, jobs[j][0]:jobs[j][0] + chunk])
    nxt = proj(0)
    for j in range(len(jobs)):
        cur = nxt
        if j + 1 < len(jobs):
            nxt = proj(j + 1)
        jobs[j][2](jobs[j][1], cur)

    base = SSM_D_INNER + SSM_CONV_DIM
    dtl = _dot(xn, w_ref[:, base:base + SSM_DT_PAD]) + dtb_ref[...]
    dt = jnp.maximum(dtl, 0.0) + jnp.log(1.0 + jnp.exp(-jnp.abs(dtl)))
    a = dt * (-jnp.exp(alog_ref[...]))
    tri = tri_ref[...]
    for c in range(tm // Q):
        a1, a2, a3 = _split3(a[Q * c:Q * (c + 1)])
        cum = _dot(tri, a1) + _dot(tri, a2) + _dot(tri, a3)
        cum_ref[Q * c:Q * (c + 1), :] = cum
        cumt_ref[c] = cum.T
        dtt_ref[c] = dt[Q * c:Q * (c + 1)].T


def _ssm_inproj(h, nw, w, cw, cb, dtb, alog, tri, S, tm=512):
    T = h.shape[0]
    Q = SSM_CHUNK
    row = lambda width: pl.BlockSpec((tm, width), lambda i: (i, 0))
    full = lambda a: pl.BlockSpec(a.shape, lambda i: (0,) * a.ndim)
    tile = pl.BlockSpec((tm // Q, SSM_DT_PAD, Q), lambda i: (i, 0, 0))
    tshape = jax.ShapeDtypeStruct((T // Q, SSM_DT_PAD, Q), F32)
    return pl.pallas_call(
        functools.partial(_ssm_inproj_kernel, tiles_per_seq=S // tm),
        grid=(T // tm,),
        in_specs=[row(D_MODEL), full(nw), full(w), full(cw), full(cb), full(dtb), full(alog), full(tri)],
        out_specs=[row(SSM_D_INNER), row(SSM_CONV_DIM), row(SSM_DT_PAD), tile, tile],
        out_shape=[jax.ShapeDtypeStruct((T, SSM_D_INNER), BF16), jax.ShapeDtypeStruct((T, SSM_CONV_DIM), BF16),
                   jax.ShapeDtypeStruct((T, SSM_DT_PAD), F32), tshape, tshape],
        scratch_shapes=[pltpu.VMEM((SSM_HALO, SSM_CONV_DIM), F32)],
        compiler_params=_cparams("arbitrary"),
        name="ssm_inproj",
    )(h, nw, w, cw, cb, dtb, alog, tri)


def _ssd_kernel(xbc_ref, zs_ref, cum_ref, cumt_ref, dtt_ref, dsk_ref, nw_ref, y_ref, state_ref):
    Q, P, N, G, HPG = SSM_CHUNK, SSM_P, SSM_N, SSM_G, SSM_HPG
    gw = SSM_D_INNER // G

    @pl.when(pl.program_id(1) == 0)
    def _():
        state_ref[...] = jnp.zeros_like(state_ref)

    cum = cum_ref[...]
    cum_t = cumt_ref[0]
    dt_t = dtt_ref[0]
    row_i = lax.broadcasted_iota(jnp.int32, (Q, Q), 0)
    col_i = lax.broadcasted_iota(jnp.int32, (Q, Q), 1)
    tril = row_i >= col_i
    eye = row_i == col_i
    b_off = SSM_D_INNER
    c_off = SSM_D_INNER + G * N

    def elementwise(g):
        cg = xbc_ref[:, c_off + N * g:c_off + N * (g + 1)]
        bg = xbc_ref[:, b_off + N * g:b_off + N * (g + 1)]
        cb = _dot_nt(cg, bg)
        cg_f = cg.astype(F32)
        bg_t = bg.astype(F32).T
        st_g = state_ref[g]
        st_b = st_g.astype(BF16)
        lhs, wgt, rhs, keep = [], [], [], []
        for hh in range(HPG):
            h = g * HPG + hh
            cum_b = jnp.broadcast_to(cum[:, h:h + 1], (Q, Q))
            cum_row = cum_t[h:h + 1, :]
            dt_row = dt_t[h:h + 1, :]
            cum_last = cum_row[:, Q - 1:Q]
            mm = cb * jnp.exp(jnp.where(tril, cum_b - cum_row, -jnp.inf)) * dt_row
            mm = jnp.where(eye, mm + dsk_ref[:, h:h + 1], mm)
            lhs += [mm.astype(BF16), (jnp.exp(cum_b) * cg_f).astype(BF16)]
            wgt.append((bg_t * (dt_row * jnp.exp(cum_last - cum_row))).astype(BF16))
            keep.append(jnp.broadcast_to(jnp.exp(cum_last), (1, P)))
        low = lax.broadcasted_iota(jnp.int32, (Q, 2 * P), 1) < P
        zero = jnp.zeros((Q, 2 * P), BF16)
        for pr in range(HPG // 2):
            xp = xbc_ref[:, 2 * P * (g * HPG // 2 + pr):2 * P * (g * HPG // 2 + pr + 1)]
            sp = st_b[:, 2 * P * pr:2 * P * (pr + 1)]
            rhs.append((jnp.where(low, xp, zero), jnp.where(low, sp, zero),
                        jnp.where(low, zero, xp), jnp.where(low, zero, sp)))
        return lhs, wgt, rhs, st_g * jnp.concatenate(keep, axis=1)

    def matmuls(g, ops):
        lhs, wgt, rhs, kept = ops
        ys, upd = [], []
        for pr in range(HPG // 2):
            x_lo, s_lo, x_hi, s_hi = rhs[pr]
            ys.append(_dot(jnp.concatenate(lhs[4 * pr:4 * pr + 4], axis=1),
                           jnp.concatenate([x_lo, s_lo, x_hi, s_hi], axis=0)))
            upd.append(_dot(jnp.concatenate(wgt[2 * pr:2 * pr + 2], axis=1),
                            jnp.concatenate([x_lo, x_hi], axis=0)))
        state_ref[g] = kept + jnp.concatenate(upd, axis=1)
        yg = jnp.concatenate(ys, axis=1) * zs_ref[:, gw * g:gw * (g + 1)].astype(F32)
        yg = yg * lax.rsqrt(jnp.mean(yg * yg, axis=-1, keepdims=True) + EPS)
        y_ref[:, gw * g:gw * (g + 1)] = (yg * nw_ref[:, gw * g:gw * (g + 1)]).astype(BF16)

    ops = elementwise(0)
    for g in range(G):
        nxt = elementwise(g + 1) if g + 1 < G else None
        matmuls(g, ops)
        ops = nxt


def _ssd(xbc, zs, cum, cumt, dtt, dsk, nw, B, S):
    Q = SSM_CHUNK
    nch = S // Q
    row = lambda width: pl.BlockSpec((Q, width), lambda b, c: (b * nch + c, 0))
    full = lambda a: pl.BlockSpec(a.shape, lambda b, c: (0,) * a.ndim)
    tile = pl.BlockSpec((1, SSM_DT_PAD, Q), lambda b, c: (b * nch + c, 0, 0))
    return pl.pallas_call(
        _ssd_kernel,
        grid=(B, nch),
        in_specs=[row(SSM_CONV_DIM), row(SSM_D_INNER), row(SSM_DT_PAD), tile, tile, full(dsk), full(nw)],
        out_specs=row(SSM_D_INNER),
        out_shape=jax.ShapeDtypeStruct((B * S, SSM_D_INNER), BF16),
        scratch_shapes=[pltpu.VMEM((SSM_G, SSM_N, SSM_HPG * SSM_P), F32)],
        compiler_params=_cparams("arbitrary", "arbitrary"),
        name="ssd_scan",
    )(xbc, zs, cum, cumt, dtt, dsk, nw)


def _block_diag_mean():
    i = np.arange(LANES)
    return jnp.asarray((i[:, None] // HEAD_DIM == i[None, :] // HEAD_DIM) / HEAD_DIM, BF16)


def _overlap_t(S):
    nc = (S - CMP_BLOCK) // CMP_STRIDE + 1
    ncp = S // CMP_STRIDE
    nblk = S // SEL_BLOCK
    starts = np.arange(ncp) * CMP_STRIDE
    js = np.arange(nblk)[:, None] * SEL_BLOCK
    ov = (starts[None, :] < js + SEL_BLOCK) & (starts[None, :] + CMP_BLOCK > js) & (np.arange(ncp)[None, :] < nc)
    out = np.zeros((LANES, ncp), np.float32)
    out[:nblk] = ov
    return jnp.asarray(out, BF16)


def _pad_cols(w, width):
    return jnp.pad(w, ((0, 0), (0, width - w.shape[1])))


def _nsa_layer(h, tabs, tabs_t, tabs_c, B, S, nw, w_in, q_norm, k_norm, cmp_pe, cmp_w1, cmp_b1, cmp_w2, w_out):
    G, dh, kvw = NSA_G, HEAD_DIM, NSA_KV_WIDTH
    ncp = S // CMP_STRIDE
    cut = lambda i: w_in[:, NSA_Q_WIDTH + i * kvw:NSA_Q_WIDTH + (i + 1) * kvw]
    wqt = w_in[:, :NSA_Q_WIDTH].T.astype(BF16)
    wk = jnp.concatenate([cut(0), cut(1), cut(2), cut(4)], axis=1).astype(BF16)
    wvt = jnp.concatenate([cut(3), cut(5)], axis=1).T.astype(BF16)
    wgt = _pad_cols(w_in[:, NSA_Q_WIDTH + 6 * kvw:], NSA_GATE_PAD).T.astype(BF16)
    qn = jnp.broadcast_to(q_norm[:, None], (dh, NSA_TQ))
    ksn = jnp.tile(k_norm[1], G)[None, :]
    kwn = jnp.tile(k_norm[2], G)[None, :]
    qt, kc_raw, vc_raw, ks, kw, vst, vwt, gt = _nsa_inproj(h, nw[None, :], wqt, wk, wvt, wgt, tabs, tabs_t,
                                                           qn, ksn, kwn, _block_diag_mean(), S)
    half = CMP_STRIDE * dh
    pe = cmp_pe.reshape(2, 2, half)
    w1 = cmp_w1.reshape(2, 2, half, CMP_HIDDEN).astype(BF16)
    b1 = cmp_b1[:, None, :]
    w2k = _pad_cols(cmp_w2[0], LANES).astype(BF16)
    w2vt = cmp_w2[1].T.astype(BF16)
    knw = _pad_cols(k_norm[0][None, :], LANES)
    kc, vct = _nsa_compress(kc_raw, vc_raw, pe, w1, b1, w2k, w2vt, knw, tabs_c)

    o = _nsa_attn(qt, kc, vct, ks, vst, kw, vwt, gt, _overlap_t(S), B, S)
    return o, w_out.astype(BF16)


def _ssd_layer(h, B, S, nw, w_in, conv_w, conv_b, dt_bias, a_log, d_skip, norm_w, w_out):
    w = _pad_cols(w_in, SSM_IN_PAD).astype(BF16)
    pad1 = lambda v: _pad_cols(v[None, :], SSM_DT_PAD)
    tri = jnp.asarray(np.tril(np.ones((SSM_CHUNK, SSM_CHUNK), np.float32)), BF16)
    zs, xbc, cum, cumt, dtt = _ssm_inproj(h, nw[None, :], w, conv_w, conv_b[None, :], pad1(dt_bias), pad1(a_log),
                                          tri, S)
    y = _ssd(xbc, zs, cum, cumt, dtt, pad1(d_skip), norm_w[None, :], B, S)
    return y, w_out.astype(BF16)


def kernel(x, positions, mix_norm_w, ffn_norm_w, ffn_w_gate, ffn_w_up, ffn_w_down, nsa_w_in, nsa_q_norm, nsa_k_norm, nsa_cmp_pe, nsa_cmp_w1, nsa_cmp_b1, nsa_cmp_w2, nsa_w_out, ssm_w_in, ssm_conv_w, ssm_conv_b, ssm_dt_bias, ssm_a_log, ssm_d, ssm_norm_w, ssm_w_out):
    B, S, D = x.shape
    T = B * S
    h = x.reshape(T, D)
    ncp = S // CMP_STRIDE
    tabs = _rope_tables(positions.reshape(T, 1), 1024)
    tabs_t = _rope_tables_t(positions.reshape(1, T), 2048)
    pos_c = jnp.pad(positions[:, CMP_BLOCK - 1::CMP_STRIDE], ((0, 0), (0, 1)))[:, :ncp]
    tabs_c = _rope_tables(pos_c.reshape(B * ncp, 1), ncp)
    for i in range(DEPTH):
        j = i // 2
        if i % 2 == 0:
            y, wo = _nsa_layer(h, tabs, tabs_t, tabs_c, B, S, mix_norm_w[i], nsa_w_in[j], nsa_q_norm[j],
                               nsa_k_norm[j], nsa_cmp_pe[j], nsa_cmp_w1[j], nsa_cmp_b1[j], nsa_cmp_w2[j],
                               nsa_w_out[j])
        else:
            y, wo = _ssd_layer(h, B, S, mix_norm_w[i], ssm_w_in[j], ssm_conv_w[j], ssm_conv_b[j], ssm_dt_bias[j],
                               ssm_a_log[j], ssm_d[j], ssm_norm_w[j], ssm_w_out[j])
        h = _ffn(h, y, wo, ffn_norm_w[i][None, :], ffn_w_gate[i].astype(BF16), ffn_w_up[i].astype(BF16),
                 ffn_w_down[i].astype(BF16))
    return h.reshape(B, S, D)
```

```python
import functools
import math

import numpy as np
import jax
import jax.numpy as jnp
from jax import lax
from jax.experimental import pallas as pl
from jax.experimental.pallas import tpu as pltpu

F32 = jnp.float32
BF16 = jnp.bfloat16

D_MODEL = 1024
DEPTH = 4
EPS = 1e-6

NSA_HEADS = 16
NSA_G = 4
NSA_HPG = NSA_HEADS // NSA_G
HEAD_DIM = 64
CMP_BLOCK = 32
CMP_STRIDE = 16
CMP_HIDDEN = 256
SEL_BLOCK = 64
SEL_SHIFT = 6
SEL_TOPK = 8
WINDOW = 512
ROPE_THETA = 500000.0
ROPE_DIM = HEAD_DIM // 4
ROPE_HALF = ROPE_DIM // 2
NSA_Q_WIDTH = NSA_HEADS * HEAD_DIM
NSA_KV_WIDTH = NSA_G * HEAD_DIM
NSA_GATE_PAD = 128
NSA_TQ = 256
V_PAD = 16
QSCALE = HEAD_DIM ** -0.5 * math.log2(math.e)

SSM_D_INNER = 2 * D_MODEL
SSM_P = 64
SSM_HEADS = SSM_D_INNER // SSM_P
SSM_G = 4
SSM_HPG = SSM_HEADS // SSM_G
SSM_N = 128
SSM_CONV = 4
SSM_CHUNK = 128
SSM_CONV_DIM = SSM_D_INNER + 2 * SSM_G * SSM_N
SSM_DT_PAD = 128
SSM_HALO = 8
SSM_IN_PAD = SSM_D_INNER + SSM_CONV_DIM + SSM_DT_PAD

FFN_HIDDEN = -(-8 * D_MODEL // (3 * 256)) * 256
FFN_PIECE = 768

LANES = 128
VMEM_LIMIT_BYTES = 52 * 1024 * 1024

MASK_BIG = 1e30
KEY_FORCED = 0x7F000000
M_INIT = -3e38

_NT = (((1,), (1,)), ((), ()))


def _cparams(*sem, flags=None):
    return pltpu.CompilerParams(dimension_semantics=sem, vmem_limit_bytes=VMEM_LIMIT_BYTES, flags=flags)


def _dot(a, b):
    return jnp.dot(a, b, preferred_element_type=F32)


def _dot_nt(a, b):
    return lax.dot_general(a, b, _NT, preferred_element_type=F32)


def _split3(x):
    a = x.astype(BF16)
    r = x - a.astype(F32)
    b = r.astype(BF16)
    c = (r - b.astype(F32)).astype(BF16)
    return a, b, c


def _rms_rows(x, w):
    return x * lax.rsqrt(jnp.mean(x * x, axis=-1, keepdims=True) + EPS) * w


def _silu(x):
    h = 0.5 * x
    return h + h * jnp.tanh(h)


def _rope_table_kernel(pos_ref, c_ref, sa_ref, sb_ref):
    pos = pos_ref[...].astype(F32)
    lane = lax.broadcasted_iota(jnp.int32, (1, LANES), 1)
    d = lane & (HEAD_DIM - 1)
    f = d & (ROPE_HALF - 1)
    inv = jnp.zeros((1, LANES), F32)
    for i in range(ROPE_HALF):
        inv = jnp.where(f == i, float(np.power(np.float32(ROPE_THETA), np.float32(-i / ROPE_HALF))), inv)
    ang = pos * inv
    cos, sin = jnp.cos(ang), jnp.sin(ang)
    c_ref[...] = jnp.where(d < ROPE_DIM, cos, 1.0)
    sa_ref[...] = jnp.where(d < ROPE_HALF, -sin, 0.0)
    sb_ref[...] = jnp.where((d >= ROPE_HALF) & (d < ROPE_DIM), sin, 0.0)


def _rope_tables(pos_col, tm):
    n = pos_col.shape[0]
    out = jax.ShapeDtypeStruct((n, LANES), F32)
    spec = pl.BlockSpec((tm, LANES), lambda i: (i, 0))
    return pl.pallas_call(
        _rope_table_kernel,
        grid=(n // tm,),
        in_specs=[pl.BlockSpec((tm, 1), lambda i: (i, 0))],
        out_specs=[spec, spec, spec],
        out_shape=[out, out, out],
        compiler_params=_cparams("parallel"),
        name="rope_tables",
    )(pos_col)


def _rope_table_t_kernel(pos_ref, c_ref, s_ref):
    pos = pos_ref[...].astype(F32)
    f = lax.broadcasted_iota(jnp.int32, (ROPE_HALF, 1), 0)
    inv = jnp.zeros((ROPE_HALF, 1), F32)
    for i in range(ROPE_HALF):
        inv = jnp.where(f == i, float(np.power(np.float32(ROPE_THETA), np.float32(-i / ROPE_HALF))), inv)
    ang = inv * pos
    c_ref[...] = jnp.cos(ang)
    s_ref[...] = jnp.sin(ang)


def _rope_tables_t(pos_row, tm):
    n = pos_row.shape[1]
    out = jax.ShapeDtypeStruct((ROPE_HALF, n), F32)
    spec = pl.BlockSpec((ROPE_HALF, tm), lambda i: (0, i))
    return pl.pallas_call(
        _rope_table_t_kernel,
        grid=(n // tm,),
        in_specs=[pl.BlockSpec((1, tm), lambda i: (0, i))],
        out_specs=[spec, spec],
        out_shape=[out, out],
        compiler_params=_cparams("parallel"),
        name="rope_tables_t",
    )(pos_row)


def _rope_lanes(x, c, sa, sb):
    return x * c + pltpu.roll(x, LANES - ROPE_HALF, 1) * sa + pltpu.roll(x, ROPE_HALF, 1) * sb


def _head_norm_rope(y, w, bd, c, sa, sb, scale):
    outs = []
    for j in range(y.shape[1] // LANES):
        yc = y[:, LANES * j:LANES * (j + 1)]
        sq = yc * yc
        hi = sq.astype(BF16)
        lo = (sq - hi.astype(F32)).astype(BF16)
        ms = _dot(hi, bd) + _dot(lo, bd)
        yn = yc * lax.rsqrt(ms + EPS) * w[:, LANES * j:LANES * (j + 1)]
        outs.append(_rope_lanes(yn, c, sa, sb) * scale)
    return jnp.concatenate(outs, axis=1)


def _nsa_inproj_kernel(x_ref, nw_ref, wqt_ref, wk_ref, wvt_ref, wgt_ref, c_ref, sa_ref, sb_ref, ct_ref, st_ref,
                       qn_ref, ksn_ref, kwn_ref, bd_ref,
                       qt_ref, kc_ref, vc_ref, ks_ref, kw_ref, vst_ref, vwt_ref, gt_ref, raw_ref):
    xn = _rms_rows(x_ref[...], nw_ref[...]).astype(BF16)
    c, sa, sb, bd = c_ref[...], sa_ref[...], sb_ref[...], bd_ref[...]
    kvw, dh, tq = NSA_KV_WIDTH, HEAD_DIM, NSA_TQ
    nrow = x_ref.shape[0] // CMP_STRIDE

    qn = qn_ref[...]

    def raw_out(raw):
        for j in range(2 * kvw // LANES):
            raw_ref[j] = raw[:, LANES * j:LANES * (j + 1)]
        for which, out_ref in enumerate((kc_ref, vc_ref)):
            for g in range(NSA_G):
                j, off = divmod(which * kvw + dh * g, LANES)
                for l in range(CMP_STRIDE):
                    rows = raw_ref[j, pl.ds(l, nrow, stride=CMP_STRIDE), :]
                    out_ref[0, g, :, dh * l:dh * (l + 1)] = rows[:, off:off + dh]

    def key_out(out_ref, gain_ref):
        def write(y):
            out_ref[...] = _head_norm_rope(y, gain_ref[...], bd, c, sa, sb, 1.0).astype(BF16)
        return write

    def q_out(ch):
        def write(yt):
            cos, sin = ct_ref[:, ch * tq:(ch + 1) * tq], st_ref[:, ch * tq:(ch + 1) * tq]
            for hd in range(NSA_HEADS):
                yh = yt[dh * hd:dh * (hd + 1)]
                yn = yh * lax.rsqrt(jnp.mean(yh * yh, axis=0, keepdims=True) + EPS) * qn
                x1, x2 = yn[0:ROPE_HALF], yn[ROPE_HALF:ROPE_DIM]
                rot = jnp.concatenate([x1 * cos - x2 * sin, x2 * cos + x1 * sin, yn[ROPE_DIM:]], axis=0)
                qt_ref[ch, dh * hd:dh * (hd + 1), :] = (rot * QSCALE).astype(BF16)
        return write

    def v_out(ch):
        def write(vt):
            vst_ref[ch] = vt[0:kvw].astype(BF16)
            vwt_ref[ch] = vt[kvw:2 * kvw].astype(BF16)
        return write

    def g_out(ch):
        def write(gl):
            gt_ref[ch] = 1.0 / (1.0 + jnp.exp(-gl))
        return write

    jobs = [(lambda: _dot(xn, wk_ref[:, 0:2 * kvw]), raw_out),
            (lambda: _dot(xn, wk_ref[:, 2 * kvw:3 * kvw]), key_out(ks_ref, ksn_ref)),
            (lambda: _dot(xn, wk_ref[:, 3 * kvw:4 * kvw]), key_out(kw_ref, kwn_ref))]
    for ch in range(x_ref.shape[0] // tq):
        xc = xn[ch * tq:(ch + 1) * tq]
        jobs += [(functools.partial(_dot_nt, wqt_ref[...], xc), q_out(ch)),
                 (functools.partial(_dot_nt, wvt_ref[...], xc), v_out(ch)),
                 (functools.partial(_dot_nt, wgt_ref[...], xc), g_out(ch))]
    nxt = jobs[0][0]()
    for j, (_, epilogue) in enumerate(jobs):
        cur = nxt
        if j + 1 < len(jobs):
            nxt = jobs[j + 1][0]()
        epilogue(cur)


def _nsa_inproj(h, nw, wqt, wk, wvt, wgt, tabs, tabs_t, qn, ksn, kwn, bd, S, tm=1024):
    T = h.shape[0]
    tq, kvw = NSA_TQ, NSA_KV_WIDTH
    per_seq = S // tm
    width = CMP_STRIDE * HEAD_DIM
    chunked = pl.BlockSpec((1, NSA_G, tm // CMP_STRIDE, width), lambda i: (i // per_seq, 0, i % per_seq, 0))
    cshape = jax.ShapeDtypeStruct((T // S, NSA_G, S // CMP_STRIDE, width), F32)
    row = lambda width: pl.BlockSpec((tm, width), lambda i: (i, 0))
    full = lambda a: pl.BlockSpec(a.shape, lambda i: (0,) * a.ndim)
    colt = pl.BlockSpec((ROPE_HALF, tm), lambda i: (0, i))
    tile = lambda ch: pl.BlockSpec((tm // tq, ch, tq), lambda i: (i, 0, 0))
    tshape = lambda ch, dt: jax.ShapeDtypeStruct((T // tq, ch, tq), dt)
    c, sa, sb = tabs
    ct, st = tabs_t
    return pl.pallas_call(
        _nsa_inproj_kernel,
        grid=(T // tm,),
        in_specs=[row(D_MODEL), full(nw), full(wqt), full(wk), full(wvt), full(wgt),
                  row(LANES), row(LANES), row(LANES), colt, colt,
                  full(qn), full(ksn), full(kwn), full(bd)],
        out_specs=[tile(NSA_Q_WIDTH), chunked, chunked, row(kvw), row(kvw), tile(kvw), tile(kvw),
                   tile(NSA_GATE_PAD)],
        out_shape=[tshape(NSA_Q_WIDTH, BF16), cshape, cshape, jax.ShapeDtypeStruct((T, kvw), BF16),
                   jax.ShapeDtypeStruct((T, kvw), BF16), tshape(kvw, BF16), tshape(kvw, BF16),
                   tshape(NSA_GATE_PAD, F32)],
        scratch_shapes=[pltpu.VMEM((2 * kvw // LANES, tm, LANES), F32)],
        compiler_params=_cparams("parallel"),
        name="nsa_inproj",
    )(h, nw, wqt, wk, wvt, wgt, c, sa, sb, ct, st, qn, ksn, kwn, bd)


def _nsa_compress_kernel(xk_ref, xv_ref, pe_ref, w1_ref, b1_ref, w2k_ref, w2vt_ref, knw_ref,
                         c_ref, sa_ref, sb_ref, kc_ref, vct_ref):
    ncp = xk_ref.shape[2]

    def hidden(x, which):
        lo = _dot((x + pe_ref[which, 0:1, :]).astype(BF16), w1_ref[which, 0])
        hi = _dot((x + pe_ref[which, 1:2, :]).astype(BF16), w1_ref[which, 1])
        return _silu(lo + pltpu.roll(hi, ncp - 1, 0) + b1_ref[which]).astype(BF16)

    for g in range(NSA_G):
        kc = _dot(hidden(xk_ref[0, g], 0), w2k_ref[...])
        ms = jnp.sum(kc * kc, axis=-1, keepdims=True) * (1.0 / HEAD_DIM)
        kn = kc * lax.rsqrt(ms + EPS) * knw_ref[...]
        kn = _rope_lanes(kn, c_ref[...], sa_ref[...], sb_ref[...])
        kc_ref[0, g] = kn[:, :HEAD_DIM].astype(BF16)
        vct_ref[0, g] = _dot_nt(w2vt_ref[...], hidden(xv_ref[0, g], 1)).astype(BF16)


def _nsa_compress(xk, xv, pe, w1, b1, w2k, w2vt, knw, tabs):
    B, G, ncp, width = xk.shape
    c, sa, sb = tabs
    xspec = pl.BlockSpec((1, G, ncp, width), lambda b: (b, 0, 0, 0))
    tspec = pl.BlockSpec((ncp, LANES), lambda b: (b, 0))
    full = lambda a: pl.BlockSpec(a.shape, lambda b: (0,) * a.ndim)
    return pl.pallas_call(
        _nsa_compress_kernel,
        grid=(B,),
        in_specs=[xspec, xspec, full(pe), full(w1), full(b1), full(w2k), full(w2vt), full(knw),
                  tspec, tspec, tspec],
        out_specs=[pl.BlockSpec((1, G, ncp, HEAD_DIM), lambda b: (b, 0, 0, 0)),
                   pl.BlockSpec((1, G, HEAD_DIM, ncp), lambda b: (b, 0, 0, 0))],
        out_shape=[jax.ShapeDtypeStruct((B, G, ncp, HEAD_DIM), BF16),
                   jax.ShapeDtypeStruct((B, G, HEAD_DIM, ncp), BF16)],
        compiler_params=_cparams("parallel"),
        name="nsa_compress",
    )(xk, xv, pe, w1, b1, w2k, w2vt, knw, c, sa, sb)


def _flash_steps(jobs):
    n = len(jobs)
    ahead = 3
    score = lambda i: _dot(jobs[i][1], jobs[i][0])
    scores = [score(i) for i in range(min(ahead, n))]
    out = []
    for i, (_, _, vt, state, mask) in enumerate(jobs):
        m, acc = out[state] if isinstance(state, int) else state
        s = scores[i]
        if mask is not None:
            s = jnp.where(mask, s, -MASK_BIG)
        m_new = jnp.maximum(m, jnp.max(s, axis=0, keepdims=True))
        p = jnp.exp2(s - m_new).astype(BF16)
        out.append((m_new, jnp.exp2(m - m_new) * acc + _dot(vt, p)))
        if i + ahead < n:
            scores.append(score(i + ahead))
    return out


def _nsa_attn_kernel(qt_ref, kc_ref, vct_ref, ks_ref, vst_ref, kw_ref, vwt_ref, gt_ref, ovl_ref, o_ref,
                     ksa_ref, kwa_ref, qat_ref, part_ref):
    S = ks_ref.shape[0]
    ncp = kc_ref.shape[2]
    nblk = S // SEL_BLOCK
    G, HP, dh, tq = NSA_G, NSA_HPG, HEAD_DIM, NSA_TQ
    cols = HP * tq
    qi = pl.program_id(1)
    q0 = qi * tq

    @pl.when(qi == 0)
    def _():
        rblk = lax.broadcasted_iota(jnp.int32, (S, dh), 0) >> SEL_SHIFT
        lane = lax.broadcasted_iota(jnp.int32, (S, dh), 1)
        onehot = jnp.where(rblk == lane, 1.0, 0.0).astype(BF16)
        zeros = jnp.zeros((S, dh), BF16)
        for g in range(G):
            ksa_ref[g, :, 0:dh] = ks_ref[:, dh * g:dh * (g + 1)]
            ksa_ref[g, :, dh:2 * dh] = onehot
            kwa_ref[g, :, 0:dh] = kw_ref[:, dh * g:dh * (g + 1)]
            kwa_ref[g, :, dh:2 * dh] = zeros

    t_cols = q0 + (lax.broadcasted_iota(jnp.int32, (1, cols), 1) & (tq - 1))
    t_q = t_cols[:, 0:tq]
    k_loc = lax.broadcasted_iota(jnp.int32, (tq, 1), 0)

    cmp_end = lax.broadcasted_iota(jnp.int32, (ncp, 1), 0) * CMP_STRIDE + (CMP_BLOCK - 1)
    cmask = cmp_end <= t_cols
    jb = lax.broadcasted_iota(jnp.int32, (nblk, tq), 0)
    jb_col = lax.broadcasted_iota(jnp.int32, (nblk, 1), 0)
    tblk = t_q >> SEL_SHIFT
    forced = (jb == 0) | (jb == tblk) | (jb == tblk - 1)
    gt = gt_ref[0]

    def gate_row(g, branch):
        return jnp.concatenate([gt[3 * (g * HP + h) + branch:3 * (g * HP + h) + branch + 1] for h in range(HP)],
                               axis=1)

    init = (jnp.full((1, cols), M_INIT, F32), jnp.zeros((dh + V_PAD, cols), F32))
    ones_rows = jnp.where(lax.broadcasted_iota(jnp.int32, (V_PAD, tq), 0) == 0, 1.0, 0.0).astype(BF16)

    def v_aug(v_ref, kt, g):
        return jnp.concatenate([v_ref[kt, dh * g:dh * (g + 1), :], ones_rows], axis=0)

    def normalised(acc):
        return acc[0:dh] * (1.0 / acc[dh:dh + 1])

    causal = (q0 + k_loc) <= t_cols
    n_back = (WINDOW + tq - 1) // tq
    back = []
    c_loc = t_cols - q0
    for dk in range(1, n_back + 1):
        kt = qi - dk
        far = jnp.where(kt < 0, 2 * WINDOW + S, 0)
        back.append((jnp.maximum(kt, 0), (c_loc + (dk * tq - WINDOW + far)) < k_loc))

    grp = range(G)
    for g in grp:
        qat_ref[g, dh + nblk:2 * dh, :] = jnp.zeros((dh - nblk, cols), BF16)
        for h in range(HP):
            hd = g * HP + h
            qat_ref[g, 0:dh, h * tq:(h + 1) * tq] = qt_ref[0, dh * hd:dh * (hd + 1), :]

    sc = [_dot(kc_ref[0, g], qat_ref[g, 0:dh, :]) for g in grp]
    pc = []
    for g in grp:
        s = jnp.where(cmask, sc[g], -MASK_BIG)
        m = jnp.max(s, axis=0, keepdims=True)
        p = jnp.where(cmask, jnp.exp2(s - m), 0.0)
        l = jnp.sum(p, axis=0, keepdims=True)
        pc.append(p * jnp.where(l > 0.0, 1.0 / l, 0.0))
    oc = [_dot(vct_ref[0, g], pc[g].astype(BF16)) for g in grp]
    ovl = ovl_ref[...]
    imp = []
    for g in grp:
        psum = pc[g][:, 0:tq]
        for h in range(1, HP):
            psum = psum + pc[g][:, h * tq:(h + 1) * tq]
        p1, p2, p3 = _split3(psum)
        imp.append((_dot(ovl, p1) + _dot(ovl, p2) + _dot(ovl, p3))[0:nblk])
    for g in grp:
        key = jnp.where(forced, KEY_FORCED, jnp.where(jb > tblk, -1, pltpu.bitcast(imp[g], jnp.int32)))
        cnt = jnp.zeros((nblk, tq), jnp.int32)
        for j in range(nblk):
            rj = key[j:j + 1, :] + jnp.where(jb_col > j, 1, 0)
            cnt = cnt + jnp.where(rj > key, 1, 0)
        selneg = jnp.where(cnt < SEL_TOPK, 0.0, -MASK_BIG).astype(BF16)
        for h in range(HP):
            qat_ref[g, dh:dh + nblk, h * tq:(h + 1) * tq] = selneg
    qats = [qat_ref[g] for g in grp]

    jobs = [(qats[g], kwa_ref[g, pl.ds(q0, tq), :], v_aug(vwt_ref, qi, g), init, causal) for g in grp]
    jobs += [(qats[g], ksa_ref[g, pl.ds(q0, tq), :], v_aug(vst_ref, qi, g), init, causal) for g in grp]
    last_win = list(grp)
    for kt, inside in back:
        k0 = pl.multiple_of(kt * tq, tq)
        for g in grp:
            jobs.append((qats[g], kwa_ref[g, pl.ds(k0, tq), :], v_aug(vwt_ref, kt, g), last_win[g], inside))
            last_win[g] = len(jobs) - 1
    res = _flash_steps(jobs)
    for g in grp:
        part_ref[g] = gate_row(g, 0) * oc[g] + gate_row(g, 2) * normalised(res[last_win[g]][1])

    def sel_body(kt, states):
        k0 = pl.multiple_of(kt * tq, tq)
        return tuple(_flash_steps([(qat_ref[g], ksa_ref[g, pl.ds(k0, tq), :], v_aug(vst_ref, kt, g), states[g], None)
                                   for g in grp]))

    sel_state = lax.fori_loop(0, qi, sel_body, tuple(res[G + g] for g in grp))

    for g in range(G):
        og = part_ref[g] + gate_row(g, 1) * normalised(sel_state[g][1])
        og_t = jnp.concatenate([og[:, h * tq:(h + 1) * tq] for h in range(HP)], axis=0)
        o_ref[:, HP * dh * g:HP * dh * (g + 1)] = og_t.T.astype(BF16)


def _nsa_attn(qt, kc, vct, ks, vst, kw, vwt, gt, ovl, B, S):
    T = B * S
    tq = NSA_TQ
    nq = S // tq
    G, dh = NSA_G, HEAD_DIM
    ncp = kc.shape[2]
    qspec = pl.BlockSpec((1, NSA_Q_WIDTH, tq), lambda b, i: (b * nq + i, 0, 0))
    gspec = pl.BlockSpec((1, NSA_GATE_PAD, tq), lambda b, i: (b * nq + i, 0, 0))
    kcspec = pl.BlockSpec((1, G, ncp, dh), lambda b, i: (b, 0, 0, 0))
    vcspec = pl.BlockSpec((1, G, dh, ncp), lambda b, i: (b, 0, 0, 0))
    kspec = pl.BlockSpec((S, NSA_KV_WIDTH), lambda b, i: (b, 0))
    vspec = pl.BlockSpec((nq, NSA_KV_WIDTH, tq), lambda b, i: (b, 0, 0))
    ovspec = pl.BlockSpec(ovl.shape, lambda b, i: (0, 0))
    return pl.pallas_call(
        _nsa_attn_kernel,
        grid=(B, nq),
        in_specs=[qspec, kcspec, vcspec, kspec, vspec, kspec, vspec, gspec, ovspec],
        out_specs=pl.BlockSpec((tq, NSA_Q_WIDTH), lambda b, i: (b * nq + i, 0)),
        out_shape=jax.ShapeDtypeStruct((T, NSA_Q_WIDTH), BF16),
        scratch_shapes=[pltpu.VMEM((G, S, 2 * dh), BF16), pltpu.VMEM((G, S, 2 * dh), BF16),
                        pltpu.VMEM((G, 2 * dh, NSA_HPG * tq), BF16),
                        pltpu.VMEM((G, dh, NSA_HPG * tq), F32)],
        compiler_params=_cparams("arbitrary", "arbitrary"),
        name="nsa_attn",
    )(qt, kc, vct, ks, vst, kw, vwt, gt, ovl)


def _ffn_kernel(x_ref, y_ref, wo_ref, nw_ref, wg_ref, wu_ref, wd_ref, o_ref):
    h1 = x_ref[...] + _dot(y_ref[...], wo_ref[...])
    xn = _rms_rows(h1, nw_ref[...]).astype(BF16)
    hidden = wg_ref.shape[1]
    cuts = list(range(0, hidden, FFN_PIECE)) + [hidden]
    pieces = list(zip(cuts[:-1], cuts[1:]))
    gate_up = lambda a, b: (_dot(xn, wg_ref[:, a:b]), _dot(xn, wu_ref[:, a:b]))
    nxt = gate_up(*pieces[0])
    out = h1
    for i, (a, b) in enumerate(pieces):
        g, u = nxt
        if i + 1 < len(pieces):
            nxt = gate_up(*pieces[i + 1])
        out = out + _dot((_silu(g) * u).astype(BF16), wd_ref[a:b, :])
    o_ref[...] = out


def _ffn(h, y, wo, nw, wg, wu, wd, tm=512):
    T = h.shape[0]
    H = wg.shape[1]
    K = y.shape[1]
    row = lambda width: pl.BlockSpec((tm, width), lambda i: (i, 0))
    resident = lambda a: pl.BlockSpec(a.shape, lambda i: (0,) * a.ndim, pipeline_mode=pl.Buffered(1))
    return pl.pallas_call(
        _ffn_kernel,
        grid=(T // tm,),
        in_specs=[row(D_MODEL), row(K), resident(wo), resident(nw), resident(wg), resident(wu), resident(wd)],
        out_specs=row(D_MODEL),
        out_shape=jax.ShapeDtypeStruct((T, D_MODEL), F32),
        compiler_params=_cparams("parallel"),
        name="ffn",
    )(h, y, wo, nw, wg, wu, wd)


def _ssm_inproj_kernel(x_ref, nw_ref, w_ref, cw_ref, cb_ref, dtb_ref, alog_ref, tri_ref,
                       zs_ref, xbc_ref, cum_ref, cumt_ref, dtt_ref, ext_ref, *, tiles_per_seq):
    tm = x_ref.shape[0]
    halo, Q = SSM_HALO, SSM_CHUNK
    chunk = 256
    xn = _rms_rows(x_ref[...], nw_ref[...]).astype(BF16)

    @pl.when(pl.program_id(0) % tiles_per_seq == 0)
    def _():
        ext_ref[...] = jnp.zeros_like(ext_ref)

    row = lax.broadcasted_iota(jnp.int32, (halo, chunk), 0)

    def gate_out(lo, y):
        zs_ref[:, lo:lo + chunk] = _silu(y).astype(BF16)

    def conv_out(lo, x):
        prev = ext_ref[:, lo:lo + chunk]
        acc = cb_ref[:, lo:lo + chunk] + cw_ref[SSM_CONV - 1:SSM_CONV, lo:lo + chunk] * x
        for k in range(SSM_CONV - 1):
            sh = SSM_CONV - 1 - k
            r = pltpu.roll(x, sh, 0)
            top = jnp.where(row < sh, pltpu.roll(prev, sh, 0), r[0:halo])
            acc = acc + cw_ref[k:k + 1, lo:lo + chunk] * jnp.concatenate([top, r[halo:]], axis=0)
        ext_ref[:, lo:lo + chunk] = x[tm - halo:tm]
        xbc_ref[:, lo:lo + chunk] = _silu(acc).astype(BF16)

    gate_jobs = [(lo, lo, gate_out) for lo in range(0, SSM_D_INNER, chunk)]
    conv_jobs = [(SSM_D_INNER + lo, lo, conv_out) for lo in range(0, SSM_CONV_DIM, chunk)]
    jobs = []
    for j in range(max(len(gate_jobs), len(conv_jobs))):
        jobs += conv_jobs[j:j + 1] + gate_jobs[j:j + 1]
    proj = lambda j: _dot(xn, w_ref[:, jobs[j][0]:jobs[j][0] + chunk])
    nxt = proj(0)
    for j in range(len(jobs)):
        cur = nxt
        if j + 1 < len(jobs):
            nxt = proj(j + 1)
        jobs[j][2](jobs[j][1], cur)

    base = SSM_D_INNER + SSM_CONV_DIM
    dtl = _dot(xn, w_ref[:, base:base + SSM_DT_PAD]) + dtb_ref[...]
    dt = jnp.maximum(dtl, 0.0) + jnp.log(1.0 + jnp.exp(-jnp.abs(dtl)))
    a = dt * (-jnp.exp(alog_ref[...]))
    tri = tri_ref[...]
    for c in range(tm // Q):
        a1, a2, a3 = _split3(a[Q * c:Q * (c + 1)])
        cum = _dot(tri, a1) + _dot(tri, a2) + _dot(tri, a3)
        cum_ref[Q * c:Q * (c + 1), :] = cum
        cumt_ref[c] = cum.T
        dtt_ref[c] = dt[Q * c:Q * (c + 1)].T


def _ssm_inproj(h, nw, w, cw, cb, dtb, alog, tri, S, tm=512):
    T = h.shape[0]
    Q = SSM_CHUNK
    row = lambda width: pl.BlockSpec((tm, width), lambda i: (i, 0))
    full = lambda a: pl.BlockSpec(a.shape, lambda i: (0,) * a.ndim)
    tile = pl.BlockSpec((tm // Q, SSM_DT_PAD, Q), lambda i: (i, 0, 0))
    tshape = jax.ShapeDtypeStruct((T // Q, SSM_DT_PAD, Q), F32)
    return pl.pallas_call(
        functools.partial(_ssm_inproj_kernel, tiles_per_seq=S // tm),
        grid=(T // tm,),
        in_specs=[row(D_MODEL), full(nw), full(w), full(cw), full(cb), full(dtb), full(alog), full(tri)],
        out_specs=[row(SSM_D_INNER), row(SSM_CONV_DIM), row(SSM_DT_PAD), tile, tile],
        out_shape=[jax.ShapeDtypeStruct((T, SSM_D_INNER), BF16), jax.ShapeDtypeStruct((T, SSM_CONV_DIM), BF16),
                   jax.ShapeDtypeStruct((T, SSM_DT_PAD), F32), tshape, tshape],
        scratch_shapes=[pltpu.VMEM((SSM_HALO, SSM_CONV_DIM), F32)],
        compiler_params=_cparams("arbitrary"),
        name="ssm_inproj",
    )(h, nw, w, cw, cb, dtb, alog, tri)


def _ssd_kernel(xbc_ref, zs_ref, cum_ref, cumt_ref, dtt_ref, dsk_ref, nw_ref, y_ref, state_ref):
    Q, P, N, G, HPG = SSM_CHUNK, SSM_P, SSM_N, SSM_G, SSM_HPG
    gw = SSM_D_INNER // G

    @pl.when(pl.program_id(1) == 0)
    def _():
        state_ref[...] = jnp.zeros_like(state_ref)

    cum = cum_ref[...]
    cum_t = cumt_ref[0]
    dt_t = dtt_ref[0]
    row_i = lax.broadcasted_iota(jnp.int32, (Q, Q), 0)
    col_i = lax.broadcasted_iota(jnp.int32, (Q, Q), 1)
    tril = row_i >= col_i
    eye = row_i == col_i
    b_off = SSM_D_INNER
    c_off = SSM_D_INNER + G * N

    def elementwise(g):
        cg = xbc_ref[:, c_off + N * g:c_off + N * (g + 1)]
        bg = xbc_ref[:, b_off + N * g:b_off + N * (g + 1)]
        cb = _dot_nt(cg, bg)
        cg_f = cg.astype(F32)
        bg_t = bg.astype(F32).T
        st_g = state_ref[g]
        st_b = st_g.astype(BF16)
        lhs, wgt, rhs, keep = [], [], [], []
        for hh in range(HPG):
            h = g * HPG + hh
            cum_b = jnp.broadcast_to(cum[:, h:h + 1], (Q, Q))
            cum_row = cum_t[h:h + 1, :]
            dt_row = dt_t[h:h + 1, :]
            cum_last = cum_row[:, Q - 1:Q]
            mm = cb * jnp.exp(jnp.where(tril, cum_b - cum_row, -jnp.inf)) * dt_row
            mm = jnp.where(eye, mm + dsk_ref[:, h:h + 1], mm)
            lhs += [mm.astype(BF16), (jnp.exp(cum_b) * cg_f).astype(BF16)]
            wgt.append((bg_t * (dt_row * jnp.exp(cum_last - cum_row))).astype(BF16))
            keep.append(jnp.broadcast_to(jnp.exp(cum_last), (1, P)))
        low = lax.broadcasted_iota(jnp.int32, (Q, 2 * P), 1) < P
        zero = jnp.zeros((Q, 2 * P), BF16)
        for pr in range(HPG // 2):
            xp = xbc_ref[:, 2 * P * (g * HPG // 2 + pr):2 * P * (g * HPG // 2 + pr + 1)]
            sp = st_b[:, 2 * P * pr:2 * P * (pr + 1)]
            rhs.append((jnp.where(low, xp, zero), jnp.where(low, sp, zero),
                        jnp.where(low, zero, xp), jnp.where(low, zero, sp)))
        return lhs, wgt, rhs, st_g * jnp.concatenate(keep, axis=1)

    def matmuls(g, ops):
        lhs, wgt, rhs, kept = ops
        ys, upd = [], []
        for pr in range(HPG // 2):
            x_lo, s_lo, x_hi, s_hi = rhs[pr]
            ys.append(_dot(jnp.concatenate(lhs[4 * pr:4 * pr + 4], axis=1),
                           jnp.concatenate([x_lo, s_lo, x_hi, s_hi], axis=0)))
            upd.append(_dot(jnp.concatenate(wgt[2 * pr:2 * pr + 2], axis=1),
                            jnp.concatenate([x_lo, x_hi], axis=0)))
        state_ref[g] = kept + jnp.concatenate(upd, axis=1)
        yg = jnp.concatenate(ys, axis=1) * zs_ref[:, gw * g:gw * (g + 1)].astype(F32)
        yg = yg * lax.rsqrt(jnp.mean(yg * yg, axis=-1, keepdims=True) + EPS)
        y_ref[:, gw * g:gw * (g + 1)] = (yg * nw_ref[:, gw * g:gw * (g + 1)]).astype(BF16)

    ops = elementwise(0)
    for g in range(G):
        nxt = elementwise(g + 1) if g + 1 < G else None
        matmuls(g, ops)
        ops = nxt


def _ssd(xbc, zs, cum, cumt, dtt, dsk, nw, B, S):
    Q = SSM_CHUNK
    nch = S // Q
    row = lambda width: pl.BlockSpec((Q, width), lambda b, c: (b * nch + c, 0))
    full = lambda a: pl.BlockSpec(a.shape, lambda b, c: (0,) * a.ndim)
    tile = pl.BlockSpec((1, SSM_DT_PAD, Q), lambda b, c: (b * nch + c, 0, 0))
    return pl.pallas_call(
        _ssd_kernel,
        grid=(B, nch),
        in_specs=[row(SSM_CONV_DIM), row(SSM_D_INNER), row(SSM_DT_PAD), tile, tile, full(dsk), full(nw)],
        out_specs=row(SSM_D_INNER),
        out_shape=jax.ShapeDtypeStruct((B * S, SSM_D_INNER), BF16),
        scratch_shapes=[pltpu.VMEM((SSM_G, SSM_N, SSM_HPG * SSM_P), F32)],
        compiler_params=_cparams("arbitrary", "arbitrary"),
        name="ssd_scan",
    )(xbc, zs, cum, cumt, dtt, dsk, nw)


def _block_diag_mean():
    i = np.arange(LANES)
    return jnp.asarray((i[:, None] // HEAD_DIM == i[None, :] // HEAD_DIM) / HEAD_DIM, BF16)


def _overlap_t(S):
    nc = (S - CMP_BLOCK) // CMP_STRIDE + 1
    ncp = S // CMP_STRIDE
    nblk = S // SEL_BLOCK
    starts = np.arange(ncp) * CMP_STRIDE
    js = np.arange(nblk)[:, None] * SEL_BLOCK
    ov = (starts[None, :] < js + SEL_BLOCK) & (starts[None, :] + CMP_BLOCK > js) & (np.arange(ncp)[None, :] < nc)
    out = np.zeros((LANES, ncp), np.float32)
    out[:nblk] = ov
    return jnp.asarray(out, BF16)


def _pad_cols(w, width):
    return jnp.pad(w, ((0, 0), (0, width - w.shape[1])))


def _nsa_layer(h, tabs, tabs_t, tabs_c, B, S, nw, w_in, q_norm, k_norm, cmp_pe, cmp_w1, cmp_b1, cmp_w2, w_out):
    G, dh, kvw = NSA_G, HEAD_DIM, NSA_KV_WIDTH
    ncp = S // CMP_STRIDE
    cut = lambda i: w_in[:, NSA_Q_WIDTH + i * kvw:NSA_Q_WIDTH + (i + 1) * kvw]
    wqt = w_in[:, :NSA_Q_WIDTH].T.astype(BF16)
    wk = jnp.concatenate([cut(0), cut(1), cut(2), cut(4)], axis=1).astype(BF16)
    wvt = jnp.concatenate([cut(3), cut(5)], axis=1).T.astype(BF16)
    wgt = _pad_cols(w_in[:, NSA_Q_WIDTH + 6 * kvw:], NSA_GATE_PAD).T.astype(BF16)
    qn = jnp.broadcast_to(q_norm[:, None], (dh, NSA_TQ))
    ksn = jnp.tile(k_norm[1], G)[None, :]
    kwn = jnp.tile(k_norm[2], G)[None, :]
    qt, kc_raw, vc_raw, ks, kw, vst, vwt, gt = _nsa_inproj(h, nw[None, :], wqt, wk, wvt, wgt, tabs, tabs_t,
                                                           qn, ksn, kwn, _block_diag_mean(), S)
    half = CMP_STRIDE * dh
    pe = cmp_pe.reshape(2, 2, half)
    w1 = cmp_w1.reshape(2, 2, half, CMP_HIDDEN).astype(BF16)
    b1 = cmp_b1[:, None, :]
    w2k = _pad_cols(cmp_w2[0], LANES).astype(BF16)
    w2vt = cmp_w2[1].T.astype(BF16)
    knw = _pad_cols(k_norm[0][None, :], LANES)
    kc, vct = _nsa_compress(kc_raw, vc_raw, pe, w1, b1, w2k, w2vt, knw, tabs_c)

    o = _nsa_attn(qt, kc, vct, ks, vst, kw, vwt, gt, _overlap_t(S), B, S)
    return o, w_out.astype(BF16)


def _ssd_layer(h, B, S, nw, w_in, conv_w, conv_b, dt_bias, a_log, d_skip, norm_w, w_out):
    w = _pad_cols(w_in, SSM_IN_PAD).astype(BF16)
    pad1 = lambda v: _pad_cols(v[None, :], SSM_DT_PAD)
    tri = jnp.asarray(np.tril(np.ones((SSM_CHUNK, SSM_CHUNK), np.float32)), BF16)
    zs, xbc, cum, cumt, dtt = _ssm_inproj(h, nw[None, :], w, conv_w, conv_b[None, :], pad1(dt_bias), pad1(a_log),
                                          tri, S)
    y = _ssd(xbc, zs, cum, cumt, dtt, pad1(d_skip), norm_w[None, :], B, S)
    return y, w_out.astype(BF16)


def kernel(x, positions, mix_norm_w, ffn_norm_w, ffn_w_gate, ffn_w_up, ffn_w_down, nsa_w_in, nsa_q_norm, nsa_k_norm, nsa_cmp_pe, nsa_cmp_w1, nsa_cmp_b1, nsa_cmp_w2, nsa_w_out, ssm_w_in, ssm_conv_w, ssm_conv_b, ssm_dt_bias, ssm_a_log, ssm_d, ssm_norm_w, ssm_w_out):
    B, S, D = x.shape
    T = B * S
    h = x.reshape(T, D)
    ncp = S // CMP_STRIDE
    tabs = _rope_tables(positions.reshape(T, 1), 1024)
    tabs_t = _rope_tables_t(positions.reshape(1, T), 2048)
    pos_c = jnp.pad(positions[:, CMP_BLOCK - 1::CMP_STRIDE], ((0, 0), (0, 1)))[:, :ncp]
    tabs_c = _rope_tables(pos_c.reshape(B * ncp, 1), ncp)
    for i in range(DEPTH):
        j = i // 2
        if i % 2 == 0:
            y, wo = _nsa_layer(h, tabs, tabs_t, tabs_c, B, S, mix_norm_w[i], nsa_w_in[j], nsa_q_norm[j],
                               nsa_k_norm[j], nsa_cmp_pe[j], nsa_cmp_w1[j], nsa_cmp_b1[j], nsa_cmp_w2[j],
                               nsa_w_out[j])
        else:
            y, wo = _ssd_layer(h, B, S, mix_norm_w[i], ssm_w_in[j], ssm_conv_w[j], ssm_conv_b[j], ssm_dt_bias[j],
                               ssm_a_log[j], ssm_d[j], ssm_norm_w[j], ssm_w_out[j])
        h = _ffn(h, y, wo, ffn_norm_w[i][None, :], ffn_w_gate[i].astype(BF16), ffn_w_up[i].astype(BF16),
                 ffn_w_down[i].astype(BF16))
    return h.reshape(B, S, D)
```
